```python
import numpy as np
import jax, jax.numpy as jnp
from jax import lax

D_MODEL = 1024
BATCH = 32
SEQ = 2048
DEPTH = 4

NSA_HEADS = 8
HEAD_DIM = 64
NSA_KV_GROUPS = 2
HEADS_PER_GROUP = NSA_HEADS // NSA_KV_GROUPS
NSA_WIDTH = NSA_HEADS * HEAD_DIM
KV_WIDTH = NSA_KV_GROUPS * HEAD_DIM
CMP_LEN = 32
CMP_STRIDE = 16
CMP_HIDDEN = 256
SEL_BLOCK = 64
SEL_TOPN = 8
WINDOW = 512
Q_CHUNK = 64
ATTN_SCALE = HEAD_DIM ** -0.5
POOL_GROUPS = 4
POOL_WINDOWS = (2, 4, 8, 16)
POOL_GROUP_WIDTH = 128
POOL_WIDTH = POOL_GROUPS * POOL_GROUP_WIDTH
N_MERGE = 2
IN_COLS = NSA_WIDTH + 6 * KV_WIDTH + 3 * NSA_HEADS + POOL_WIDTH + N_MERGE * D_MODEL
D_FF = 2816
N_EXPERTS = 8
TOP_K = 2
D_FF_EXPERT = 3584
MOE_ROW_BLOCK = 512
N_DENSE = (DEPTH + 1) // 2
N_MOE = DEPTH // 2
RMS_EPS = 1e-6
NEG_INF = -1e30
FORCE = 1e9

kernel_name = "hybrid_nsa_pool_moe_trunk"


def rms_norm(x, g):
    xf = x.astype(jnp.float32)
    y = xf * lax.rsqrt(jnp.mean(xf * xf, axis=-1, keepdims=True) + RMS_EPS)
    return (y * g.astype(jnp.float32)).astype(x.dtype)


def swiglu(h, w_gate, w_up, w_down):
    return (jax.nn.silu(h @ w_gate) * (h @ w_up)) @ w_down


def compress_tokens(k, pos, w1, w2):
    B, S, G, dh = k.shape
    n_cmp = (S - CMP_LEN) // CMP_STRIDE + 1
    idx = np.arange(n_cmp)[:, None] * CMP_STRIDE + np.arange(CMP_LEN)[None, :]
    blocks = k[:, idx] + pos[:, None, :]
    blocks = blocks.transpose(0, 1, 3, 2, 4).reshape(B, n_cmp, G, CMP_LEN * dh)
    return jax.nn.gelu(blocks @ w1) @ w2


def compressed_attention(q, k_c, v_c):
    S = q.shape[1]
    n_cmp = k_c.shape[1]
    t = jnp.arange(S)
    end = jnp.arange(n_cmp) * CMP_STRIDE + CMP_LEN - 1
    valid = end[None, :] <= t[:, None]
    s = jnp.einsum('bsgrd,bngd->bgrsn', q, k_c).astype(jnp.float32) * ATTN_SCALE
    s = jnp.where(valid, s, NEG_INF)
    p = jax.nn.softmax(s, axis=-1) * valid
    o = jnp.einsum('bgrsn,bngd->bsgrd', p.astype(v_c.dtype), v_c)
    return o, p


def cmp_to_sel_map(n_cmp, n_blk):
    cs = np.arange(n_cmp) * CMP_STRIDE
    ce = cs + CMP_LEN - 1
    bs = np.arange(n_blk) * SEL_BLOCK
    be = bs + SEL_BLOCK - 1
    m = (cs[:, None] <= be[None, :]) & (ce[:, None] >= bs[None, :])
    return jnp.asarray(m.astype(np.float32))


def select_blocks(p_cmp, S):
    n_cmp = p_cmp.shape[-1]
    n_blk = S // SEL_BLOCK
    imp = jnp.einsum('bgrsn,nj->bgsj', p_cmp, cmp_to_sel_map(n_cmp, n_blk))
    t = jnp.arange(S)[:, None]
    blk = jnp.arange(n_blk)[None, :]
    cur = t // SEL_BLOCK
    imp = jnp.where(blk > cur, -FORCE, imp)
    imp = jnp.where((blk == 0) | (blk == cur), FORCE, imp)
    _, idx = lax.top_k(imp, min(SEL_TOPN, n_blk))
    return idx


def selected_and_window_attention(q, k_sel, v_sel, k_win, v_win, sel_idx):
    B, S, G, R, dh = q.shape
    n_blk = S // SEL_BLOCK
    kb = k_sel.reshape(B, n_blk, SEL_BLOCK, G, dh).transpose(0, 3, 1, 2, 4)
    vb = v_sel.reshape(B, n_blk, SEL_BLOCK, G, dh).transpose(0, 3, 1, 2, 4)
    pad = ((0, 0), (WINDOW, 0), (0, 0), (0, 0))
    kw = jnp.pad(k_win, pad)
    vw = jnp.pad(v_win, pad)
    bi = jnp.arange(B)[:, None, None, None]
    gi = jnp.arange(G)[None, :, None, None]
    span = WINDOW + Q_CHUNK

    def chunk(c):
        s0 = c * Q_CHUNK
        qc = lax.dynamic_slice_in_dim(q, s0, Q_CHUNK, axis=1)
        t = s0 + jnp.arange(Q_CHUNK)
        ic = lax.dynamic_slice_in_dim(sel_idx, s0, Q_CHUNK, axis=2)
        ks = kb[bi, gi, ic]
        vs = vb[bi, gi, ic]
        kpos = ic[..., None] * SEL_BLOCK + jnp.arange(SEL_BLOCK)
        m = kpos <= t[None, None, :, None, None]
        sc = jnp.einsum('bqgrd,bgqnkd->bgrqnk', qc, ks).astype(jnp.float32) * ATTN_SCALE
        sc = jnp.where(m[:, :, None], sc, NEG_INF)
        ps = jax.nn.softmax(sc.reshape(B, G, R, Q_CHUNK, -1), axis=-1).reshape(sc.shape)
        o_sel = jnp.einsum('bgrqnk,bgqnkd->bqgrd', ps.astype(vs.dtype), vs)
        kwc = lax.dynamic_slice_in_dim(kw, s0, span, axis=1)
        vwc = lax.dynamic_slice_in_dim(vw, s0, span, axis=1)
        kp = s0 - WINDOW + jnp.arange(span)
        mw = (kp[None, :] <= t[:, None]) & (kp[None, :] > t[:, None] - WINDOW) & (kp[None, :] >= 0)
        sw = jnp.einsum('bqgrd,bkgd->bgrqk', qc, kwc).astype(jnp.float32) * ATTN_SCALE
        sw = jnp.where(mw, sw, NEG_INF)
        pw = jax.nn.softmax(sw, axis=-1)
        o_win = jnp.einsum('bgrqk,bkgd->bqgrd', pw.astype(vwc.dtype), vwc)
        return o_sel, o_win

    o_sel, o_win = lax.map(chunk, jnp.arange(S // Q_CHUNK))
    o_sel = jnp.moveaxis(o_sel, 0, 1).reshape(B, S, G, R, dh)
    o_win = jnp.moveaxis(o_win, 0, 1).reshape(B, S, G, R, dh)
    return o_sel, o_win


def multiscale_pool(u, pool_w, pool_scale):
    B, S, _ = u.shape
    uf = u.astype(jnp.float32).reshape(B, S, POOL_GROUPS, POOL_GROUP_WIDTH)
    csum = jnp.cumsum(uf, axis=1)
    outs = []
    for gi, w in enumerate(POOL_WINDOWS):
        cg = csum[:, :, gi]
        lag = jnp.pad(cg, ((0, 0), (w, 0), (0, 0)))[:, :S]
        cnt = jnp.minimum(jnp.arange(1, S + 1), w).astype(jnp.float32)
        outs.append((cg - lag) / cnt[None, :, None] - uf[:, :, gi])
    pooled = jnp.stack(outs, axis=2).astype(u.dtype)
    mixed = jnp.einsum('bsgc,gcd->bsgd', pooled, pool_w)
    return mixed.reshape(B, S, POOL_WIDTH) * pool_scale


def token_mixer(a, w_in, cmp_pos, cmp_w1, cmp_w2, w_nsa_up, pool_w, pool_scale, w_pool_up, w_out):
    B, S, _ = a.shape
    G, R, dh = NSA_KV_GROUPS, HEADS_PER_GROUP, HEAD_DIM
    sizes = [NSA_WIDTH, 6 * KV_WIDTH, 3 * NSA_HEADS, POOL_WIDTH]
    offsets = np.cumsum(sizes).tolist()
    proj = a @ w_in
    q, kv, gl, u, mg = jnp.split(proj, offsets, axis=-1)
    q = q.reshape(B, S, G, R, dh)
    kv = kv.reshape(B, S, 6, G, dh)
    k_cmp, v_cmp, k_sel, v_sel, k_win, v_win = (kv[:, :, j] for j in range(6))
    kc = compress_tokens(k_cmp, cmp_pos[0], cmp_w1[0], cmp_w2[0])
    vc = compress_tokens(v_cmp, cmp_pos[1], cmp_w1[1], cmp_w2[1])
    o_cmp, p_cmp = compressed_attention(q, kc, vc)
    sel_idx = select_blocks(p_cmp, S)
    o_sel, o_win = selected_and_window_attention(q, k_sel, v_sel, k_win, v_win, sel_idx)
    g = jax.nn.sigmoid(gl).reshape(B, S, G, R, 3)
    o_nsa = (g[..., 0:1] * o_cmp + g[..., 1:2] * o_sel + g[..., 2:3] * o_win).reshape(B, S, NSA_WIDTH)
    y_pool = multiscale_pool(u, pool_w, pool_scale)
    mg = jax.nn.sigmoid(mg).reshape(B, S, N_MERGE, D_MODEL)
    merged = mg[:, :, 0] * (o_nsa @ w_nsa_up) + mg[:, :, 1] * (y_pool @ w_pool_up)
    return merged @ w_out


def moe_swiglu(h, w_router, b_router, w_gate, w_up, w_down):
    T, D = h.shape
    E, K, BLK = N_EXPERTS, TOP_K, MOE_ROW_BLOCK
    logits = (h @ w_router).astype(jnp.float32) + b_router.astype(jnp.float32)
    top_logit, top_e = lax.top_k(logits, K)
    gates = jax.nn.softmax(top_logit, axis=-1)
    flat_e = top_e.reshape(-1)
    flat_tok = jnp.repeat(jnp.arange(T, dtype=jnp.int32), K)
    order = jnp.argsort(flat_e)
    e_s = flat_e[order]
    tok_s = flat_tok[order]
    g_s = gates.reshape(-1)[order]
    counts = jnp.bincount(flat_e, length=E)
    padded = (counts + BLK - 1) // BLK * BLK
    pad_end = jnp.cumsum(padded)
    pad_start = pad_end - padded
    grp_start = jnp.cumsum(counts) - counts
    dest = pad_start[e_s] + jnp.arange(T * K) - grp_start[e_s]
    n_blocks = -(-(T * K + E * (BLK - 1)) // BLK)
    x_buf = jnp.zeros((n_blocks * BLK, D), h.dtype).at[dest].set(h[tok_s])
    block_start = jnp.arange(n_blocks) * BLK
    block_e = jnp.minimum(jnp.sum(block_start[:, None] >= pad_end[None, :], axis=1), E - 1)

    def expert_block(args):
        xb, e = args
        return swiglu(xb, w_gate[e], w_up[e], w_down[e])

    y_buf = lax.map(expert_block, (x_buf.reshape(n_blocks, BLK, D), block_e))
    y = y_buf.reshape(n_blocks * BLK, D)[dest] * g_s[:, None].astype(h.dtype)
    return jax.ops.segment_sum(y, tok_s, num_segments=T)


def setup_inputs(seed: int = 0) -> dict:
    key = jax.random.key(seed)
    ks = jax.random.split(key, 22)

    def nrm(k, shape, fan_in):
        return jax.random.normal(k, shape, jnp.float32) * (fan_in ** -0.5)

    def gain(k, shape):
        return 1.0 + 0.05 * jax.random.normal(k, shape, jnp.float32)

    return {
        "x": jax.random.normal(ks[0], (BATCH, SEQ, D_MODEL), jnp.float32),
        "norm_mix_g": gain(ks[1], (DEPTH, D_MODEL)),
        "w_in": nrm(ks[2], (DEPTH, D_MODEL, IN_COLS), D_MODEL),
        "cmp_pos": 0.1 * jax.random.normal(ks[3], (DEPTH, 2, CMP_LEN, HEAD_DIM), jnp.float32),
        "cmp_w1": nrm(ks[4], (DEPTH, 2, CMP_LEN * HEAD_DIM, CMP_HIDDEN), CMP_LEN * HEAD_DIM),
        "cmp_w2": nrm(ks[5], (DEPTH, 2, CMP_HIDDEN, HEAD_DIM), CMP_HIDDEN),
        "w_nsa_up": nrm(ks[6], (DEPTH, NSA_WIDTH, D_MODEL), NSA_WIDTH),
        "pool_w": nrm(ks[7], (DEPTH, POOL_GROUPS, POOL_GROUP_WIDTH, POOL_GROUP_WIDTH), POOL_GROUP_WIDTH),
        "pool_scale": 1.0 + 0.1 * jax.random.normal(ks[8], (DEPTH, POOL_WIDTH), jnp.float32),
        "w_pool_up": nrm(ks[9], (DEPTH, POOL_WIDTH, D_MODEL), POOL_WIDTH),
        "w_out": nrm(ks[10], (DEPTH, D_MODEL, D_MODEL), D_MODEL),
        "norm_ffn_g": gain(ks[11], (DEPTH, D_MODEL)),
        "ffn_w_gate": nrm(ks[12], (N_DENSE, D_MODEL, D_FF), D_MODEL),
        "ffn_w_up": nrm(ks[13], (N_DENSE, D_MODEL, D_FF), D_MODEL),
        "ffn_w_down": nrm(ks[14], (N_DENSE, D_FF, D_MODEL), D_FF),
        "moe_w_router": nrm(ks[15], (N_MOE, D_MODEL, N_EXPERTS), D_MODEL),
        "moe_b_router": 0.01 * jax.random.normal(ks[16], (N_MOE, N_EXPERTS), jnp.float32),
        "moe_w_gate": nrm(ks[17], (N_MOE, N_EXPERTS, D_MODEL, D_FF_EXPERT), D_MODEL),
        "moe_w_up": nrm(ks[18], (N_MOE, N_EXPERTS, D_MODEL, D_FF_EXPERT), D_MODEL),
        "moe_w_down": nrm(ks[19], (N_MOE, N_EXPERTS, D_FF_EXPERT, D_MODEL), D_FF_EXPERT),
        "final_norm_g": gain(ks[20], (D_MODEL,)),
    }


def reference(x, norm_mix_g, w_in, cmp_pos, cmp_w1, cmp_w2, w_nsa_up, pool_w, pool_scale, w_pool_up,
              w_out, norm_ffn_g, ffn_w_gate, ffn_w_up, ffn_w_down, moe_w_router, moe_b_router,
              moe_w_gate, moe_w_up, moe_w_down, final_norm_g):
    B, S, D = x.shape
    h = x
    for i in range(DEPTH):
        a = rms_norm(h, norm_mix_g[i])
        h = h + token_mixer(a, w_in[i], cmp_pos[i], cmp_w1[i], cmp_w2[i], w_nsa_up[i],
                            pool_w[i], pool_scale[i], w_pool_up[i], w_out[i])
        f = rms_norm(h, norm_ffn_g[i])
        j = i // 2
        if i % 2 == 0:
            h = h + swiglu(f, ffn_w_gate[j], ffn_w_up[j], ffn_w_down[j])
        else:
            h = h + moe_swiglu(f.reshape(B * S, D), moe_w_router[j], moe_b_router[j],
                               moe_w_gate[j], moe_w_up[j], moe_w_down[j]).reshape(B, S, D)
    return rms_norm(h, final_norm_g)
```

```python
import functools

import numpy as np
import jax
import jax.numpy as jnp
from jax import lax
from jax.experimental import pallas as pl
from jax.experimental.pallas import tpu as pltpu

D_MODEL = 1024
NSA_HEADS = 8
HEAD_DIM = 64
KV_GROUPS = 2
HEADS_PER_GROUP = NSA_HEADS // KV_GROUPS
NSA_WIDTH = NSA_HEADS * HEAD_DIM
KV_WIDTH = KV_GROUPS * HEAD_DIM
CMP_LEN = 32
CMP_STRIDE = 16
CMP_HIDDEN = 256
SEL_BLOCK = 64
SEL_TOPN = 8
WINDOW = 512
ATTN_SCALE = HEAD_DIM ** -0.5
POOL_WINDOWS = (2, 4, 8, 16)
POOL_GROUP_WIDTH = 128
POOL_WIDTH = 512
N_EXPERTS = 8
D_FF_EXPERT = 3584
RMS_EPS = 1e-6
NEG_INF = -1e30
FORCE = 1e9

LANES = 128
VMEM_LIMIT = 56 * 1024 * 1024

BF16 = jnp.bfloat16
F32 = jnp.float32

_Q0 = 0
_KV0 = NSA_WIDTH
_GL0 = _KV0 + 6 * KV_WIDTH
_U0 = _GL0 + 3 * NSA_HEADS
_MG0 = _U0 + POOL_WIDTH
_MAIN_COLS = NSA_WIDTH + 6 * KV_WIDTH + POOL_WIDTH + 2 * D_MODEL
_GATES_PER_GROUP = 3 * HEADS_PER_GROUP


def _dot(a, b, **kw):
    return jnp.dot(a, b, preferred_element_type=F32, **kw)


def _dot_tn(a, b):
    return lax.dot_general(a, b, (((0,), (0,)), ((), ())), preferred_element_type=F32)


def _dot_nt(a, b):
    return lax.dot_general(a, b, (((1,), (1,)), ((), ())), preferred_element_type=F32)


def _rms(x, g):
    return x * lax.rsqrt(jnp.mean(x * x, axis=-1, keepdims=True) + RMS_EPS) * g


def _params(*sem):
    return pltpu.CompilerParams(dimension_semantics=sem, vmem_limit_bytes=VMEM_LIMIT)


def _const_spec(shape):
    nd = len(shape)
    return pl.BlockSpec(shape, lambda *_: (0,) * nd, pipeline_mode=pl.Buffered(1))


def _in_kernel(h_ref, g_ref, w_ref, wgl_ref, q_ref, kv_ref, u_ref, mg_ref, gl_ref):
    a = _rms(h_ref[...], g_ref[...]).astype(BF16)
    q = _dot(a, w_ref[:, 0:NSA_WIDTH]) * ATTN_SCALE
    for h in range(NSA_HEADS):
        q_ref[h] = q[:, h * HEAD_DIM:(h + 1) * HEAD_DIM].astype(BF16)
    for c in range(3):
        c0 = NSA_WIDTH + c * 256
        kv = _dot(a, w_ref[:, c0:c0 + 256])
        for i in range(4):
            kv_ref[c * 4 + i] = kv[:, i * HEAD_DIM:(i + 1) * HEAD_DIM].astype(BF16)
    u0 = NSA_WIDTH + 6 * KV_WIDTH
    u_ref[...] = _dot(a, w_ref[:, u0:u0 + POOL_WIDTH])
    m0 = u0 + POOL_WIDTH
    for c in range(4):
        mg = _dot(a, w_ref[:, m0 + c * 512:m0 + (c + 1) * 512])
        mg_ref[:, c * 512:(c + 1) * 512] = jax.nn.sigmoid(mg).astype(BF16)
    gl = jax.nn.sigmoid(_dot(a, wgl_ref[...]))
    for g in range(KV_GROUPS):
        gl_ref[g] = gl[:, g * LANES:(g + 1) * LANES]


def _in_proj(h, g, w_main, w_gl, tm=512):
    T = h.shape[0]
    return pl.pallas_call(
        _in_kernel,
        grid=(T // tm,),
        in_specs=[
            pl.BlockSpec((tm, D_MODEL), lambda i: (i, 0)),
            _const_spec((1, D_MODEL)),
            _const_spec((D_MODEL, _MAIN_COLS)),
            _const_spec((D_MODEL, KV_GROUPS * LANES)),
        ],
        out_specs=[
            pl.BlockSpec((NSA_HEADS, tm, HEAD_DIM), lambda i: (0, i, 0)),
            pl.BlockSpec((12, tm, HEAD_DIM), lambda i: (0, i, 0)),
            pl.BlockSpec((tm, POOL_WIDTH), lambda i: (i, 0)),
            pl.BlockSpec((tm, 2 * D_MODEL), lambda i: (i, 0)),
            pl.BlockSpec((KV_GROUPS, tm, LANES), lambda i: (0, i, 0)),
        ],
        out_shape=[
            jax.ShapeDtypeStruct((NSA_HEADS, T, HEAD_DIM), BF16),
            jax.ShapeDtypeStruct((12, T, HEAD_DIM), BF16),
            jax.ShapeDtypeStruct((T, POOL_WIDTH), F32),
            jax.ShapeDtypeStruct((T, 2 * D_MODEL), BF16),
            jax.ShapeDtypeStruct((KV_GROUPS, T, LANES), F32),
        ],
        compiler_params=_params("parallel"),
        name="in_proj",
    )(h, g, w_main, w_gl)


def _cmp_kernel(x_ref, pos_ref, w1_ref, w2_ref, kc_ref, vc_ref):
    n_chunks = x_ref.shape[1]
    for j in range(2):
        out_ref = kc_ref if j == 0 else vc_ref
        for g in range(KV_GROUPS):
            x = x_ref[j * KV_GROUPS + g].astype(F32)
            lo = (x + pos_ref[j, 0:1, :]).astype(BF16)
            hi = (x + pos_ref[j, 1:2, :]).astype(BF16)
            hid = _dot(lo, w1_ref[j, 0]) + pltpu.roll(_dot(hi, w1_ref[j, 1]), n_chunks - 1, axis=0)
            act = jax.nn.gelu(hid).astype(BF16)
            out_ref[0, g] = _dot(act, w2_ref[j]).astype(BF16)


def _compress(kv12, pos, w1, w2, B, S):
    T = B * S
    nck = S // CMP_STRIDE
    half = CMP_STRIDE * HEAD_DIM
    x = kv12.reshape(12, T // CMP_STRIDE, half)
    out = jax.ShapeDtypeStruct((B, KV_GROUPS, nck, HEAD_DIM), BF16)
    return pl.pallas_call(
        _cmp_kernel,
        grid=(B,),
        in_specs=[
            pl.BlockSpec((4, nck, half), lambda b: (0, b, 0)),
            _const_spec((2, 2, half)),
            _const_spec((2, 2, half, CMP_HIDDEN)),
            _const_spec((2, CMP_HIDDEN, HEAD_DIM)),
        ],
        out_specs=[pl.BlockSpec((1, KV_GROUPS, nck, HEAD_DIM), lambda b: (b, 0, 0, 0))] * 2,
        out_shape=[out, out],
        compiler_params=_params("parallel"),
        name="compress",
    )(x, pos, w1, w2)


def _sel_kernel(q_ref, kc_ref, vc_ref, map_ref, ocmp_ref, sel_ref, *, tq, n_blk):
    i = pl.program_id(2)
    kc = kc_ref[0, 0]
    vc = vc_ref[0, 0]
    nck = kc.shape[0]
    n_idx = lax.broadcasted_iota(jnp.int32, (nck, tq), 0)
    t_idx = i * tq + lax.broadcasted_iota(jnp.int32, (nck, tq), 1)
    valid = n_idx * CMP_STRIDE + (CMP_LEN - 1) <= t_idx
    psum = jnp.zeros((nck, tq), F32)
    for r in range(HEADS_PER_GROUP):
        s = jnp.where(valid, _dot_nt(kc, q_ref[r]), NEG_INF)
        e = jnp.exp(s - jnp.max(s, axis=0, keepdims=True))
        p = jnp.where(valid, e / jnp.sum(e, axis=0, keepdims=True), 0.0)
        ocmp_ref[r] = _dot_tn(p.astype(BF16), vc)
        psum = psum + p
    imp = _dot(map_ref[...], psum, precision=lax.Precision.HIGHEST)
    blk = lax.broadcasted_iota(jnp.int32, (n_blk, tq), 0)
    cur = (i * tq + lax.broadcasted_iota(jnp.int32, (n_blk, tq), 1)) // SEL_BLOCK
    imp = jnp.where(blk > cur, -FORCE, imp)
    imp = jnp.where((blk == 0) | (blk == cur), FORCE, imp)
    rank = jnp.zeros((n_blk, tq), F32)
    for j in range(n_blk):
        row = imp[j:j + 1, :]
        ahead = (row > imp) | ((row == imp) & (blk > j))
        rank = rank + jnp.where(ahead, 1.0, 0.0)
    sel_ref[0, 0] = jnp.where(rank < SEL_TOPN, 1.0, 0.0).astype(BF16)


def _cmp_select(q8, kc, vc, cmp_map, B, S, tq=512):
    T = B * S
    n_blk = S // SEL_BLOCK
    nck = S // CMP_STRIDE
    nq = S // tq
    hp = HEADS_PER_GROUP
    return pl.pallas_call(
        functools.partial(_sel_kernel, tq=tq, n_blk=n_blk),
        grid=(B, KV_GROUPS, nq),
        in_specs=[
            pl.BlockSpec((hp, tq, HEAD_DIM), lambda b, g, i: (g, b * nq + i, 0)),
            pl.BlockSpec((1, 1, nck, HEAD_DIM), lambda b, g, i: (b, g, 0, 0)),
            pl.BlockSpec((1, 1, nck, HEAD_DIM), lambda b, g, i: (b, g, 0, 0)),
            _const_spec((n_blk, nck)),
        ],
        out_specs=[
            pl.BlockSpec((hp, tq, HEAD_DIM), lambda b, g, i: (g, b * nq + i, 0)),
            pl.BlockSpec((1, 1, n_blk, tq), lambda b, g, i: (b, g, 0, i)),
        ],
        out_shape=[
            jax.ShapeDtypeStruct((NSA_HEADS, T, HEAD_DIM), F32),
            jax.ShapeDtypeStruct((B, KV_GROUPS, n_blk, S), BF16),
        ],
        compiler_params=_params("parallel", "parallel", "parallel"),
        name="cmp_select",
    )(q8, kc, vc, cmp_map)


def _attn_kernel(q_ref, ks_ref, vs_ref, kw_ref, vw_ref, sel_ref, ocmp_ref, gl_ref, exp_ref, o_ref,
                 mask_sc, m_sc, l_sc, acc_sc, *, tq, tk, S):
    i = pl.program_id(2)
    hp = HEADS_PER_GROUP
    q4 = q_ref[...].reshape(hp * tq, HEAD_DIM)
    t_row = i * tq + lax.broadcasted_iota(jnp.int32, (tq, 1), 0)

    mask_sc[...] = _dot_tn(sel_ref[0, 0], exp_ref[...])
    m_sc[...] = jnp.full(m_sc.shape, NEG_INF, F32)
    l_sc[...] = jnp.zeros(l_sc.shape, F32)
    acc_sc[...] = jnp.zeros(acc_sc.shape, F32)

    def sel_step(j, carry):
        k0 = pl.multiple_of(j * tk, tk)
        k = ks_ref[0, pl.ds(k0, tk), :]
        v = vs_ref[0, pl.ds(k0, tk), :]
        kpos = k0 + lax.broadcasted_iota(jnp.int32, (1, tk), 1)
        ok = (mask_sc[:, pl.ds(k0, tk)] > 0.5) & (kpos <= t_row)
        bias = jnp.where(ok, 0.0, NEG_INF)
        s = _dot_nt(q4, k).reshape(hp, tq, tk) + bias[None]
        m_old = m_sc[...]
        m_new = jnp.maximum(m_old, jnp.max(s, axis=-1, keepdims=True))
        alpha = jnp.exp(m_old - m_new)
        p = jnp.exp(s - m_new)
        l_sc[...] = alpha * l_sc[...] + jnp.sum(p, axis=-1, keepdims=True)
        pv = _dot(p.reshape(hp * tq, tk).astype(BF16), v).reshape(hp, tq, HEAD_DIM)
        acc_sc[...] = alpha * acc_sc[...] + pv
        m_sc[...] = m_new
        return carry

    n_sel = (i * tq + tq + tk - 1) // tk
    lax.fori_loop(0, n_sel, sel_step, 0)
    o_sel = acc_sc[...] / l_sc[...]

    span = WINDOW + tq
    w0 = pl.multiple_of(jnp.clip(i * tq - WINDOW, 0, S - span), LANES)
    kw = kw_ref[0, pl.ds(w0, span), :]
    vw = vw_ref[0, pl.ds(w0, span), :]
    kpos = w0 + lax.broadcasted_iota(jnp.int32, (1, span), 1)
    ok = (kpos <= t_row) & (kpos > t_row - WINDOW)
    bias = jnp.where(ok, 0.0, NEG_INF)
    s = _dot_nt(q4, kw).reshape(hp, tq, span) + bias[None]
    p = jnp.exp(s - jnp.max(s, axis=-1, keepdims=True))
    lw = jnp.sum(p, axis=-1, keepdims=True)
    o_win = _dot(p.reshape(hp * tq, span).astype(BF16), vw).reshape(hp, tq, HEAD_DIM) / lw

    gl = gl_ref[0]
    outs = []
    for r in range(hp):
        outs.append(gl[:, 3 * r:3 * r + 1] * ocmp_ref[r]
                    + gl[:, 3 * r + 1:3 * r + 2] * o_sel[r]
                    + gl[:, 3 * r + 2:3 * r + 3] * o_win[r])
    o_ref[...] = jnp.concatenate(outs, axis=-1).astype(BF16)


def _sparse_attention(q8, kv12, sel, ocmp, gls, expand, B, S, tq=128, tk=512):
    T = B * S
    nq = S // tq
    hp = HEADS_PER_GROUP
    n_blk = S // SEL_BLOCK

    def kv_spec(branch):
        return pl.BlockSpec((1, S, HEAD_DIM), lambda b, g, i: (2 * branch + g, b, 0))

    return pl.pallas_call(
        functools.partial(_attn_kernel, tq=tq, tk=tk, S=S),
        grid=(B, KV_GROUPS, nq),
        in_specs=[
            pl.BlockSpec((hp, tq, HEAD_DIM), lambda b, g, i: (g, b * nq + i, 0)),
            kv_spec(2), kv_spec(3), kv_spec(4), kv_spec(5),
            pl.BlockSpec((1, 1, n_blk, tq), lambda b, g, i: (b, g, 0, i)),
            pl.BlockSpec((hp, tq, HEAD_DIM), lambda b, g, i: (g, b * nq + i, 0)),
            pl.BlockSpec((1, tq, LANES), lambda b, g, i: (g, b * nq + i, 0)),
            _const_spec((n_blk, S)),
        ],
        out_specs=pl.BlockSpec((tq, hp * HEAD_DIM), lambda b, g, i: (b * nq + i, g)),
        out_shape=jax.ShapeDtypeStruct((T, NSA_WIDTH), BF16),
        scratch_shapes=[
            pltpu.VMEM((tq, S), F32),
            pltpu.VMEM((hp, tq, 1), F32),
            pltpu.VMEM((hp, tq, 1), F32),
            pltpu.VMEM((hp, tq, HEAD_DIM), F32),
        ],
        compiler_params=_params("parallel", "parallel", "parallel"),
        name="sparse_attention",
    )(q8, kv12, kv12, kv12, kv12, sel, ocmp, gls, expand)


_POOL_HALO = 16


def _pool_kernel(u_ref, halo_ref, w_ref, scale_ref, y_ref, *, tp, S):
    i = pl.program_id(0)
    pos0 = (i * tp) % S
    halo = jnp.where(pos0 == 0, 0.0, halo_ref[...])
    x = jnp.concatenate([halo, u_ref[...]], axis=0)
    pos = pos0 + lax.broadcasted_iota(jnp.int32, (tp, 1), 0)
    for gi, w in enumerate(POOL_WINDOWS):
        c0 = gi * POOL_GROUP_WIDTH
        xg = x[:, c0:c0 + POOL_GROUP_WIDTH]
        tot = xg[_POOL_HALO:_POOL_HALO + tp]
        for d in range(1, w):
            tot = tot + xg[_POOL_HALO - d:_POOL_HALO - d + tp]
        cnt = jnp.minimum(pos + 1, w).astype(F32)
        pooled = tot / cnt - xg[_POOL_HALO:_POOL_HALO + tp]
        mixed = _dot(pooled.astype(BF16), w_ref[gi])
        y_ref[:, c0:c0 + POOL_GROUP_WIDTH] = (mixed * scale_ref[:, c0:c0 + POOL_GROUP_WIDTH]).astype(BF16)


def _pool_mixer(u, pool_w, pool_scale, S, tp=512):
    T = u.shape[0]
    hb = tp // _POOL_HALO
    return pl.pallas_call(
        functools.partial(_pool_kernel, tp=tp, S=S),
        grid=(T // tp,),
        in_specs=[
            pl.BlockSpec((tp, POOL_WIDTH), lambda i: (i, 0)),
            pl.BlockSpec((_POOL_HALO, POOL_WIDTH), lambda i: (jnp.maximum(i * hb - 1, 0), 0)),
            _const_spec((4, POOL_GROUP_WIDTH, POOL_GROUP_WIDTH)),
            _const_spec((1, POOL_WIDTH)),
        ],
        out_specs=pl.BlockSpec((tp, POOL_WIDTH), lambda i: (i, 0)),
        out_shape=jax.ShapeDtypeStruct((T, POOL_WIDTH), BF16),
        compiler_params=_params("parallel"),
        name="pool_mixer",
    )(u, u, pool_w, pool_scale)


def _merge_residual(onsa_ref, ypool_ref, mg_ref, h_ref, wn_ref, wp_ref, wo_ref):
    a = _dot(onsa_ref[...], wn_ref[...])
    b = _dot(ypool_ref[...], wp_ref[...])
    mg = mg_ref[...].astype(F32)
    merged = mg[:, :D_MODEL] * a + mg[:, D_MODEL:] * b
    return h_ref[...] + _dot(merged.astype(BF16), wo_ref[...])


def _post_dense_kernel(onsa_ref, ypool_ref, mg_ref, h_ref, wn_ref, wp_ref, wo_ref, g_ref, h1_ref, f_ref):
    h1 = _merge_residual(onsa_ref, ypool_ref, mg_ref, h_ref, wn_ref, wp_ref, wo_ref)
    h1_ref[...] = h1
    f_ref[...] = _rms(h1, g_ref[...]).astype(BF16)


def _post_moe_kernel(onsa_ref, ypool_ref, mg_ref, h_ref, wn_ref, wp_ref, wo_ref, g_ref, wr_ref, br_ref,
                     h1_ref, f_ref, route_ref, cnt_ref, base_sc, *, tm):
    @pl.when(pl.program_id(0) == 0)
    def _():
        base_sc[...] = jnp.zeros(base_sc.shape, F32)

    h1 = _merge_residual(onsa_ref, ypool_ref, mg_ref, h_ref, wn_ref, wp_ref, wo_ref)
    h1_ref[...] = h1
    f = _rms(h1, g_ref[...])
    f_ref[...] = f

    lane = lax.broadcasted_iota(jnp.int32, (tm, LANES), 1)
    logits = _dot(f, wr_ref[...], precision=lax.Precision.HIGHEST) + br_ref[...]
    lowest = jnp.finfo(F32).min
    lane_f = lane.astype(F32)
    lg = jnp.where(lane < N_EXPERTS, logits, lowest)
    m1 = jnp.max(lg, axis=-1, keepdims=True)
    i1 = jnp.min(jnp.where(lg == m1, lane_f, float(LANES)), axis=-1, keepdims=True)
    lg2 = jnp.where(lane_f == i1, lowest, lg)
    m2 = jnp.max(lg2, axis=-1, keepdims=True)
    i2 = jnp.min(jnp.where(lg2 == m2, lane_f, float(LANES)), axis=-1, keepdims=True)
    e2 = jnp.exp(m2 - m1)
    gate1 = 1.0 / (1.0 + e2)
    gate2 = e2 / (1.0 + e2)

    oh1 = lane_f == i1
    oh2 = lane_f == i2
    both = jnp.where(oh1 | oh2, 1.0, 0.0)
    r_i = lax.broadcasted_iota(jnp.int32, (tm, tm), 0)
    c_i = lax.broadcasted_iota(jnp.int32, (tm, tm), 1)
    before = jnp.where(r_i > c_i, 1.0, 0.0).astype(BF16)
    seen = _dot(before, both.astype(BF16)) + base_sc[...]
    rank1 = jnp.sum(jnp.where(oh1, seen, 0.0), axis=-1, keepdims=True)
    rank2 = jnp.sum(jnp.where(oh2, seen, 0.0), axis=-1, keepdims=True)
    base_sc[...] = base_sc[...] + jnp.sum(both, axis=0, keepdims=True)
    cnt_ref[...] = base_sc[...]

    route = jnp.zeros((tm, LANES), F32)
    for col, val in enumerate((i1, i2, gate1, gate2, rank1, rank2)):
        route = jnp.where(lane == col, val, route)
    route_ref[...] = route


def _post_mixer(onsa, ypool, mgs, h, wn, wp, wo, g_ffn, router=None, tm=512):
    T = h.shape[0]
    tile = lambda w: pl.BlockSpec((tm, w), lambda i: (i, 0))
    in_specs = [
        tile(NSA_WIDTH), tile(POOL_WIDTH), tile(2 * D_MODEL), tile(D_MODEL),
        _const_spec((NSA_WIDTH, D_MODEL)), _const_spec((POOL_WIDTH, D_MODEL)),
        _const_spec((D_MODEL, D_MODEL)), _const_spec((1, D_MODEL)),
    ]
    args = [onsa, ypool, mgs, h, wn, wp, wo, g_ffn]
    if router is None:
        return pl.pallas_call(
            _post_dense_kernel,
            grid=(T // tm,),
            in_specs=in_specs,
            out_specs=[tile(D_MODEL), tile(D_MODEL)],
            out_shape=[jax.ShapeDtypeStruct((T, D_MODEL), F32), jax.ShapeDtypeStruct((T, D_MODEL), BF16)],
            compiler_params=_params("parallel"),
            name="post_mixer_dense",
        )(*args)
    w_router, b_router = router
    return pl.pallas_call(
        functools.partial(_post_moe_kernel, tm=tm),
        grid=(T // tm,),
        in_specs=in_specs + [_const_spec((D_MODEL, LANES)), _const_spec((1, LANES))],
        out_specs=[tile(D_MODEL), tile(D_MODEL), tile(LANES), pl.BlockSpec((1, LANES), lambda i: (0, 0))],
        out_shape=[
            jax.ShapeDtypeStruct((T, D_MODEL), F32),
            jax.ShapeDtypeStruct((T, D_MODEL), F32),
            jax.ShapeDtypeStruct((T, LANES), F32),
            jax.ShapeDtypeStruct((1, LANES), F32),
        ],
        scratch_shapes=[pltpu.VMEM((1, LANES), F32)],
        compiler_params=_params("arbitrary"),
        name="post_mixer_moe",
    )(*args, w_router, b_router)


def _ffn_kernel(f_ref, h_ref, wg_ref, wu_ref, wd_ref, o_ref, *, fc):
    f = f_ref[...]
    o_ref[...] = h_ref[...]
    for c in range(wg_ref.shape[1] // fc):
        g = _dot(f, wg_ref[:, c * fc:(c + 1) * fc])
        u = _dot(f, wu_ref[:, c * fc:(c + 1) * fc])
        o_ref[...] += _dot((jax.nn.silu(g) * u).astype(BF16), wd_ref[c * fc:(c + 1) * fc, :])


def _dense_ffn(f, h1, wg, wu, wd, tm=512, fc=256):
    T = f.shape[0]
    d_ff = wg.shape[1]
    tile = pl.BlockSpec((tm, D_MODEL), lambda i: (i, 0))
    return pl.pallas_call(
        functools.partial(_ffn_kernel, fc=fc),
        grid=(T // tm,),
        in_specs=[tile, tile, _const_spec((D_MODEL, d_ff)), _const_spec((D_MODEL, d_ff)),
                  _const_spec((d_ff, D_MODEL))],
        out_specs=tile,
        out_shape=jax.ShapeDtypeStruct((T, D_MODEL), F32),
        compiler_params=_params("parallel"),
        name="dense_ffn",
    )(f, h1, wg, wu, wd)


def _dispatch_kernel(dest_ref, f_ref, x_hbm, sem, *, tg):
    def row_copy(r, k):
        return pltpu.make_async_copy(f_ref.at[pl.ds(r, 1)], x_hbm.at[pl.ds(dest_ref[0, 0, 2 * r + k], 1)], sem)

    def start(r, carry):
        row_copy(r, 0).start()
        row_copy(r, 1).start()
        return carry

    def wait(r, carry):
        row_copy(r, 0).wait()
        row_copy(r, 1).wait()
        return carry

    lax.fori_loop(0, tg, start, 0)
    lax.fori_loop(0, tg, wait, 0)


def _dispatch(f, dest, tg=256):
    T = f.shape[0]
    return pl.pallas_call(
        functools.partial(_dispatch_kernel, tg=tg),
        grid=(T // tg,),
        in_specs=[
            pl.BlockSpec((1, 1, 2 * tg), lambda i: (i, 0, 0), memory_space=pltpu.SMEM),
            pl.BlockSpec((tg, D_MODEL), lambda i: (i, 0)),
        ],
        out_specs=pl.BlockSpec(memory_space=pl.ANY),
        out_shape=jax.ShapeDtypeStruct((2 * T, D_MODEL), F32),
        scratch_shapes=[pltpu.SemaphoreType.DMA(())],
        compiler_params=_params("arbitrary"),
        name="moe_dispatch",
    )(dest.reshape(T // tg, 1, 2 * tg), f)


def _combine_kernel(dest_ref, h_ref, route_ref, g_ref, y_hbm, o_ref, z_sc, sem, *, tg, final_norm):
    def row_copy(r, k):
        return pltpu.make_async_copy(y_hbm.at[pl.ds(dest_ref[0, 0, 2 * r + k], 1)], z_sc.at[k, pl.ds(r, 1)], sem)

    def start(r, carry):
        row_copy(r, 0).start()
        row_copy(r, 1).start()
        return carry

    def wait(r, carry):
        row_copy(r, 0).wait()
        row_copy(r, 1).wait()
        return carry

    lax.fori_loop(0, tg, start, 0)
    lax.fori_loop(0, tg, wait, 0)
    route = route_ref[...]
    out = h_ref[...] + (route[:, 2:3] * z_sc[0] + route[:, 3:4] * z_sc[1])
    if final_norm:
        out = _rms(out, g_ref[...])
    o_ref[...] = out


def _combine(h1, route, dest, y_sorted, g_final, final_norm, tg=256):
    T = h1.shape[0]
    return pl.pallas_call(
        functools.partial(_combine_kernel, tg=tg, final_norm=final_norm),
        grid=(T // tg,),
        in_specs=[
            pl.BlockSpec((1, 1, 2 * tg), lambda i: (i, 0, 0), memory_space=pltpu.SMEM),
            pl.BlockSpec((tg, D_MODEL), lambda i: (i, 0)),
            pl.BlockSpec((tg, LANES), lambda i: (i, 0)),
            _const_spec((1, D_MODEL)),
            pl.BlockSpec(memory_space=pl.ANY),
        ],
        out_specs=pl.BlockSpec((tg, D_MODEL), lambda i: (i, 0)),
        out_shape=jax.ShapeDtypeStruct((T, D_MODEL), F32),
        scratch_shapes=[pltpu.VMEM((2, tg, D_MODEL), F32), pltpu.SemaphoreType.DMA(())],
        compiler_params=_params("arbitrary"),
        name="moe_combine",
    )(dest.reshape(T // tg, 1, 2 * tg), h1, route, g_final, y_sorted)


def _moe_kernel(blk_ref, exp_ref, lo_ref, hi_ref, first_ref, x_ref, wg_ref, wu_ref, wd_ref, y_ref, acc_sc, *, bm):
    v = pl.program_id(0)
    c = pl.program_id(1)

    @pl.when((first_ref[v] == 1) & (c == 0))
    def _():
        acc_sc[...] = jnp.zeros(acc_sc.shape, F32)

    lo = lo_ref[v]
    hi = hi_ref[v]

    @pl.when(hi > lo)
    def _():
        row = blk_ref[v] * bm + lax.broadcasted_iota(jnp.int32, (bm, 1), 0)
        x = jnp.where((row >= lo) & (row < hi), x_ref[...], 0.0).astype(BF16)
        g = _dot(x, wg_ref[0])
        u = _dot(x, wu_ref[0])
        acc_sc[...] += _dot((jax.nn.silu(g) * u).astype(BF16), wd_ref[0])

    @pl.when(c == pl.num_programs(1) - 1)
    def _():
        y_ref[...] = acc_sc[...]


def _grouped_ffn(x_sorted, visits, wg, wu, wd, bm=1024, fc=512):
    N = x_sorted.shape[0]
    n_visits = visits[0].shape[0]
    d_ff = wg.shape[2]
    grid_spec = pltpu.PrefetchScalarGridSpec(
        num_scalar_prefetch=5,
        grid=(n_visits, d_ff // fc),
        in_specs=[
            pl.BlockSpec((bm, D_MODEL), lambda v, c, blk, ex, lo, hi, fi: (blk[v], 0)),
            pl.BlockSpec((1, D_MODEL, fc), lambda v, c, blk, ex, lo, hi, fi: (ex[v], 0, c)),
            pl.BlockSpec((1, D_MODEL, fc), lambda v, c, blk, ex, lo, hi, fi: (ex[v], 0, c)),
            pl.BlockSpec((1, fc, D_MODEL), lambda v, c, blk, ex, lo, hi, fi: (ex[v], c, 0)),
        ],
        out_specs=pl.BlockSpec((bm, D_MODEL), lambda v, c, blk, ex, lo, hi, fi: (blk[v], 0)),
        scratch_shapes=[pltpu.VMEM((bm, D_MODEL), F32)],
    )
    return pl.pallas_call(
        functools.partial(_moe_kernel, bm=bm),
        grid_spec=grid_spec,
        out_shape=jax.ShapeDtypeStruct((N, D_MODEL), F32),
        compiler_params=_params("arbitrary", "arbitrary"),
        name="grouped_ffn",
    )(*visits, x_sorted, wg, wu, wd)


def _routing_tables(route, counts, n_rows, bm):
    counts = counts[0, :N_EXPERTS].astype(jnp.int32)
    ends = jnp.cumsum(counts)
    starts = ends - counts
    e = route[:, 0:2].astype(jnp.int32)
    seg_start = jnp.sum(jnp.where(e[..., None] == jnp.arange(N_EXPERTS), starts, 0), axis=-1)
    dest = seg_start + route[:, 4:6].astype(jnp.int32)
    n_blocks = n_rows // bm
    cuts = jnp.sort(jnp.concatenate([jnp.arange(n_blocks, dtype=jnp.int32) * bm, starts[1:]]))
    lo = cuts
    hi = jnp.concatenate([cuts[1:], jnp.array([n_rows], jnp.int32)])
    blk = jnp.minimum(lo // bm, n_blocks - 1)
    ex = jnp.minimum(jnp.sum(ends[None, :] <= lo[:, None], axis=1), N_EXPERTS - 1).astype(jnp.int32)
    first = jnp.concatenate([jnp.ones((1,), jnp.int32), (blk[1:] != blk[:-1]).astype(jnp.int32)])
    return dest, (blk, ex, lo, hi, first)


def _layer_weights(w_in, cmp_pos, cmp_w1, cmp_w2):
    w_main = jnp.concatenate([w_in[:, _Q0:_GL0], w_in[:, _U0:]], axis=1).astype(BF16)
    gl = w_in[:, _GL0:_U0].reshape(D_MODEL, KV_GROUPS, _GATES_PER_GROUP)
    w_gl = jnp.pad(gl, ((0, 0), (0, 0), (0, LANES - _GATES_PER_GROUP))).reshape(D_MODEL, KV_GROUPS * LANES)
    half = CMP_STRIDE * HEAD_DIM
    pos = cmp_pos.reshape(2, 2, half)
    w1 = cmp_w1.reshape(2, 2, half, CMP_HIDDEN).astype(BF16)
    return w_main, w_gl.astype(BF16), pos, w1, cmp_w2.astype(BF16)


def _static_tables(S):
    n_blk = S // SEL_BLOCK
    n_cmp = (S - CMP_LEN) // CMP_STRIDE + 1
    nck = S // CMP_STRIDE
    cs = np.arange(nck) * CMP_STRIDE
    bs = np.arange(n_blk) * SEL_BLOCK
    overlap = (cs[None, :] <= bs[:, None] + SEL_BLOCK - 1) & (cs[None, :] + CMP_LEN - 1 >= bs[:, None])
    overlap &= (np.arange(nck) < n_cmp)[None, :]
    expand = (np.arange(S)[None, :] // SEL_BLOCK) == np.arange(n_blk)[:, None]
    return jnp.asarray(overlap.astype(np.float32)), jnp.asarray(expand.astype(np.float32), dtype=BF16)


def kernel(x, norm_mix_g, w_in, cmp_pos, cmp_w1, cmp_w2, w_nsa_up, pool_w, pool_scale, w_pool_up, w_out,
           norm_ffn_g, ffn_w_gate, ffn_w_up, ffn_w_down, moe_w_router, moe_b_router, moe_w_gate, moe_w_up,
           moe_w_down, final_norm_g):
    B, S, D = x.shape
    T = B * S
    depth = w_in.shape[0]
    cmp_map, expand = _static_tables(S)
    h = x.reshape(T, D)
    bm = 1024
    for i in range(depth):
        w_main, w_gl, pos, w1, w2 = _layer_weights(w_in[i], cmp_pos[i], cmp_w1[i], cmp_w2[i])
        q8, kv12, u, mgs, gls = _in_proj(h, norm_mix_g[i][None], w_main, w_gl)
        kc, vc = _compress(kv12, pos, w1, w2, B, S)
        ocmp, sel = _cmp_select(q8, kc, vc, cmp_map, B, S)
        onsa = _sparse_attention(q8, kv12, sel, ocmp, gls, expand, B, S)
        ypool = _pool_mixer(u, pool_w[i].astype(BF16), pool_scale[i][None], S)
        mix_w = (w_nsa_up[i].astype(BF16), w_pool_up[i].astype(BF16), w_out[i].astype(BF16), norm_ffn_g[i][None])
        j = i // 2
        if i % 2 == 0:
            h1, f = _post_mixer(onsa, ypool, mgs, h, *mix_w)
            h = _dense_ffn(f, h1, ffn_w_gate[j].astype(BF16), ffn_w_up[j].astype(BF16),
                           ffn_w_down[j].astype(BF16))
        else:
            w_router = jnp.pad(moe_w_router[j], ((0, 0), (0, LANES - N_EXPERTS)))
            b_router = jnp.pad(moe_b_router[j], (0, LANES - N_EXPERTS))[None]
            h1, f, route, counts = _post_mixer(onsa, ypool, mgs, h, *mix_w, router=(w_router, b_router))
            dest, visits = _routing_tables(route, counts, 2 * T, bm)
            x_sorted = _dispatch(f, dest)
            y_sorted = _grouped_ffn(x_sorted, visits, moe_w_gate[j].astype(BF16), moe_w_up[j].astype(BF16),
                                    moe_w_down[j].astype(BF16), bm=bm)
            h = _combine(h1, route, dest, y_sorted, final_norm_g[None], final_norm=(i == depth - 1))
    return h.reshape(B, S, D)
```

```python
import functools

import numpy as np
import jax
import jax.numpy as jnp
from jax import lax
from jax.experimental import pallas as pl
from jax.experimental.pallas import tpu as pltpu
from jax.experimental.pallas import tpu_sc as plsc

D_MODEL = 1024
NSA_HEADS = 8
HEAD_DIM = 64
KV_GROUPS = 2
HEADS_PER_GROUP = NSA_HEADS // KV_GROUPS
NSA_WIDTH = NSA_HEADS * HEAD_DIM
KV_WIDTH = KV_GROUPS * HEAD_DIM
CMP_LEN = 32
CMP_STRIDE = 16
CMP_HIDDEN = 256
SEL_BLOCK = 64
SEL_TOPN = 8
WINDOW = 512
ATTN_SCALE = HEAD_DIM ** -0.5
POOL_WINDOWS = (2, 4, 8, 16)
POOL_GROUP_WIDTH = 128
POOL_WIDTH = 512
N_EXPERTS = 8
D_FF_EXPERT = 3584
RMS_EPS = 1e-6
NEG_INF = -1e30
FORCE = 1e9

LOG2E = 1.4426950408889634
MASK_BIG = 2.0 ** 100
SEL_ROWS = 48

LANES = 128
VMEM_LIMIT = 56 * 1024 * 1024

BF16 = jnp.bfloat16
F32 = jnp.float32

_Q0 = 0
_KV0 = NSA_WIDTH
_GL0 = _KV0 + 6 * KV_WIDTH
_U0 = _GL0 + 3 * NSA_HEADS
_MG0 = _U0 + POOL_WIDTH
_MAIN_COLS = NSA_WIDTH + 6 * KV_WIDTH + POOL_WIDTH + 2 * D_MODEL
_GATES_PER_GROUP = 3 * HEADS_PER_GROUP


def _dot(a, b, **kw):
    return jnp.dot(a, b, preferred_element_type=F32, **kw)


def _dot_tn(a, b):
    return lax.dot_general(a, b, (((0,), (0,)), ((), ())), preferred_element_type=F32)


def _dot_nt(a, b):
    return lax.dot_general(a, b, (((1,), (1,)), ((), ())), preferred_element_type=F32)


def _rms(x, g):
    return x * lax.rsqrt(jnp.mean(x * x, axis=-1, keepdims=True) + RMS_EPS) * g


def _params(*sem):
    return pltpu.CompilerParams(dimension_semantics=sem, vmem_limit_bytes=VMEM_LIMIT)


def _const_spec(shape):
    nd = len(shape)
    return pl.BlockSpec(shape, lambda *_: (0,) * nd, pipeline_mode=pl.Buffered(1))


def _in_kernel(h_ref, g_ref, w_ref, wgl_ref, q_ref, kv_ref, vx_ref, u_ref, mg_ref, gl_ref):
    a = _rms(h_ref[...], g_ref[...]).astype(BF16)
    q = _dot(a, w_ref[:, 0:NSA_WIDTH]) * (ATTN_SCALE * LOG2E)
    for h in range(NSA_HEADS):
        q_ref[h] = q[:, h * HEAD_DIM:(h + 1) * HEAD_DIM].astype(BF16)
    tm = a.shape[0]
    ones_col = jnp.where(lax.broadcasted_iota(jnp.int32, (tm, HEAD_DIM), 1) == 0, 1.0, 0.0)
    for c in range(3):
        c0 = NSA_WIDTH + c * 256
        kv = _dot(a, w_ref[:, c0:c0 + 256])
        pieces = [kv[:, i * HEAD_DIM:(i + 1) * HEAD_DIM] for i in range(4)]
        if c == 0:
            for i in range(4):
                kv_ref[i] = pieces[i].astype(BF16)
        else:
            for g in range(KV_GROUPS):
                kv_ref[2 + 2 * c + g] = pieces[g].astype(BF16)
                vx_ref[2 * (c - 1) + g] = jnp.concatenate([pieces[2 + g], ones_col], axis=-1).astype(BF16)
    u0 = NSA_WIDTH + 6 * KV_WIDTH
    u_ref[...] = _dot(a, w_ref[:, u0:u0 + POOL_WIDTH])
    m0 = u0 + POOL_WIDTH
    for c in range(4):
        mg = _dot(a, w_ref[:, m0 + c * 512:m0 + (c + 1) * 512])
        mg_ref[:, c * 512:(c + 1) * 512] = jax.nn.sigmoid(mg).astype(BF16)
    gl = jax.nn.sigmoid(_dot(a, wgl_ref[...]))
    for g in range(KV_GROUPS):
        gl_ref[g] = gl[:, g * LANES:(g + 1) * LANES]


def _in_proj(h, g, w_main, w_gl, tm=512):
    T = h.shape[0]
    return pl.pallas_call(
        _in_kernel,
        grid=(T // tm,),
        in_specs=[
            pl.BlockSpec((tm, D_MODEL), lambda i: (i, 0)),
            _const_spec((1, D_MODEL)),
            _const_spec((D_MODEL, _MAIN_COLS)),
            _const_spec((D_MODEL, KV_GROUPS * LANES)),
        ],
        out_specs=[
            pl.BlockSpec((NSA_HEADS, tm, HEAD_DIM), lambda i: (0, i, 0)),
            pl.BlockSpec((8, tm, HEAD_DIM), lambda i: (0, i, 0)),
            pl.BlockSpec((4, tm, LANES), lambda i: (0, i, 0)),
            pl.BlockSpec((tm, POOL_WIDTH), lambda i: (i, 0)),
            pl.BlockSpec((tm, 2 * D_MODEL), lambda i: (i, 0)),
            pl.BlockSpec((KV_GROUPS, tm, LANES), lambda i: (0, i, 0)),
        ],
        out_shape=[
            jax.ShapeDtypeStruct((NSA_HEADS, T, HEAD_DIM), BF16),
            jax.ShapeDtypeStruct((8, T, HEAD_DIM), BF16),
            jax.ShapeDtypeStruct((4, T, LANES), BF16),
            jax.ShapeDtypeStruct((T, POOL_WIDTH), F32),
            jax.ShapeDtypeStruct((T, 2 * D_MODEL), BF16),
            jax.ShapeDtypeStruct((KV_GROUPS, T, LANES), F32),
        ],
        compiler_params=_params("parallel"),
        name="in_proj",
    )(h, g, w_main, w_gl)


def _cmp_kernel(x_ref, pos_ref, w1_ref, w2_ref, kc_ref, vc_ref):
    n_chunks = x_ref.shape[1]
    for j in range(2):
        out_ref = kc_ref if j == 0 else vc_ref
        for g in range(KV_GROUPS):
            x = x_ref[j * KV_GROUPS + g].astype(F32)
            lo = (x + pos_ref[j, 0:1, :]).astype(BF16)
            hi = (x + pos_ref[j, 1:2, :]).astype(BF16)
            hid = _dot(lo, w1_ref[j, 0]) + pltpu.roll(_dot(hi, w1_ref[j, 1]), n_chunks - 1, axis=0)
            act = jax.nn.gelu(hid).astype(BF16)
            out_ref[0, g] = _dot(act, w2_ref[j]).astype(BF16)


def _compress(kv8, pos, w1, w2, B, S):
    T = B * S
    nck = S // CMP_STRIDE
    half = CMP_STRIDE * HEAD_DIM
    x = kv8.reshape(kv8.shape[0], T // CMP_STRIDE, half)
    out = jax.ShapeDtypeStruct((B, KV_GROUPS, nck, HEAD_DIM), BF16)
    return pl.pallas_call(
        _cmp_kernel,
        grid=(B,),
        in_specs=[
            pl.BlockSpec((4, nck, half), lambda b: (0, b, 0)),
            _const_spec((2, 2, half)),
            _const_spec((2, 2, half, CMP_HIDDEN)),
            _const_spec((2, CMP_HIDDEN, HEAD_DIM)),
        ],
        out_specs=[pl.BlockSpec((1, KV_GROUPS, nck, HEAD_DIM), lambda b: (b, 0, 0, 0))] * 2,
        out_shape=[out, out],
        compiler_params=_params("parallel"),
        name="compress",
    )(x, pos, w1, w2)


def _sel_kernel(q_ref, kc_ref, vc_ref, map_ref, ocmp_ref, sel_ref, *, tq, n_blk):
    i = pl.program_id(2)
    kc = kc_ref[0, 0]
    vc = vc_ref[0, 0]
    nck = kc.shape[0]
    n_idx = lax.broadcasted_iota(jnp.int32, (nck, tq), 0)
    t_idx = i * tq + lax.broadcasted_iota(jnp.int32, (nck, tq), 1)
    valid = n_idx * CMP_STRIDE + (CMP_LEN - 1) <= t_idx
    psum = jnp.zeros((nck, tq), F32)
    for r in range(HEADS_PER_GROUP):
        s = jnp.where(valid, _dot_nt(kc, q_ref[r]), NEG_INF)
        e = jnp.exp2(s - jnp.max(s, axis=0, keepdims=True))
        p = jnp.where(valid, e / jnp.sum(e, axis=0, keepdims=True), 0.0)
        ocmp_ref[r] = _dot_tn(p.astype(BF16), vc)
        psum = psum + p
    imp = _dot(map_ref[...], psum, precision=lax.Precision.HIGHEST)
    blk = lax.broadcasted_iota(jnp.int32, (n_blk, tq), 0)
    cur = (i * tq + lax.broadcasted_iota(jnp.int32, (n_blk, tq), 1)) // SEL_BLOCK
    imp = jnp.where(blk > cur, -FORCE, imp)
    imp = jnp.where((blk == 0) | (blk == cur), FORCE, imp)
    rank = jnp.zeros((n_blk, tq), F32)
    for j in range(n_blk):
        row = imp[j:j + 1, :]
        ahead = (row > imp) | ((row == imp) & (blk > j))
        rank = rank + jnp.where(ahead, 1.0, 0.0)
    pad_row = lax.broadcasted_iota(jnp.int32, (SEL_ROWS - n_blk, tq), 0)
    sel = jnp.concatenate([jnp.where(rank < SEL_TOPN, 1.0, 0.0), jnp.where(pad_row == 0, 1.0, 0.0)], axis=0)
    sel_ref[0, 0] = sel.astype(BF16)


def _cmp_select(q8, kc, vc, cmp_map, B, S, tq=512):
    T = B * S
    n_blk = S // SEL_BLOCK
    nck = S // CMP_STRIDE
    nq = S // tq
    hp = HEADS_PER_GROUP
    return pl.pallas_call(
        functools.partial(_sel_kernel, tq=tq, n_blk=n_blk),
        grid=(B, KV_GROUPS, nq),
        in_specs=[
            pl.BlockSpec((hp, tq, HEAD_DIM), lambda b, g, i: (g, b * nq + i, 0)),
            pl.BlockSpec((1, 1, nck, HEAD_DIM), lambda b, g, i: (b, g, 0, 0)),
            pl.BlockSpec((1, 1, nck, HEAD_DIM), lambda b, g, i: (b, g, 0, 0)),
            _const_spec((n_blk, nck)),
        ],
        out_specs=[
            pl.BlockSpec((hp, tq, HEAD_DIM), lambda b, g, i: (g, b * nq + i, 0)),
            pl.BlockSpec((1, 1, SEL_ROWS, tq), lambda b, g, i: (b, g, 0, i)),
        ],
        out_shape=[
            jax.ShapeDtypeStruct((NSA_HEADS, T, HEAD_DIM), F32),
            jax.ShapeDtypeStruct((B, KV_GROUPS, SEL_ROWS, S), BF16),
        ],
        compiler_params=_params("parallel", "parallel", "parallel"),
        name="cmp_select",
    )(q8, kc, vc, cmp_map)


def _attn_kernel(q_ref, ks_ref, vs_ref, kw_ref, vw_ref, sel_ref, ocmp_ref, gl_ref, exp_ref, diag_ref, wint_ref,
                 o_ref, mask_sc, m_sc, acc_sc, *, tq, tk, S):
    i = pl.program_id(2)
    hp = HEADS_PER_GROUP
    rows = hp * tq
    q4 = q_ref[...].reshape(rows, HEAD_DIM)

    mask_sc[...] = _dot_tn(sel_ref[0, 0], exp_ref[...])
    m_sc[...] = jnp.full(m_sc.shape, NEG_INF, F32)
    acc_sc[...] = jnp.zeros(acc_sc.shape, F32)

    def sel_tile(k0, mask):
        k = ks_ref[0, pl.ds(k0, tk), :]
        v = vs_ref[0, pl.ds(k0, tk), :]
        s = (_dot_nt(q4, k).reshape(hp, tq, tk) + mask[None]).reshape(rows, tk)
        m_old = m_sc[...]
        m_new = jnp.maximum(m_old, jnp.max(s, axis=-1, keepdims=True))
        p = jnp.exp2(s - jnp.concatenate([m_new] * (tk // LANES), axis=1))
        acc_sc[...] = jnp.exp2(m_old - m_new) * acc_sc[...] + _dot(p.astype(BF16), v)
        m_sc[...] = m_new

    n_full = (i * tq) // tk

    def full_step(j, carry):
        k0 = pl.multiple_of(j * tk, tk)
        sel_tile(k0, mask_sc[:, pl.ds(k0, tk)])
        return carry

    lax.fori_loop(0, n_full, full_step, 0)
    kd = pl.multiple_of(n_full * tk, tk)
    sel_tile(kd, mask_sc[:, pl.ds(kd, tk)] + diag_ref[0])
    acc = acc_sc[...]
    o_sel = acc[:, :HEAD_DIM] / acc[:, HEAD_DIM:HEAD_DIM + 1]

    span = WINDOW + tq
    w0 = pl.multiple_of(jnp.clip(i * tq - WINDOW, 0, S - span), LANES)
    kw = kw_ref[0, pl.ds(w0, span), :]
    vw = vw_ref[0, pl.ds(w0, span), :]
    s = (_dot_nt(q4, kw).reshape(hp, tq, span) + wint_ref[0][None]).reshape(rows, span)
    p = jnp.exp2(s - jnp.max(s, axis=-1, keepdims=True))
    accw = _dot(p.astype(BF16), vw)
    o_win = accw[:, :HEAD_DIM] / accw[:, HEAD_DIM:HEAD_DIM + 1]

    gl = gl_ref[0]
    outs = []
    for r in range(hp):
        rs = slice(r * tq, (r + 1) * tq)
        outs.append(gl[:, 3 * r:3 * r + 1] * ocmp_ref[r]
                    + gl[:, 3 * r + 1:3 * r + 2] * o_sel[rs]
                    + gl[:, 3 * r + 2:3 * r + 3] * o_win[rs])
    o_ref[...] = jnp.concatenate(outs, axis=-1).astype(BF16)


def _attention_tables(S, tq, tk):
    n_blk = S // SEL_BLOCK
    expand = np.zeros((SEL_ROWS, S), np.float32)
    expand[:n_blk] = ((np.arange(S)[None, :] // SEL_BLOCK) == np.arange(n_blk)[:, None]) * MASK_BIG
    expand[n_blk] = -MASK_BIG
    a = np.arange(tq)[:, None]
    c = np.arange(tk)[None, :]
    diag = np.stack([np.where(c <= p * tq + a, 0.0, -MASK_BIG) for p in range(tk // tq)])
    span = WINDOW + tq
    win = []
    for i in range(WINDOW // tq + 1):
        t = i * tq + a
        kp = min(max(i * tq - WINDOW, 0), S - span) + np.arange(span)[None, :]
        win.append(np.where((kp <= t) & (kp > t - WINDOW), 0.0, -MASK_BIG))
    return (jnp.asarray(expand, dtype=BF16), jnp.asarray(diag.astype(np.float32)),
            jnp.asarray(np.stack(win).astype(np.float32)))


def _sparse_attention(q8, kv8, vx, sel, ocmp, gls, tables, B, S, tq, tk):
    expand, diag, win = tables
    T = B * S
    nq = S // tq
    hp = HEADS_PER_GROUP
    span = WINDOW + tq
    n_win = WINDOW // tq

    def k_spec(first):
        return pl.BlockSpec((1, S, HEAD_DIM), lambda b, g, i: (first + g, b, 0))

    def v_spec(first):
        return pl.BlockSpec((1, S, LANES), lambda b, g, i: (first + g, b, 0))

    return pl.pallas_call(
        functools.partial(_attn_kernel, tq=tq, tk=tk, S=S),
        grid=(B, KV_GROUPS, nq),
        in_specs=[
            pl.BlockSpec((hp, tq, HEAD_DIM), lambda b, g, i: (g, b * nq + i, 0)),
            k_spec(4), v_spec(0), k_spec(6), v_spec(2),
            pl.BlockSpec((1, 1, SEL_ROWS, tq), lambda b, g, i: (b, g, 0, i)),
            pl.BlockSpec((hp, tq, HEAD_DIM), lambda b, g, i: (g, b * nq + i, 0)),
            pl.BlockSpec((1, tq, LANES), lambda b, g, i: (g, b * nq + i, 0)),
            _const_spec((SEL_ROWS, S)),
            pl.BlockSpec((1, tq, tk), lambda b, g, i: (i % (tk // tq), 0, 0)),
            pl.BlockSpec((1, tq, span), lambda b, g, i: (jnp.minimum(i, n_win), 0, 0)),
        ],
        out_specs=pl.BlockSpec((tq, hp * HEAD_DIM), lambda b, g, i: (b * nq + i, g)),
        out_shape=jax.ShapeDtypeStruct((T, NSA_WIDTH), BF16),
        scratch_shapes=[
            pltpu.VMEM((tq, S), F32),
            pltpu.VMEM((hp * tq, LANES), F32),
            pltpu.VMEM((hp * tq, LANES), F32),
        ],
        compiler_params=_params("parallel", "parallel", "parallel"),
        name="sparse_attention",
    )(q8, kv8, vx, kv8, vx, sel, ocmp, gls, expand, diag, win)


_POOL_HALO = 16


def _pool_kernel(u_ref, halo_ref, w_ref, scale_ref, y_ref, *, tp, S):
    i = pl.program_id(0)
    pos0 = (i * tp) % S
    halo = jnp.where(pos0 == 0, 0.0, halo_ref[...])
    x = jnp.concatenate([halo, u_ref[...]], axis=0)
    pos = pos0 + lax.broadcasted_iota(jnp.int32, (tp, 1), 0)
    for gi, w in enumerate(POOL_WINDOWS):
        c0 = gi * POOL_GROUP_WIDTH
        xg = x[:, c0:c0 + POOL_GROUP_WIDTH]
        tot = xg[_POOL_HALO:_POOL_HALO + tp]
        for d in range(1, w):
            tot = tot + xg[_POOL_HALO - d:_POOL_HALO - d + tp]
        cnt = jnp.minimum(pos + 1, w).astype(F32)
        pooled = tot / cnt - xg[_POOL_HALO:_POOL_HALO + tp]
        mixed = _dot(pooled.astype(BF16), w_ref[gi])
        y_ref[:, c0:c0 + POOL_GROUP_WIDTH] = (mixed * scale_ref[:, c0:c0 + POOL_GROUP_WIDTH]).astype(BF16)


def _pool_mixer(u, pool_w, pool_scale, S, tp=512):
    T = u.shape[0]
    hb = tp // _POOL_HALO
    return pl.pallas_call(
        functools.partial(_pool_kernel, tp=tp, S=S),
        grid=(T // tp,),
        in_specs=[
            pl.BlockSpec((tp, POOL_WIDTH), lambda i: (i, 0)),
            pl.BlockSpec((_POOL_HALO, POOL_WIDTH), lambda i: (jnp.maximum(i * hb - 1, 0), 0)),
            _const_spec((4, POOL_GROUP_WIDTH, POOL_GROUP_WIDTH)),
            _const_spec((1, POOL_WIDTH)),
        ],
        out_specs=pl.BlockSpec((tp, POOL_WIDTH), lambda i: (i, 0)),
        out_shape=jax.ShapeDtypeStruct((T, POOL_WIDTH), BF16),
        compiler_params=_params("parallel"),
        name="pool_mixer",
    )(u, u, pool_w, pool_scale)


def _merge_residual(onsa_ref, ypool_ref, mg_ref, h_ref, wn_ref, wp_ref, wo_ref):
    a = _dot(onsa_ref[...], wn_ref[...])
    b = _dot(ypool_ref[...], wp_ref[...])
    mg = mg_ref[...].astype(F32)
    merged = mg[:, :D_MODEL] * a + mg[:, D_MODEL:] * b
    return h_ref[...] + _dot(merged.astype(BF16), wo_ref[...])


def _post_dense_kernel(onsa_ref, ypool_ref, mg_ref, h_ref, wn_ref, wp_ref, wo_ref, g_ref, h1_ref, f_ref):
    h1 = _merge_residual(onsa_ref, ypool_ref, mg_ref, h_ref, wn_ref, wp_ref, wo_ref)
    h1_ref[...] = h1
    f_ref[...] = _rms(h1, g_ref[...]).astype(BF16)


def _post_moe_kernel(onsa_ref, ypool_ref, mg_ref, h_ref, wn_ref, wp_ref, wo_ref, g_ref, wr_ref, br_ref,
                     h1_ref, f_ref, route_ref, cnt_ref, base_sc, *, tm):
    @pl.when(pl.program_id(0) == 0)
    def _():
        base_sc[...] = jnp.zeros(base_sc.shape, F32)

    h1 = _merge_residual(onsa_ref, ypool_ref, mg_ref, h_ref, wn_ref, wp_ref, wo_ref)
    h1_ref[...] = h1
    f = _rms(h1, g_ref[...])
    for p in range(f_ref.shape[0]):
        f_ref[p] = f[:, p * f_ref.shape[2]:(p + 1) * f_ref.shape[2]]

    lane = lax.broadcasted_iota(jnp.int32, (tm, LANES), 1)
    logits = _dot(f, wr_ref[...], precision=lax.Precision.HIGHEST) + br_ref[...]
    lowest = jnp.finfo(F32).min
    lane_f = lane.astype(F32)
    lg = jnp.where(lane < N_EXPERTS, logits, lowest)
    m1 = jnp.max(lg, axis=-1, keepdims=True)
    i1 = jnp.min(jnp.where(lg == m1, lane_f, float(LANES)), axis=-1, keepdims=True)
    lg2 = jnp.where(lane_f == i1, lowest, lg)
    m2 = jnp.max(lg2, axis=-1, keepdims=True)
    i2 = jnp.min(jnp.where(lg2 == m2, lane_f, float(LANES)), axis=-1, keepdims=True)
    e2 = jnp.exp(m2 - m1)
    gate1 = 1.0 / (1.0 + e2)
    gate2 = e2 / (1.0 + e2)

    oh1 = lane_f == i1
    oh2 = lane_f == i2
    both = jnp.where(oh1 | oh2, 1.0, 0.0)
    r_i = lax.broadcasted_iota(jnp.int32, (tm, tm), 0)
    c_i = lax.broadcasted_iota(jnp.int32, (tm, tm), 1)
    before = jnp.where(r_i > c_i, 1.0, 0.0).astype(BF16)
    seen = _dot(before, both.astype(BF16)) + base_sc[...]
    rank1 = jnp.sum(jnp.where(oh1, seen, 0.0), axis=-1, keepdims=True)
    rank2 = jnp.sum(jnp.where(oh2, seen, 0.0), axis=-1, keepdims=True)
    base_sc[...] = base_sc[...] + jnp.sum(both, axis=0, keepdims=True)
    cnt_ref[...] = base_sc[...]

    route = jnp.zeros((tm, LANES), F32)
    for col, val in enumerate((i1, i2, gate1, gate2, rank1, rank2)):
        route = jnp.where(lane == col, val, route)
    route_ref[...] = route


def _post_mixer(onsa, ypool, mgs, h, wn, wp, wo, g_ffn, router=None, tm=512):
    T = h.shape[0]
    tile = lambda w: pl.BlockSpec((tm, w), lambda i: (i, 0))
    in_specs = [
        tile(NSA_WIDTH), tile(POOL_WIDTH), tile(2 * D_MODEL), tile(D_MODEL),
        _const_spec((NSA_WIDTH, D_MODEL)), _const_spec((POOL_WIDTH, D_MODEL)),
        _const_spec((D_MODEL, D_MODEL)), _const_spec((1, D_MODEL)),
    ]
    args = [onsa, ypool, mgs, h, wn, wp, wo, g_ffn]
    if router is None:
        return pl.pallas_call(
            _post_dense_kernel,
            grid=(T // tm,),
            in_specs=in_specs,
            out_specs=[tile(D_MODEL), tile(D_MODEL)],
            out_shape=[jax.ShapeDtypeStruct((T, D_MODEL), F32), jax.ShapeDtypeStruct((T, D_MODEL), BF16)],
            compiler_params=_params("parallel"),
            name="post_mixer_dense",
        )(*args)
    w_router, b_router = router
    return pl.pallas_call(
        functools.partial(_post_moe_kernel, tm=tm),
        grid=(T // tm,),
        in_specs=in_specs + [_const_spec((D_MODEL, LANES)), _const_spec((1, LANES))],
        out_specs=[tile(D_MODEL), pl.BlockSpec((N_PLANES, tm, PLANE_WIDTH), lambda i: (0, i, 0)), tile(LANES),
                   pl.BlockSpec((1, LANES), lambda i: (0, 0))],
        out_shape=[
            jax.ShapeDtypeStruct((T, D_MODEL), F32),
            jax.ShapeDtypeStruct((N_PLANES, T, PLANE_WIDTH), F32),
            jax.ShapeDtypeStruct((T, LANES), F32),
            jax.ShapeDtypeStruct((1, LANES), F32),
        ],
        scratch_shapes=[pltpu.VMEM((1, LANES), F32)],
        compiler_params=_params("arbitrary"),
        name="post_mixer_moe",
    )(*args, w_router, b_router)


def _ffn_kernel(f_ref, h_ref, wg_ref, wu_ref, wd_ref, o_ref, *, fc):
    f = f_ref[...]
    o_ref[...] = h_ref[...]
    for c in range(wg_ref.shape[1] // fc):
        g = _dot(f, wg_ref[:, c * fc:(c + 1) * fc])
        u = _dot(f, wu_ref[:, c * fc:(c + 1) * fc])
        o_ref[...] += _dot((jax.nn.silu(g) * u).astype(BF16), wd_ref[c * fc:(c + 1) * fc, :])


def _dense_ffn(f, h1, wg, wu, wd, tm=512, fc=256):
    T = f.shape[0]
    d_ff = wg.shape[1]
    tile = pl.BlockSpec((tm, D_MODEL), lambda i: (i, 0))
    return pl.pallas_call(
        functools.partial(_ffn_kernel, fc=fc),
        grid=(T // tm,),
        in_specs=[tile, tile, _const_spec((D_MODEL, d_ff)), _const_spec((D_MODEL, d_ff)),
                  _const_spec((d_ff, D_MODEL))],
        out_specs=tile,
        out_shape=jax.ShapeDtypeStruct((T, D_MODEL), F32),
        compiler_params=_params("parallel"),
        name="dense_ffn",
    )(f, h1, wg, wu, wd)


PLANE_WIDTH = 256
N_PLANES = D_MODEL // PLANE_WIDTH
SC_WINDOW = 128


def _sc_mesh():
    return plsc.VectorSubcoreMesh(core_axis_name="core", subcore_axis_name="subcore")


def _sc_scatter_pair(rows, idx_a, idx_b, n_out):
    n_rows, width = rows.shape

    @pl.kernel(out_type=jax.ShapeDtypeStruct((n_out, width), rows.dtype), mesh=_sc_mesh(), scratch_types=[])
    def scatter(rows_hbm, a_hbm, b_hbm, out_hbm):
        def body(rows_vmem, a_vmem, b_vmem):
            pltpu.sync_copy(rows_vmem, out_hbm.at[a_vmem.at[0]])
            pltpu.sync_copy(rows_vmem, out_hbm.at[b_vmem.at[0]])

        idx_spec = pl.BlockSpec((1, SC_WINDOW), lambda i: (0, i))
        pltpu.emit_pipeline(
            body,
            grid=(n_rows // SC_WINDOW,),
            in_specs=[pl.BlockSpec((SC_WINDOW, width), lambda i: (i, 0)), idx_spec, idx_spec],
            out_specs=[],
            core_axis_name=("core", "subcore"),
            dimension_semantics=(pltpu.PARALLEL,),
        )(rows_hbm, a_hbm, b_hbm)

    return scatter(rows, idx_a, idx_b)


def _sc_gather(rows, idx):
    n_out = idx.shape[1]
    width = rows.shape[1]

    @pl.kernel(out_type=jax.ShapeDtypeStruct((n_out, width), rows.dtype), mesh=_sc_mesh(), scratch_types=[])
    def gather(rows_hbm, idx_hbm, out_hbm):
        def body(idx_vmem, out_vmem):
            pltpu.sync_copy(rows_hbm.at[idx_vmem.at[0]], out_vmem)

        pltpu.emit_pipeline(
            body,
            grid=(n_out // SC_WINDOW,),
            in_specs=[pl.BlockSpec((1, SC_WINDOW), lambda i: (0, i))],
            out_specs=[pl.BlockSpec((SC_WINDOW, width), lambda i: (i, 0))],
            core_axis_name=("core", "subcore"),
            dimension_semantics=(pltpu.PARALLEL,),
        )(idx_hbm, out_hbm)

    return gather(rows, idx)


def _dispatch(f_planes, dest):
    n_planes, T, width = f_planes.shape
    n_rows = 2 * T
    plane_base = (jnp.arange(n_planes, dtype=jnp.int32) * n_rows)[:, None]
    idx = [(plane_base + dest[:, k][None, :]).reshape(1, n_planes * T) for k in range(2)]
    out = _sc_scatter_pair(f_planes.reshape(n_planes * T, width), idx[0], idx[1], n_planes * n_rows)
    return out.reshape(n_planes, n_rows, width)


def _undispatch(y_planes, dest):
    n_planes, n_rows, width = y_planes.shape
    T = dest.shape[0]
    plane_base = (jnp.arange(n_planes, dtype=jnp.int32) * n_rows)[None, :, None]
    idx = (plane_base + dest.T[:, None, :]).reshape(1, 2 * n_planes * T)
    out = _sc_gather(y_planes.reshape(n_planes * n_rows, width), idx)
    return out.reshape(2, n_planes, T, width)


def _combine_kernel(h_ref, route_ref, z_ref, g_ref, o_ref, *, final_norm):
    route = route_ref[...]
    z = [jnp.concatenate([z_ref[k, p] for p in range(N_PLANES)], axis=-1) for k in range(2)]
    out = h_ref[...] + (route[:, 2:3] * z[0] + route[:, 3:4] * z[1])
    if final_norm:
        out = _rms(out, g_ref[...])
    o_ref[...] = out


def _combine(h1, route, z, g_final, final_norm, tg=512):
    T = h1.shape[0]
    return pl.pallas_call(
        functools.partial(_combine_kernel, final_norm=final_norm),
        grid=(T // tg,),
        in_specs=[
            pl.BlockSpec((tg, D_MODEL), lambda i: (i, 0)),
            pl.BlockSpec((tg, LANES), lambda i: (i, 0)),
            pl.BlockSpec((2, N_PLANES, tg, PLANE_WIDTH), lambda i: (0, 0, i, 0)),
            _const_spec((1, D_MODEL)),
        ],
        out_specs=pl.BlockSpec((tg, D_MODEL), lambda i: (i, 0)),
        out_shape=jax.ShapeDtypeStruct((T, D_MODEL), F32),
        compiler_params=_params("parallel"),
        name="moe_combine",
    )(h1, route, z, g_final)


def _moe_kernel(blk_ref, exp_ref, lo_ref, hi_ref, first_ref, x_ref, wg_ref, wu_ref, wd_ref, y_ref, acc_sc, *, bm):
    v = pl.program_id(0)
    c = pl.program_id(1)

    @pl.when((first_ref[v] == 1) & (c == 0))
    def _():
        acc_sc[...] = jnp.zeros(acc_sc.shape, F32)

    lo = lo_ref[v]
    hi = hi_ref[v]

    @pl.when(hi > lo)
    def _():
        row = blk_ref[v] * bm + lax.broadcasted_iota(jnp.int32, (bm, 1), 0)
        x = jnp.concatenate([x_ref[p] for p in range(N_PLANES)], axis=-1)
        x = jnp.where((row >= lo) & (row < hi), x, 0.0).astype(BF16)
        g = _dot(x, wg_ref[0])
        u = _dot(x, wu_ref[0])
        acc_sc[...] += _dot((jax.nn.silu(g) * u).astype(BF16), wd_ref[0])

    @pl.when(c == pl.num_programs(1) - 1)
    def _():
        for p in range(N_PLANES):
            y_ref[p] = acc_sc[:, p * PLANE_WIDTH:(p + 1) * PLANE_WIDTH]


def _grouped_ffn(x_sorted, visits, wg, wu, wd, bm=1024, fc=512):
    n_rows = x_sorted.shape[1]
    n_visits = visits[0].shape[0]
    d_ff = wg.shape[2]
    rows_spec = pl.BlockSpec((N_PLANES, bm, PLANE_WIDTH), lambda v, c, blk, ex, lo, hi, fi: (0, blk[v], 0))
    grid_spec = pltpu.PrefetchScalarGridSpec(
        num_scalar_prefetch=5,
        grid=(n_visits, d_ff // fc),
        in_specs=[
            rows_spec,
            pl.BlockSpec((1, D_MODEL, fc), lambda v, c, blk, ex, lo, hi, fi: (ex[v], 0, c)),
            pl.BlockSpec((1, D_MODEL, fc), lambda v, c, blk, ex, lo, hi, fi: (ex[v], 0, c)),
            pl.BlockSpec((1, fc, D_MODEL), lambda v, c, blk, ex, lo, hi, fi: (ex[v], c, 0)),
        ],
        out_specs=rows_spec,
        scratch_shapes=[pltpu.VMEM((bm, D_MODEL), F32)],
    )
    return pl.pallas_call(
        functools.partial(_moe_kernel, bm=bm),
        grid_spec=grid_spec,
        out_shape=jax.ShapeDtypeStruct((N_PLANES, n_rows, PLANE_WIDTH), F32),
        compiler_params=_params("arbitrary", "arbitrary"),
        name="grouped_ffn",
    )(*visits, x_sorted, wg, wu, wd)


def _routing_tables(route, counts, n_rows, bm):
    counts = counts[0, :N_EXPERTS].astype(jnp.int32)
    ends = jnp.cumsum(counts)
    starts = ends - counts
    e = route[:, 0:2].astype(jnp.int32)
    seg_start = jnp.sum(jnp.where(e[..., None] == jnp.arange(N_EXPERTS), starts, 0), axis=-1)
    dest = seg_start + route[:, 4:6].astype(jnp.int32)
    n_blocks = n_rows // bm
    cuts = jnp.sort(jnp.concatenate([jnp.arange(n_blocks, dtype=jnp.int32) * bm, starts[1:]]))
    lo = cuts
    hi = jnp.concatenate([cuts[1:], jnp.array([n_rows], jnp.int32)])
    blk = jnp.minimum(lo // bm, n_blocks - 1)
    ex = jnp.minimum(jnp.sum(ends[None, :] <= lo[:, None], axis=1), N_EXPERTS - 1).astype(jnp.int32)
    first = jnp.concatenate([jnp.ones((1,), jnp.int32), (blk[1:] != blk[:-1]).astype(jnp.int32)])
    return dest, (blk, ex, lo, hi, first)


def _layer_weights(w_in, cmp_pos, cmp_w1, cmp_w2):
    w_main = jnp.concatenate([w_in[:, _Q0:_GL0], w_in[:, _U0:]], axis=1).astype(BF16)
    gl = w_in[:, _GL0:_U0].reshape(D_MODEL, KV_GROUPS, _GATES_PER_GROUP)
    w_gl = jnp.pad(gl, ((0, 0), (0, 0), (0, LANES - _GATES_PER_GROUP))).reshape(D_MODEL, KV_GROUPS * LANES)
    half = CMP_STRIDE * HEAD_DIM
    pos = cmp_pos.reshape(2, 2, half)
    w1 = cmp_w1.reshape(2, 2, half, CMP_HIDDEN).astype(BF16)
    return w_main, w_gl.astype(BF16), pos, w1, cmp_w2.astype(BF16)


def _cmp_to_sel_map(S):
    n_blk = S // SEL_BLOCK
    n_cmp = (S - CMP_LEN) // CMP_STRIDE + 1
    nck = S // CMP_STRIDE
    cs = np.arange(nck) * CMP_STRIDE
    bs = np.arange(n_blk) * SEL_BLOCK
    overlap = (cs[None, :] <= bs[:, None] + SEL_BLOCK - 1) & (cs[None, :] + CMP_LEN - 1 >= bs[:, None])
    overlap &= (np.arange(nck) < n_cmp)[None, :]
    return jnp.asarray(overlap.astype(np.float32))


def kernel(x, norm_mix_g, w_in, cmp_pos, cmp_w1, cmp_w2, w_nsa_up, pool_w, pool_scale, w_pool_up, w_out,
           norm_ffn_g, ffn_w_gate, ffn_w_up, ffn_w_down, moe_w_router, moe_b_router, moe_w_gate, moe_w_up,
           moe_w_down, final_norm_g):
    B, S, D = x.shape
    T = B * S
    depth = w_in.shape[0]
    cmp_map = _cmp_to_sel_map(S)
    attn_tq, attn_tk = 256, 512
    attn_tables = _attention_tables(S, attn_tq, attn_tk)
    h = x.reshape(T, D)
    bm = 1024
    for i in range(depth):
        w_main, w_gl, pos, w1, w2 = _layer_weights(w_in[i], cmp_pos[i], cmp_w1[i], cmp_w2[i])
        q8, kv8, vx, u, mgs, gls = _in_proj(h, norm_mix_g[i][None], w_main, w_gl)
        kc, vc = _compress(kv8, pos, w1, w2, B, S)
        ocmp, sel = _cmp_select(q8, kc, vc, cmp_map, B, S)
        onsa = _sparse_attention(q8, kv8, vx, sel, ocmp, gls, attn_tables, B, S, attn_tq, attn_tk)
        ypool = _pool_mixer(u, pool_w[i].astype(BF16), pool_scale[i][None], S)
        mix_w = (w_nsa_up[i].astype(BF16), w_pool_up[i].astype(BF16), w_out[i].astype(BF16), norm_ffn_g[i][None])
        j = i // 2
        if i % 2 == 0:
            h1, f = _post_mixer(onsa, ypool, mgs, h, *mix_w)
            h = _dense_ffn(f, h1, ffn_w_gate[j].astype(BF16), ffn_w_up[j].astype(BF16),
                           ffn_w_down[j].astype(BF16))
        else:
            w_router = jnp.pad(moe_w_router[j], ((0, 0), (0, LANES - N_EXPERTS)))
            b_router = jnp.pad(moe_b_router[j], (0, LANES - N_EXPERTS))[None]
            h1, f, route, counts = _post_mixer(onsa, ypool, mgs, h, *mix_w, router=(w_router, b_router))
            dest, visits = _routing_tables(route, counts, 2 * T, bm)
            x_sorted = _dispatch(f, dest)
            y_sorted = _grouped_ffn(x_sorted, visits, moe_w_gate[j].astype(BF16), moe_w_up[j].astype(BF16),
                                    moe_w_down[j].astype(BF16), bm=bm)
            z = _undispatch(y_sorted, dest)
            h = _combine(h1, route, z, final_norm_g[None], final_norm=(i == depth - 1))
    return h.reshape(B, S, D)
```

```python
import functools

import numpy as np
import jax
import jax.numpy as jnp
from jax import lax
from jax.experimental import pallas as pl
from jax.experimental.pallas import tpu as pltpu
from jax.experimental.pallas import tpu_sc as plsc

D_MODEL = 1024
NSA_HEADS = 8
HEAD_DIM = 64
KV_GROUPS = 2
HEADS_PER_GROUP = NSA_HEADS // KV_GROUPS
NSA_WIDTH = NSA_HEADS * HEAD_DIM
KV_WIDTH = KV_GROUPS * HEAD_DIM
CMP_LEN = 32
CMP_STRIDE = 16
CMP_HIDDEN = 256
SEL_BLOCK = 64
SEL_TOPN = 8
WINDOW = 512
ATTN_SCALE = HEAD_DIM ** -0.5
POOL_WINDOWS = (2, 4, 8, 16)
POOL_GROUP_WIDTH = 128
POOL_WIDTH = 512
N_EXPERTS = 8
D_FF_EXPERT = 3584
RMS_EPS = 1e-6
NEG_INF = -1e30
FORCE = 1e9

LOG2E = 1.4426950408889634
MASK_BIG = 2.0 ** 100

LANES = 128
VMEM_LIMIT = 56 * 1024 * 1024

BF16 = jnp.bfloat16
F32 = jnp.float32

_Q0 = 0
_KV0 = NSA_WIDTH
_GL0 = _KV0 + 6 * KV_WIDTH
_U0 = _GL0 + 3 * NSA_HEADS
_MG0 = _U0 + POOL_WIDTH
_MAIN_COLS = NSA_WIDTH + 6 * KV_WIDTH + POOL_WIDTH + 2 * D_MODEL
_GATES_PER_GROUP = 3 * HEADS_PER_GROUP


def _dot(a, b, **kw):
    return jnp.dot(a, b, preferred_element_type=F32, **kw)


def _dot_tn(a, b):
    return lax.dot_general(a, b, (((0,), (0,)), ((), ())), preferred_element_type=F32)


def _dot_nt(a, b):
    return lax.dot_general(a, b, (((1,), (1,)), ((), ())), preferred_element_type=F32)


def _rms(x, g):
    return x * lax.rsqrt(jnp.mean(x * x, axis=-1, keepdims=True) + RMS_EPS) * g


def _params(*sem):
    return pltpu.CompilerParams(dimension_semantics=sem, vmem_limit_bytes=VMEM_LIMIT)


def _const_spec(shape):
    nd = len(shape)
    return pl.BlockSpec(shape, lambda *_: (0,) * nd, pipeline_mode=pl.Buffered(1))


def _in_kernel(h_ref, g_ref, w_ref, wgl_ref, chunk_ref, q_ref, xc_ref, kw_ref, ksx_ref, vx_ref, u_ref, mg_ref,
               gl_ref, *, S):
    a = _rms(h_ref[...], g_ref[...]).astype(BF16)
    tm = a.shape[0]
    zeros = jnp.zeros((tm, HEAD_DIM), F32)
    ones = jnp.ones((tm, HEAD_DIM), F32)
    pos = (pl.program_id(0) * tm + lax.broadcasted_iota(jnp.int32, (tm, HEAD_DIM), 0)) % S
    lane = lax.broadcasted_iota(jnp.int32, (tm, HEAD_DIM), 1)
    block_code = jnp.where(pos // SEL_BLOCK == lane, -MASK_BIG, 0.0)

    def wide(lo, hi):
        return jnp.concatenate([lo, hi], axis=-1).astype(BF16)

    q = _dot(a, w_ref[:, 0:NSA_WIDTH]) * (ATTN_SCALE * LOG2E)
    for h in range(NSA_HEADS):
        q_ref[h] = wide(q[:, h * HEAD_DIM:(h + 1) * HEAD_DIM], zeros)
    for c in range(3):
        c0 = NSA_WIDTH + c * 256
        kv = _dot(a, w_ref[:, c0:c0 + 256])
        pieces = [kv[:, i * HEAD_DIM:(i + 1) * HEAD_DIM] for i in range(4)]
        if c == 0:
            n_chunks = tm // CMP_STRIDE
            for i in range(4):
                by_offset = _dot(chunk_ref[...], pieces[i].astype(BF16))
                for l in range(CMP_STRIDE):
                    xc_ref[i, :, l * HEAD_DIM:(l + 1) * HEAD_DIM] = (
                        by_offset[l * n_chunks:(l + 1) * n_chunks].astype(BF16))
            continue
        for g in range(KV_GROUPS):
            if c == 1:
                ksx_ref[g] = wide(pieces[g], block_code)
            else:
                kw_ref[g] = pieces[g].astype(BF16)
            vx_ref[2 * (c - 1) + g] = wide(pieces[2 + g], ones)
    u0 = NSA_WIDTH + 6 * KV_WIDTH
    u_ref[...] = _dot(a, w_ref[:, u0:u0 + POOL_WIDTH])
    m0 = u0 + POOL_WIDTH
    for c in range(4):
        mg = _dot(a, w_ref[:, m0 + c * 512:m0 + (c + 1) * 512])
        mg_ref[:, c * 512:(c + 1) * 512] = jax.nn.sigmoid(mg).astype(BF16)
    gl = jax.nn.sigmoid(_dot(a, wgl_ref[...]))
    for g in range(KV_GROUPS):
        gl_ref[g] = gl[:, g * LANES:(g + 1) * LANES]


def _in_proj(h, g, w_main, w_gl, S, tm=512):
    T = h.shape[0]
    return pl.pallas_call(
        functools.partial(_in_kernel, S=S),
        grid=(T // tm,),
        in_specs=[
            pl.BlockSpec((tm, D_MODEL), lambda i: (i, 0)),
            _const_spec((1, D_MODEL)),
            _const_spec((D_MODEL, _MAIN_COLS)),
            _const_spec((D_MODEL, KV_GROUPS * LANES)),
            _const_spec((tm, tm)),
        ],
        out_specs=[
            pl.BlockSpec((NSA_HEADS, tm, LANES), lambda i: (0, i, 0)),
            pl.BlockSpec((4, tm // CMP_STRIDE, CMP_STRIDE * HEAD_DIM), lambda i: (0, i, 0)),
            pl.BlockSpec((KV_GROUPS, tm, HEAD_DIM), lambda i: (0, i, 0)),
            pl.BlockSpec((KV_GROUPS, tm, LANES), lambda i: (0, i, 0)),
            pl.BlockSpec((4, tm, LANES), lambda i: (0, i, 0)),
            pl.BlockSpec((tm, POOL_WIDTH), lambda i: (i, 0)),
            pl.BlockSpec((tm, 2 * D_MODEL), lambda i: (i, 0)),
            pl.BlockSpec((KV_GROUPS, tm, LANES), lambda i: (0, i, 0)),
        ],
        out_shape=[
            jax.ShapeDtypeStruct((NSA_HEADS, T, LANES), BF16),
            jax.ShapeDtypeStruct((4, T // CMP_STRIDE, CMP_STRIDE * HEAD_DIM), BF16),
            jax.ShapeDtypeStruct((KV_GROUPS, T, HEAD_DIM), BF16),
            jax.ShapeDtypeStruct((KV_GROUPS, T, LANES), BF16),
            jax.ShapeDtypeStruct((4, T, LANES), BF16),
            jax.ShapeDtypeStruct((T, POOL_WIDTH), F32),
            jax.ShapeDtypeStruct((T, 2 * D_MODEL), BF16),
            jax.ShapeDtypeStruct((KV_GROUPS, T, LANES), F32),
        ],
        compiler_params=_params("parallel"),
        name="in_proj",
    )(h, g, w_main, w_gl, _chunk_permutation(tm))


def _chunk_permutation(tm):
    n_chunks = tm // CMP_STRIDE
    perm = np.zeros((tm, tm), np.float32)
    l, n = np.meshgrid(np.arange(CMP_STRIDE), np.arange(n_chunks), indexing="ij")
    perm[(l * n_chunks + n).ravel(), (CMP_STRIDE * n + l).ravel()] = 1.0
    return jnp.asarray(perm, dtype=BF16)


def _cmp_kernel(x_ref, pos_ref, w1_ref, w2_ref, kc_ref, vc_ref):
    n_chunks = x_ref.shape[1]
    for j in range(2):
        out_ref = kc_ref if j == 0 else vc_ref
        for g in range(KV_GROUPS):
            x = x_ref[j * KV_GROUPS + g].astype(F32)
            lo = (x + pos_ref[j, 0:1, :]).astype(BF16)
            hi = (x + pos_ref[j, 1:2, :]).astype(BF16)
            hid = _dot(lo, w1_ref[j, 0]) + pltpu.roll(_dot(hi, w1_ref[j, 1]), n_chunks - 1, axis=0)
            act = jax.nn.gelu(hid).astype(BF16)
            out_ref[0, g] = _dot(act, w2_ref[j]).astype(BF16)


def _compress(x, pos, w1, w2, B, S):
    nck = S // CMP_STRIDE
    half = CMP_STRIDE * HEAD_DIM
    out = jax.ShapeDtypeStruct((B, KV_GROUPS, nck, HEAD_DIM), BF16)
    return pl.pallas_call(
        _cmp_kernel,
        grid=(B,),
        in_specs=[
            pl.BlockSpec((4, nck, half), lambda b: (0, b, 0)),
            _const_spec((2, 2, half)),
            _const_spec((2, 2, half, CMP_HIDDEN)),
            _const_spec((2, CMP_HIDDEN, HEAD_DIM)),
        ],
        out_specs=[pl.BlockSpec((1, KV_GROUPS, nck, HEAD_DIM), lambda b: (b, 0, 0, 0))] * 2,
        out_shape=[out, out],
        compiler_params=_params("parallel"),
        name="compress",
    )(x, pos, w1, w2)


def _sel_kernel(q_ref, kc_ref, vc_ref, map_ref, ocmp_ref, unsel_ref, *, tq, n_blk):
    i = pl.program_id(2)
    kc = kc_ref[0, 0]
    vc = vc_ref[0, 0]
    nck = kc.shape[0]
    n_idx = lax.broadcasted_iota(jnp.int32, (nck, tq), 0)
    t_idx = i * tq + lax.broadcasted_iota(jnp.int32, (nck, tq), 1)
    valid = n_idx * CMP_STRIDE + (CMP_LEN - 1) <= t_idx
    psum = jnp.zeros((nck, tq), F32)
    for r in range(HEADS_PER_GROUP):
        s = jnp.where(valid, _dot_nt(kc, q_ref[r][:, :HEAD_DIM]), NEG_INF)
        e = jnp.exp2(s - jnp.max(s, axis=0, keepdims=True))
        p = jnp.where(valid, e / jnp.sum(e, axis=0, keepdims=True), 0.0)
        ocmp_ref[r] = _dot_tn(p.astype(BF16), vc)
        psum = psum + p
    imp = _dot(map_ref[...], psum, precision=lax.Precision.HIGHEST)
    blk = lax.broadcasted_iota(jnp.int32, (n_blk, tq), 0)
    cur = (i * tq + lax.broadcasted_iota(jnp.int32, (n_blk, tq), 1)) // SEL_BLOCK
    imp = jnp.where(blk > cur, -FORCE, imp)
    imp = jnp.where((blk == 0) | (blk == cur), FORCE, imp)
    rank = jnp.zeros((n_blk, tq), F32)
    for j in range(n_blk):
        row = imp[j:j + 1, :]
        ahead = (row > imp) | ((row == imp) & (blk > j))
        rank = rank + jnp.where(ahead, 1.0, 0.0)
    unsel = jnp.where(rank < SEL_TOPN, 0.0, 1.0)
    table = jnp.concatenate([jnp.zeros((HEAD_DIM, tq), F32), unsel,
                             jnp.zeros((LANES - HEAD_DIM - n_blk, tq), F32)], axis=0).astype(BF16)
    eye = jnp.where(lax.broadcasted_iota(jnp.int32, (LANES, LANES), 0)
                    == lax.broadcasted_iota(jnp.int32, (LANES, LANES), 1), 1.0, 0.0).astype(BF16)
    unsel_ref[0] = _dot_tn(table, eye).astype(BF16)


def _cmp_select(q8, kc, vc, cmp_map, B, S, tq=512):
    T = B * S
    n_blk = S // SEL_BLOCK
    nck = S // CMP_STRIDE
    nq = S // tq
    hp = HEADS_PER_GROUP
    return pl.pallas_call(
        functools.partial(_sel_kernel, tq=tq, n_blk=n_blk),
        grid=(B, KV_GROUPS, nq),
        in_specs=[
            pl.BlockSpec((hp, tq, LANES), lambda b, g, i: (g, b * nq + i, 0)),
            pl.BlockSpec((1, 1, nck, HEAD_DIM), lambda b, g, i: (b, g, 0, 0)),
            pl.BlockSpec((1, 1, nck, HEAD_DIM), lambda b, g, i: (b, g, 0, 0)),
            _const_spec((n_blk, nck)),
        ],
        out_specs=[
            pl.BlockSpec((hp, tq, HEAD_DIM), lambda b, g, i: (g, b * nq + i, 0)),
            pl.BlockSpec((1, tq, LANES), lambda b, g, i: (g, b * nq + i, 0)),
        ],
        out_shape=[
            jax.ShapeDtypeStruct((NSA_HEADS, T, HEAD_DIM), F32),
            jax.ShapeDtypeStruct((KV_GROUPS, T, LANES), BF16),
        ],
        compiler_params=_params("parallel", "parallel", "parallel"),
        name="cmp_select",
    )(q8, kc, vc, cmp_map)


def _attn_kernel(q_ref, unsel_ref, ks_ref, vs_ref, kw_ref, vw_ref, ocmp_ref, gl_ref, diag_ref, wint_ref,
                 o_ref, m_sc, acc_sc, *, tq, tk, S):
    i = pl.program_id(2)
    hp = HEADS_PER_GROUP
    rows = hp * tq
    q4 = (q_ref[...] + unsel_ref[...]).reshape(rows, LANES)

    m_sc[...] = jnp.full(m_sc.shape, NEG_INF, F32)
    acc_sc[...] = jnp.zeros(acc_sc.shape, F32)

    def sel_tile(k0, causal):
        k = ks_ref[0, pl.ds(k0, tk), :]
        v = vs_ref[0, pl.ds(k0, tk), :]
        s = _dot_nt(q4, k)
        if causal is not None:
            s = (s.reshape(hp, tq, tk) + causal[None]).reshape(rows, tk)
        m_old = m_sc[...]
        m_new = jnp.maximum(m_old, jnp.max(s, axis=-1, keepdims=True))
        p = jnp.exp2(s - jnp.concatenate([m_new] * (tk // LANES), axis=1))
        acc_sc[...] = jnp.exp2(m_old - m_new) * acc_sc[...] + _dot(p.astype(BF16), v)
        m_sc[...] = m_new

    n_full = (i * tq) // tk

    def full_step(j, carry):
        sel_tile(pl.multiple_of(j * tk, tk), None)
        return carry

    lax.fori_loop(0, n_full, full_step, 0)
    sel_tile(pl.multiple_of(n_full * tk, tk), diag_ref[0])
    acc = acc_sc[...]
    o_sel = acc[:, :HEAD_DIM] / acc[:, HEAD_DIM:HEAD_DIM + 1]

    span = WINDOW + tq
    w0 = pl.multiple_of(jnp.clip(i * tq - WINDOW, 0, S - span), LANES)
    kw = kw_ref[0, pl.ds(w0, span), :]
    vw = vw_ref[0, pl.ds(w0, span), :]
    s = (_dot_nt(q4[:, :HEAD_DIM], kw).reshape(hp, tq, span) + wint_ref[0][None]).reshape(rows, span)
    p = jnp.exp2(s - jnp.max(s, axis=-1, keepdims=True))
    accw = _dot(p.astype(BF16), vw)
    o_win = accw[:, :HEAD_DIM] / accw[:, HEAD_DIM:HEAD_DIM + 1]

    gl = gl_ref[0]
    outs = []
    for r in range(hp):
        rs = slice(r * tq, (r + 1) * tq)
        outs.append(gl[:, 3 * r:3 * r + 1] * ocmp_ref[r]
                    + gl[:, 3 * r + 1:3 * r + 2] * o_sel[rs]
                    + gl[:, 3 * r + 2:3 * r + 3] * o_win[rs])
    o_ref[...] = jnp.concatenate(outs, axis=-1).astype(BF16)


def _attention_tables(S, tq, tk):
    a = np.arange(tq)[:, None]
    c = np.arange(tk)[None, :]
    diag = np.stack([np.where(c <= p * tq + a, 0.0, -MASK_BIG) for p in range(tk // tq)])
    span = WINDOW + tq
    win = []
    for i in range(WINDOW // tq + 1):
        t = i * tq + a
        kp = min(max(i * tq - WINDOW, 0), S - span) + np.arange(span)[None, :]
        win.append(np.where((kp <= t) & (kp > t - WINDOW), 0.0, -MASK_BIG))
    return jnp.asarray(diag.astype(np.float32)), jnp.asarray(np.stack(win).astype(np.float32))


def _sparse_attention(q8, unsel, ksx, kwin, vx, ocmp, gls, tables, B, S, tq, tk):
    diag, win = tables
    T = B * S
    nq = S // tq
    hp = HEADS_PER_GROUP
    span = WINDOW + tq
    n_win = WINDOW // tq
    q_tile = lambda width: pl.BlockSpec((hp, tq, width), lambda b, g, i: (g, b * nq + i, 0))
    group_tile = pl.BlockSpec((1, tq, LANES), lambda b, g, i: (g, b * nq + i, 0))
    seq = lambda first, width: pl.BlockSpec((1, S, width), lambda b, g, i: (first + g, b, 0))
    return pl.pallas_call(
        functools.partial(_attn_kernel, tq=tq, tk=tk, S=S),
        grid=(B, KV_GROUPS, nq),
        in_specs=[
            q_tile(LANES), group_tile,
            seq(0, LANES), seq(0, LANES), seq(0, HEAD_DIM), seq(2, LANES),
            q_tile(HEAD_DIM), group_tile,
            pl.BlockSpec((1, tq, tk), lambda b, g, i: (i % (tk // tq), 0, 0)),
            pl.BlockSpec((1, tq, span), lambda b, g, i: (jnp.minimum(i, n_win), 0, 0)),
        ],
        out_specs=pl.BlockSpec((tq, hp * HEAD_DIM), lambda b, g, i: (b * nq + i, g)),
        out_shape=jax.ShapeDtypeStruct((T, NSA_WIDTH), BF16),
        scratch_shapes=[
            pltpu.VMEM((hp * tq, LANES), F32),
            pltpu.VMEM((hp * tq, LANES), F32),
        ],
        compiler_params=_params("parallel", "parallel", "parallel"),
        name="sparse_attention",
    )(q8, unsel, ksx, vx, kwin, vx, ocmp, gls, diag, win)


_POOL_HALO = 16


def _pool_kernel(u_ref, halo_ref, w_ref, scale_ref, y_ref, *, tp, S):
    i = pl.program_id(0)
    pos0 = (i * tp) % S
    halo = jnp.where(pos0 == 0, 0.0, halo_ref[...])
    x = jnp.concatenate([halo, u_ref[...]], axis=0)
    pos = pos0 + lax.broadcasted_iota(jnp.int32, (tp, 1), 0)
    for gi, w in enumerate(POOL_WINDOWS):
        c0 = gi * POOL_GROUP_WIDTH
        xg = x[:, c0:c0 + POOL_GROUP_WIDTH]
        tot = xg[_POOL_HALO:_POOL_HALO + tp]
        for d in range(1, w):
            tot = tot + xg[_POOL_HALO - d:_POOL_HALO - d + tp]
        cnt = jnp.minimum(pos + 1, w).astype(F32)
        pooled = tot / cnt - xg[_POOL_HALO:_POOL_HALO + tp]
        mixed = _dot(pooled.astype(BF16), w_ref[gi])
        y_ref[:, c0:c0 + POOL_GROUP_WIDTH] = (mixed * scale_ref[:, c0:c0 + POOL_GROUP_WIDTH]).astype(BF16)


def _pool_mixer(u, pool_w, pool_scale, S, tp=512):
    T = u.shape[0]
    hb = tp // _POOL_HALO
    return pl.pallas_call(
        functools.partial(_pool_kernel, tp=tp, S=S),
        grid=(T // tp,),
        in_specs=[
            pl.BlockSpec((tp, POOL_WIDTH), lambda i: (i, 0)),
            pl.BlockSpec((_POOL_HALO, POOL_WIDTH), lambda i: (jnp.maximum(i * hb - 1, 0), 0)),
            _const_spec((4, POOL_GROUP_WIDTH, POOL_GROUP_WIDTH)),
            _const_spec((1, POOL_WIDTH)),
        ],
        out_specs=pl.BlockSpec((tp, POOL_WIDTH), lambda i: (i, 0)),
        out_shape=jax.ShapeDtypeStruct((T, POOL_WIDTH), BF16),
        compiler_params=_params("parallel"),
        name="pool_mixer",
    )(u, u, pool_w, pool_scale)


def _merge_residual(onsa_ref, ypool_ref, mg_ref, h_ref, wn_ref, wp_ref, wo_ref):
    a = _dot(onsa_ref[...], wn_ref[...])
    b = _dot(ypool_ref[...], wp_ref[...])
    mg = mg_ref[...].astype(F32)
    merged = mg[:, :D_MODEL] * a + mg[:, D_MODEL:] * b
    return h_ref[...] + _dot(merged.astype(BF16), wo_ref[...])


def _post_dense_kernel(onsa_ref, ypool_ref, mg_ref, h_ref, wn_ref, wp_ref, wo_ref, g_ref, h1_ref, f_ref):
    h1 = _merge_residual(onsa_ref, ypool_ref, mg_ref, h_ref, wn_ref, wp_ref, wo_ref)
    h1_ref[...] = h1
    f_ref[...] = _rms(h1, g_ref[...]).astype(BF16)


def _post_moe_kernel(onsa_ref, ypool_ref, mg_ref, h_ref, wn_ref, wp_ref, wo_ref, g_ref, wr_ref, br_ref,
                     h1_ref, f_ref, route_ref, cnt_ref, base_sc, *, tm):
    @pl.when(pl.program_id(0) == 0)
    def _():
        base_sc[...] = jnp.zeros(base_sc.shape, F32)

    h1 = _merge_residual(onsa_ref, ypool_ref, mg_ref, h_ref, wn_ref, wp_ref, wo_ref)
    h1_ref[...] = h1
    f = _rms(h1, g_ref[...])
    for p in range(f_ref.shape[0]):
        f_ref[p] = f[:, p * f_ref.shape[2]:(p + 1) * f_ref.shape[2]]

    lane = lax.broadcasted_iota(jnp.int32, (tm, LANES), 1)
    f_hi = f.astype(BF16)
    f_lo = (f - f_hi.astype(F32)).astype(BF16)
    hi_terms = _dot(f_hi, wr_ref[...])
    logits = hi_terms[:, :LANES] + hi_terms[:, LANES:] + _dot(f_lo, wr_ref[:, :LANES]) + br_ref[...]
    lowest = jnp.finfo(F32).min
    lane_f = lane.astype(F32)
    lg = jnp.where(lane < N_EXPERTS, logits, lowest)
    m1 = jnp.max(lg, axis=-1, keepdims=True)
    i1 = jnp.min(jnp.where(lg == m1, lane_f, float(LANES)), axis=-1, keepdims=True)
    lg2 = jnp.where(lane_f == i1, lowest, lg)
    m2 = jnp.max(lg2, axis=-1, keepdims=True)
    i2 = jnp.min(jnp.where(lg2 == m2, lane_f, float(LANES)), axis=-1, keepdims=True)
    e2 = jnp.exp(m2 - m1)
    gate1 = 1.0 / (1.0 + e2)
    gate2 = e2 / (1.0 + e2)

    oh1 = lane_f == i1
    oh2 = lane_f == i2
    both = jnp.where(oh1 | oh2, 1.0, 0.0)
    r_i = lax.broadcasted_iota(jnp.int32, (tm, tm), 0)
    c_i = lax.broadcasted_iota(jnp.int32, (tm, tm), 1)
    before = jnp.where(r_i > c_i, 1.0, 0.0).astype(BF16)
    seen = _dot(before, both.astype(BF16)) + base_sc[...]
    rank1 = jnp.sum(jnp.where(oh1, seen, 0.0), axis=-1, keepdims=True)
    rank2 = jnp.sum(jnp.where(oh2, seen, 0.0), axis=-1, keepdims=True)
    base_sc[...] = base_sc[...] + jnp.sum(both, axis=0, keepdims=True)
    cnt_ref[...] = base_sc[...]

    route = jnp.zeros((tm, LANES), F32)
    for col, val in enumerate((i1, i2, gate1, gate2, rank1, rank2)):
        route = jnp.where(lane == col, val, route)
    route_ref[...] = route


def _post_mixer(onsa, ypool, mgs, h, wn, wp, wo, g_ffn, router=None, tm=512):
    T = h.shape[0]
    tile = lambda w: pl.BlockSpec((tm, w), lambda i: (i, 0))
    in_specs = [
        tile(NSA_WIDTH), tile(POOL_WIDTH), tile(2 * D_MODEL), tile(D_MODEL),
        _const_spec((NSA_WIDTH, D_MODEL)), _const_spec((POOL_WIDTH, D_MODEL)),
        _const_spec((D_MODEL, D_MODEL)), _const_spec((1, D_MODEL)),
    ]
    args = [onsa, ypool, mgs, h, wn, wp, wo, g_ffn]
    if router is None:
        return pl.pallas_call(
            _post_dense_kernel,
            grid=(T // tm,),
            in_specs=in_specs,
            out_specs=[tile(D_MODEL), tile(D_MODEL)],
            out_shape=[jax.ShapeDtypeStruct((T, D_MODEL), F32), jax.ShapeDtypeStruct((T, D_MODEL), BF16)],
            compiler_params=_params("parallel"),
            name="post_mixer_dense",
        )(*args)
    w_router, b_router = router
    return pl.pallas_call(
        functools.partial(_post_moe_kernel, tm=tm),
        grid=(T // tm,),
        in_specs=in_specs + [_const_spec((D_MODEL, 2 * LANES)), _const_spec((1, LANES))],
        out_specs=[tile(D_MODEL), pl.BlockSpec((N_PLANES, tm, PLANE_WIDTH), lambda i: (0, i, 0)), tile(LANES),
                   pl.BlockSpec((1, LANES), lambda i: (0, 0))],
        out_shape=[
            jax.ShapeDtypeStruct((T, D_MODEL), F32),
            jax.ShapeDtypeStruct((N_PLANES, T, PLANE_WIDTH), F32),
            jax.ShapeDtypeStruct((T, LANES), F32),
            jax.ShapeDtypeStruct((1, LANES), F32),
        ],
        scratch_shapes=[pltpu.VMEM((1, LANES), F32)],
        compiler_params=_params("arbitrary"),
        name="post_mixer_moe",
    )(*args, w_router, b_router)


def _ffn_kernel(f_ref, h_ref, wg_ref, wu_ref, wd_ref, o_ref, *, fc):
    f = f_ref[...]
    o_ref[...] = h_ref[...]
    for c in range(wg_ref.shape[1] // fc):
        g = _dot(f, wg_ref[:, c * fc:(c + 1) * fc])
        u = _dot(f, wu_ref[:, c * fc:(c + 1) * fc])
        o_ref[...] += _dot((jax.nn.silu(g) * u).astype(BF16), wd_ref[c * fc:(c + 1) * fc, :])


def _dense_ffn(f, h1, wg, wu, wd, tm=512, fc=256):
    T = f.shape[0]
    d_ff = wg.shape[1]
    tile = pl.BlockSpec((tm, D_MODEL), lambda i: (i, 0))
    return pl.pallas_call(
        functools.partial(_ffn_kernel, fc=fc),
        grid=(T // tm,),
        in_specs=[tile, tile, _const_spec((D_MODEL, d_ff)), _const_spec((D_MODEL, d_ff)),
                  _const_spec((d_ff, D_MODEL))],
        out_specs=tile,
        out_shape=jax.ShapeDtypeStruct((T, D_MODEL), F32),
        compiler_params=_params("parallel"),
        name="dense_ffn",
    )(f, h1, wg, wu, wd)


PLANE_WIDTH = 256
N_PLANES = D_MODEL // PLANE_WIDTH
SC_WINDOW = 128


def _sc_mesh():
    return plsc.VectorSubcoreMesh(core_axis_name="core", subcore_axis_name="subcore")


def _sc_scatter_pair(rows, idx_a, idx_b, n_out):
    n_rows, width = rows.shape

    @pl.kernel(out_type=jax.ShapeDtypeStruct((n_out, width), rows.dtype), mesh=_sc_mesh(), scratch_types=[])
    def scatter(rows_hbm, a_hbm, b_hbm, out_hbm):
        def body(rows_vmem, a_vmem, b_vmem):
            pltpu.sync_copy(rows_vmem, out_hbm.at[a_vmem.at[0]])
            pltpu.sync_copy(rows_vmem, out_hbm.at[b_vmem.at[0]])

        idx_spec = pl.BlockSpec((1, SC_WINDOW), lambda i: (0, i))
        pltpu.emit_pipeline(
            body,
            grid=(n_rows // SC_WINDOW,),
            in_specs=[pl.BlockSpec((SC_WINDOW, width), lambda i: (i, 0)), idx_spec, idx_spec],
            out_specs=[],
            core_axis_name=("core", "subcore"),
            dimension_semantics=(pltpu.PARALLEL,),
        )(rows_hbm, a_hbm, b_hbm)

    return scatter(rows, idx_a, idx_b)


def _sc_gather(rows, idx):
    n_out = idx.shape[1]
    width = rows.shape[1]

    @pl.kernel(out_type=jax.ShapeDtypeStruct((n_out, width), rows.dtype), mesh=_sc_mesh(), scratch_types=[])
    def gather(rows_hbm, idx_hbm, out_hbm):
        def body(idx_vmem, out_vmem):
            pltpu.sync_copy(rows_hbm.at[idx_vmem.at[0]], out_vmem)

        pltpu.emit_pipeline(
            body,
            grid=(n_out // SC_WINDOW,),
            in_specs=[pl.BlockSpec((1, SC_WINDOW), lambda i: (0, i))],
            out_specs=[pl.BlockSpec((SC_WINDOW, width), lambda i: (i, 0))],
            core_axis_name=("core", "subcore"),
            dimension_semantics=(pltpu.PARALLEL,),
        )(idx_hbm, out_hbm)

    return gather(rows, idx)


def _dispatch(f_planes, dest):
    n_planes, T, width = f_planes.shape
    n_rows = 2 * T
    plane_base = (jnp.arange(n_planes, dtype=jnp.int32) * n_rows)[:, None]
    idx = [(plane_base + dest[:, k][None, :]).reshape(1, n_planes * T) for k in range(2)]
    out = _sc_scatter_pair(f_planes.reshape(n_planes * T, width), idx[0], idx[1], n_planes * n_rows)
    return out.reshape(n_planes, n_rows, width)


def _undispatch(y_planes, dest):
    n_planes, n_rows, width = y_planes.shape
    T = dest.shape[0]
    plane_base = (jnp.arange(n_planes, dtype=jnp.int32) * n_rows)[None, :, None]
    idx = (plane_base + dest.T[:, None, :]).reshape(1, 2 * n_planes * T)
    out = _sc_gather(y_planes.reshape(n_planes * n_rows, width), idx)
    return out.reshape(2, n_planes, T, width)


def _combine_kernel(h_ref, route_ref, z_ref, g_ref, o_ref, *, final_norm):
    route = route_ref[...]
    z = [jnp.concatenate([z_ref[k, p] for p in range(N_PLANES)], axis=-1) for k in range(2)]
    out = h_ref[...] + (route[:, 2:3] * z[0] + route[:, 3:4] * z[1])
    if final_norm:
        out = _rms(out, g_ref[...])
    o_ref[...] = out


def _combine(h1, route, z, g_final, final_norm, tg=512):
    T = h1.shape[0]
    return pl.pallas_call(
        functools.partial(_combine_kernel, final_norm=final_norm),
        grid=(T // tg,),
        in_specs=[
            pl.BlockSpec((tg, D_MODEL), lambda i: (i, 0)),
            pl.BlockSpec((tg, LANES), lambda i: (i, 0)),
            pl.BlockSpec((2, N_PLANES, tg, PLANE_WIDTH), lambda i: (0, 0, i, 0)),
            _const_spec((1, D_MODEL)),
        ],
        out_specs=pl.BlockSpec((tg, D_MODEL), lambda i: (i, 0)),
        out_shape=jax.ShapeDtypeStruct((T, D_MODEL), F32),
        compiler_params=_params("parallel"),
        name="moe_combine",
    )(h1, route, z, g_final)


def _moe_kernel(blk_ref, exp_ref, lo_ref, hi_ref, first_ref, x_ref, wg_ref, wu_ref, wd_ref, y_ref, acc_sc, *, bm):
    v = pl.program_id(0)
    c = pl.program_id(1)

    @pl.when((first_ref[v] == 1) & (c == 0))
    def _():
        acc_sc[...] = jnp.zeros(acc_sc.shape, F32)

    lo = lo_ref[v]
    hi = hi_ref[v]

    @pl.when(hi > lo)
    def _():
        row = blk_ref[v] * bm + lax.broadcasted_iota(jnp.int32, (bm, 1), 0)
        x = jnp.concatenate([x_ref[p] for p in range(N_PLANES)], axis=-1)
        x = jnp.where((row >= lo) & (row < hi), x, 0.0).astype(BF16)
        g = _dot(x, wg_ref[0])
        u = _dot(x, wu_ref[0])
        acc_sc[...] += _dot((jax.nn.silu(g) * u).astype(BF16), wd_ref[0])

    @pl.when(c == pl.num_programs(1) - 1)
    def _():
        for p in range(N_PLANES):
            y_ref[p] = acc_sc[:, p * PLANE_WIDTH:(p + 1) * PLANE_WIDTH]


def _grouped_ffn(x_sorted, visits, wg, wu, wd, bm=1024, fc=512):
    n_rows = x_sorted.shape[1]
    n_visits = visits[0].shape[0]
    d_ff = wg.shape[2]
    rows_spec = pl.BlockSpec((N_PLANES, bm, PLANE_WIDTH), lambda v, c, blk, ex, lo, hi, fi: (0, blk[v], 0))
    grid_spec = pltpu.PrefetchScalarGridSpec(
        num_scalar_prefetch=5,
        grid=(n_visits, d_ff // fc),
        in_specs=[
            rows_spec,
            pl.BlockSpec((1, D_MODEL, fc), lambda v, c, blk, ex, lo, hi, fi: (ex[v], 0, c)),
            pl.BlockSpec((1, D_MODEL, fc), lambda v, c, blk, ex, lo, hi, fi: (ex[v], 0, c)),
            pl.BlockSpec((1, fc, D_MODEL), lambda v, c, blk, ex, lo, hi, fi: (ex[v], c, 0)),
        ],
        out_specs=rows_spec,
        scratch_shapes=[pltpu.VMEM((bm, D_MODEL), F32)],
    )
    return pl.pallas_call(
        functools.partial(_moe_kernel, bm=bm),
        grid_spec=grid_spec,
        out_shape=jax.ShapeDtypeStruct((N_PLANES, n_rows, PLANE_WIDTH), F32),
        compiler_params=_params("arbitrary", "arbitrary"),
        name="grouped_ffn",
    )(*visits, x_sorted, wg, wu, wd)


def _routing_tables(route, counts, n_rows, bm):
    counts = counts[0, :N_EXPERTS].astype(jnp.int32)
    ends = jnp.cumsum(counts)
    starts = ends - counts
    e = route[:, 0:2].astype(jnp.int32)
    seg_start = jnp.sum(jnp.where(e[..., None] == jnp.arange(N_EXPERTS), starts, 0), axis=-1)
    dest = seg_start + route[:, 4:6].astype(jnp.int32)
    n_blocks = n_rows // bm
    cuts = jnp.sort(jnp.concatenate([jnp.arange(n_blocks, dtype=jnp.int32) * bm, starts[1:]]))
    lo = cuts
    hi = jnp.concatenate([cuts[1:], jnp.array([n_rows], jnp.int32)])
    blk = jnp.minimum(lo // bm, n_blocks - 1)
    ex = jnp.minimum(jnp.sum(ends[None, :] <= lo[:, None], axis=1), N_EXPERTS - 1).astype(jnp.int32)
    first = jnp.concatenate([jnp.ones((1,), jnp.int32), (blk[1:] != blk[:-1]).astype(jnp.int32)])
    return dest, (blk, ex, lo, hi, first)


def _layer_weights(w_in, cmp_pos, cmp_w1, cmp_w2):
    w_main = jnp.concatenate([w_in[:, _Q0:_GL0], w_in[:, _U0:]], axis=1).astype(BF16)
    gl = w_in[:, _GL0:_U0].reshape(D_MODEL, KV_GROUPS, _GATES_PER_GROUP)
    w_gl = jnp.pad(gl, ((0, 0), (0, 0), (0, LANES - _GATES_PER_GROUP))).reshape(D_MODEL, KV_GROUPS * LANES)
    half = CMP_STRIDE * HEAD_DIM
    pos = cmp_pos.reshape(2, 2, half)
    w1 = cmp_w1.reshape(2, 2, half, CMP_HIDDEN).astype(BF16)
    return w_main, w_gl.astype(BF16), pos, w1, cmp_w2.astype(BF16)


def _cmp_to_sel_map(S):
    n_blk = S // SEL_BLOCK
    n_cmp = (S - CMP_LEN) // CMP_STRIDE + 1
    nck = S // CMP_STRIDE
    cs = np.arange(nck) * CMP_STRIDE
    bs = np.arange(n_blk) * SEL_BLOCK
    overlap = (cs[None, :] <= bs[:, None] + SEL_BLOCK - 1) & (cs[None, :] + CMP_LEN - 1 >= bs[:, None])
    overlap &= (np.arange(nck) < n_cmp)[None, :]
    return jnp.asarray(overlap.astype(np.float32))


def kernel(x, norm_mix_g, w_in, cmp_pos, cmp_w1, cmp_w2, w_nsa_up, pool_w, pool_scale, w_pool_up, w_out,
           norm_ffn_g, ffn_w_gate, ffn_w_up, ffn_w_down, moe_w_router, moe_b_router, moe_w_gate, moe_w_up,
           moe_w_down, final_norm_g):
    B, S, D = x.shape
    T = B * S
    depth = w_in.shape[0]
    cmp_map = _cmp_to_sel_map(S)
    attn_tq, attn_tk = 256, 512
    attn_tables = _attention_tables(S, attn_tq, attn_tk)
    h = x.reshape(T, D)
    bm = 1024
    for i in range(depth):
        w_main, w_gl, pos, w1, w2 = _layer_weights(w_in[i], cmp_pos[i], cmp_w1[i], cmp_w2[i])
        q8, xc, kwin, ksx, vx, u, mgs, gls = _in_proj(h, norm_mix_g[i][None], w_main, w_gl, S)
        kc, vc = _compress(xc, pos, w1, w2, B, S)
        ocmp, unsel = _cmp_select(q8, kc, vc, cmp_map, B, S)
        onsa = _sparse_attention(q8, unsel, ksx, kwin, vx, ocmp, gls, attn_tables, B, S, attn_tq, attn_tk)
        ypool = _pool_mixer(u, pool_w[i].astype(BF16), pool_scale[i][None], S)
        mix_w = (w_nsa_up[i].astype(BF16), w_pool_up[i].astype(BF16), w_out[i].astype(BF16), norm_ffn_g[i][None])
        j = i // 2
        if i % 2 == 0:
            h1, f = _post_mixer(onsa, ypool, mgs, h, *mix_w)
            h = _dense_ffn(f, h1, ffn_w_gate[j].astype(BF16), ffn_w_up[j].astype(BF16),
                           ffn_w_down[j].astype(BF16))
        else:
            w_router = jnp.pad(moe_w_router[j], ((0, 0), (0, LANES - N_EXPERTS)))
            w_router_hi = w_router.astype(BF16)
            w_router_lo = (w_router - w_router_hi.astype(F32)).astype(BF16)
            w_router = jnp.concatenate([w_router_hi, w_router_lo], axis=1)
            b_router = jnp.pad(moe_b_router[j], (0, LANES - N_EXPERTS))[None]
            h1, f, route, counts = _post_mixer(onsa, ypool, mgs, h, *mix_w, router=(w_router, b_router))
            dest, visits = _routing_tables(route, counts, 2 * T, bm)
            x_sorted = _dispatch(f, dest)
            y_sorted = _grouped_ffn(x_sorted, visits, moe_w_gate[j].astype(BF16), moe_w_up[j].astype(BF16),
                                    moe_w_down[j].astype(BF16), bm=bm)
            z = _undispatch(y_sorted, dest)
            h = _combine(h1, route, z, final_norm_g[None], final_norm=(i == depth - 1))
    return h.reshape(B, S, D)
```

```python
import functools

import numpy as np
import jax
import jax.numpy as jnp
from jax import lax
from jax.experimental import pallas as pl
from jax.experimental.pallas import tpu as pltpu
from jax.experimental.pallas import tpu_sc as plsc

D_MODEL = 1024
NSA_HEADS = 8
HEAD_DIM = 64
KV_GROUPS = 2
HEADS_PER_GROUP = NSA_HEADS // KV_GROUPS
NSA_WIDTH = NSA_HEADS * HEAD_DIM
KV_WIDTH = KV_GROUPS * HEAD_DIM
CMP_LEN = 32
CMP_STRIDE = 16
CMP_HIDDEN = 256
SEL_BLOCK = 64
SEL_TOPN = 8
WINDOW = 512
ATTN_SCALE = HEAD_DIM ** -0.5
POOL_WINDOWS = (2, 4, 8, 16)
POOL_GROUP_WIDTH = 128
POOL_WIDTH = 512
N_EXPERTS = 8
D_FF_EXPERT = 3584
RMS_EPS = 1e-6
NEG_INF = -1e30
FORCE = 1e9

LOG2E = 1.4426950408889634
MASK_BIG = 2.0 ** 100

LANES = 128
VMEM_LIMIT = 56 * 1024 * 1024

BF16 = jnp.bfloat16
F32 = jnp.float32

_Q0 = 0
_KV0 = NSA_WIDTH
_GL0 = _KV0 + 6 * KV_WIDTH
_U0 = _GL0 + 3 * NSA_HEADS
_MG0 = _U0 + POOL_WIDTH
_MAIN_COLS = NSA_WIDTH + 6 * KV_WIDTH + POOL_WIDTH
_GATES_PER_GROUP = 3 * HEADS_PER_GROUP


def _dot(a, b, **kw):
    return jnp.dot(a, b, preferred_element_type=F32, **kw)


def _dot_tn(a, b):
    return lax.dot_general(a, b, (((0,), (0,)), ((), ())), preferred_element_type=F32)


def _dot_nt(a, b):
    return lax.dot_general(a, b, (((1,), (1,)), ((), ())), preferred_element_type=F32)


def _rms(x, g):
    return x * lax.rsqrt(jnp.mean(x * x, axis=-1, keepdims=True) + RMS_EPS) * g


def _params(*sem):
    return pltpu.CompilerParams(dimension_semantics=sem, vmem_limit_bytes=VMEM_LIMIT)


def _const_spec(shape):
    nd = len(shape)
    return pl.BlockSpec(shape, lambda *_: (0,) * nd, pipeline_mode=pl.Buffered(1))


def _in_kernel(h_ref, g_ref, w_ref, wgl_ref, chunk_ref, q_ref, xc_ref, kw_ref, ksx_ref, vx_ref, u_ref, gl_ref, *, S):
    a = _rms(h_ref[...], g_ref[...]).astype(BF16)
    tm = a.shape[0]
    zeros = jnp.zeros((tm, HEAD_DIM), F32)
    ones = jnp.ones((tm, HEAD_DIM), F32)
    pos = (pl.program_id(0) * tm + lax.broadcasted_iota(jnp.int32, (tm, HEAD_DIM), 0)) % S
    lane = lax.broadcasted_iota(jnp.int32, (tm, HEAD_DIM), 1)
    block_code = jnp.where(pos // SEL_BLOCK == lane, -MASK_BIG, 0.0)

    def wide(lo, hi):
        return jnp.concatenate([lo, hi], axis=-1).astype(BF16)

    q = _dot(a, w_ref[:, 0:NSA_WIDTH]) * (ATTN_SCALE * LOG2E)
    for h in range(NSA_HEADS):
        q_ref[h] = wide(q[:, h * HEAD_DIM:(h + 1) * HEAD_DIM], zeros)
    for c in range(3):
        c0 = NSA_WIDTH + c * 256
        kv = _dot(a, w_ref[:, c0:c0 + 256])
        pieces = [kv[:, i * HEAD_DIM:(i + 1) * HEAD_DIM] for i in range(4)]
        if c == 0:
            n_chunks = tm // CMP_STRIDE
            for i in range(4):
                by_offset = _dot(chunk_ref[...], pieces[i].astype(BF16))
                for l in range(CMP_STRIDE):
                    xc_ref[i, :, l * HEAD_DIM:(l + 1) * HEAD_DIM] = (
                        by_offset[l * n_chunks:(l + 1) * n_chunks].astype(BF16))
            continue
        for g in range(KV_GROUPS):
            if c == 1:
                ksx_ref[g] = wide(pieces[g], block_code)
            else:
                kw_ref[g] = pieces[g].astype(BF16)
            vx_ref[2 * (c - 1) + g] = wide(pieces[2 + g], ones)
    u0 = NSA_WIDTH + 6 * KV_WIDTH
    u_ref[...] = _dot(a, w_ref[:, u0:u0 + POOL_WIDTH])
    gl = jax.nn.sigmoid(_dot(a, wgl_ref[...]))
    for g in range(KV_GROUPS):
        gl_ref[g] = gl[:, g * LANES:(g + 1) * LANES]


def _in_proj(h, g, w_main, w_gl, S, tm=512):
    T = h.shape[0]
    return pl.pallas_call(
        functools.partial(_in_kernel, S=S),
        grid=(T // tm,),
        in_specs=[
            pl.BlockSpec((tm, D_MODEL), lambda i: (i, 0)),
            _const_spec((1, D_MODEL)),
            _const_spec((D_MODEL, _MAIN_COLS)),
            _const_spec((D_MODEL, KV_GROUPS * LANES)),
            _const_spec((tm, tm)),
        ],
        out_specs=[
            pl.BlockSpec((NSA_HEADS, tm, LANES), lambda i: (0, i, 0)),
            pl.BlockSpec((4, tm // CMP_STRIDE, CMP_STRIDE * HEAD_DIM), lambda i: (0, i, 0)),
            pl.BlockSpec((KV_GROUPS, tm, HEAD_DIM), lambda i: (0, i, 0)),
            pl.BlockSpec((KV_GROUPS, tm, LANES), lambda i: (0, i, 0)),
            pl.BlockSpec((4, tm, LANES), lambda i: (0, i, 0)),
            pl.BlockSpec((tm, POOL_WIDTH), lambda i: (i, 0)),
            pl.BlockSpec((KV_GROUPS, tm, LANES), lambda i: (0, i, 0)),
        ],
        out_shape=[
            jax.ShapeDtypeStruct((NSA_HEADS, T, LANES), BF16),
            jax.ShapeDtypeStruct((4, T // CMP_STRIDE, CMP_STRIDE * HEAD_DIM), BF16),
            jax.ShapeDtypeStruct((KV_GROUPS, T, HEAD_DIM), BF16),
            jax.ShapeDtypeStruct((KV_GROUPS, T, LANES), BF16),
            jax.ShapeDtypeStruct((4, T, LANES), BF16),
            jax.ShapeDtypeStruct((T, POOL_WIDTH), F32),
            jax.ShapeDtypeStruct((KV_GROUPS, T, LANES), F32),
        ],
        compiler_params=_params("parallel"),
        name="in_proj",
    )(h, g, w_main, w_gl, _chunk_permutation(tm))


def _chunk_permutation(tm):
    n_chunks = tm // CMP_STRIDE
    perm = np.zeros((tm, tm), np.float32)
    l, n = np.meshgrid(np.arange(CMP_STRIDE), np.arange(n_chunks), indexing="ij")
    perm[(l * n_chunks + n).ravel(), (CMP_STRIDE * n + l).ravel()] = 1.0
    return jnp.asarray(perm, dtype=BF16)


def _cmp_kernel(x_ref, pos_ref, w1_ref, w2_ref, kc_ref, vc_ref):
    n_chunks = x_ref.shape[1]
    for j in range(2):
        out_ref = kc_ref if j == 0 else vc_ref
        for g in range(KV_GROUPS):
            x = x_ref[j * KV_GROUPS + g].astype(F32)
            lo = (x + pos_ref[j, 0:1, :]).astype(BF16)
            hi = (x + pos_ref[j, 1:2, :]).astype(BF16)
            hid = _dot(lo, w1_ref[j, 0]) + pltpu.roll(_dot(hi, w1_ref[j, 1]), n_chunks - 1, axis=0)
            act = jax.nn.gelu(hid).astype(BF16)
            out_ref[0, g] = _dot(act, w2_ref[j]).astype(BF16)


def _compress(x, pos, w1, w2, B, S):
    nck = S // CMP_STRIDE
    half = CMP_STRIDE * HEAD_DIM
    out = jax.ShapeDtypeStruct((B, KV_GROUPS, nck, HEAD_DIM), BF16)
    return pl.pallas_call(
        _cmp_kernel,
        grid=(B,),
        in_specs=[
            pl.BlockSpec((4, nck, half), lambda b: (0, b, 0)),
            _const_spec((2, 2, half)),
            _const_spec((2, 2, half, CMP_HIDDEN)),
            _const_spec((2, CMP_HIDDEN, HEAD_DIM)),
        ],
        out_specs=[pl.BlockSpec((1, KV_GROUPS, nck, HEAD_DIM), lambda b: (b, 0, 0, 0))] * 2,
        out_shape=[out, out],
        compiler_params=_params("parallel"),
        name="compress",
    )(x, pos, w1, w2)


def _sel_kernel(q_ref, kc_ref, vc_ref, map_ref, ocmp_ref, unsel_ref, *, tq, n_blk):
    i = pl.program_id(2)
    kc = kc_ref[0, 0]
    vc = vc_ref[0, 0]
    nck = kc.shape[0]
    n_idx = lax.broadcasted_iota(jnp.int32, (nck, tq), 0)
    t_idx = i * tq + lax.broadcasted_iota(jnp.int32, (nck, tq), 1)
    valid = n_idx * CMP_STRIDE + (CMP_LEN - 1) <= t_idx
    psum = jnp.zeros((nck, tq), F32)
    for r in range(HEADS_PER_GROUP):
        s = jnp.where(valid, _dot_nt(kc, q_ref[r][:, :HEAD_DIM]), NEG_INF)
        e = jnp.exp2(s - jnp.max(s, axis=0, keepdims=True))
        p = jnp.where(valid, e / jnp.sum(e, axis=0, keepdims=True), 0.0)
        ocmp_ref[r] = _dot_tn(p.astype(BF16), vc)
        psum = psum + p
    imp = _dot(map_ref[...], psum, precision=lax.Precision.HIGHEST)
    blk = lax.broadcasted_iota(jnp.int32, (n_blk, tq), 0)
    cur = (i * tq + lax.broadcasted_iota(jnp.int32, (n_blk, tq), 1)) // SEL_BLOCK
    imp = jnp.where(blk > cur, -FORCE, imp)
    imp = jnp.where((blk == 0) | (blk == cur), FORCE, imp)
    rank = jnp.zeros((n_blk, tq), F32)
    for j in range(n_blk):
        row = imp[j:j + 1, :]
        ahead = (row > imp) | ((row == imp) & (blk > j))
        rank = rank + jnp.where(ahead, 1.0, 0.0)
    unsel = jnp.where(rank < SEL_TOPN, 0.0, 1.0)
    table = jnp.concatenate([jnp.zeros((HEAD_DIM, tq), F32), unsel,
                             jnp.zeros((LANES - HEAD_DIM - n_blk, tq), F32)], axis=0).astype(BF16)
    eye = jnp.where(lax.broadcasted_iota(jnp.int32, (LANES, LANES), 0)
                    == lax.broadcasted_iota(jnp.int32, (LANES, LANES), 1), 1.0, 0.0).astype(BF16)
    unsel_ref[0] = _dot_tn(table, eye).astype(BF16)


def _cmp_select(q8, kc, vc, cmp_map, B, S, tq=512):
    T = B * S
    n_blk = S // SEL_BLOCK
    nck = S // CMP_STRIDE
    nq = S // tq
    hp = HEADS_PER_GROUP
    return pl.pallas_call(
        functools.partial(_sel_kernel, tq=tq, n_blk=n_blk),
        grid=(B, KV_GROUPS, nq),
        in_specs=[
            pl.BlockSpec((hp, tq, LANES), lambda b, g, i: (g, b * nq + i, 0)),
            pl.BlockSpec((1, 1, nck, HEAD_DIM), lambda b, g, i: (b, g, 0, 0)),
            pl.BlockSpec((1, 1, nck, HEAD_DIM), lambda b, g, i: (b, g, 0, 0)),
            _const_spec((n_blk, nck)),
        ],
        out_specs=[
            pl.BlockSpec((hp, tq, HEAD_DIM), lambda b, g, i: (g, b * nq + i, 0)),
            pl.BlockSpec((1, tq, LANES), lambda b, g, i: (g, b * nq + i, 0)),
        ],
        out_shape=[
            jax.ShapeDtypeStruct((NSA_HEADS, T, HEAD_DIM), F32),
            jax.ShapeDtypeStruct((KV_GROUPS, T, LANES), BF16),
        ],
        compiler_params=_params("parallel", "parallel", "parallel"),
        name="cmp_select",
    )(q8, kc, vc, cmp_map)


def _attn_kernel(q_ref, unsel_ref, ks_ref, vs_ref, kw_ref, vw_ref, ocmp_ref, gl_ref, diag_ref, wint_ref,
                 o_ref, m_sc, acc_sc, *, tq, tk, S):
    i = pl.program_id(2)
    hp = HEADS_PER_GROUP
    rows = hp * tq
    q4 = (q_ref[...] + unsel_ref[...]).reshape(rows, LANES)

    m_sc[...] = jnp.full(m_sc.shape, NEG_INF, F32)
    acc_sc[...] = jnp.zeros(acc_sc.shape, F32)

    def sel_tile(k0, causal):
        k = ks_ref[0, pl.ds(k0, tk), :]
        v = vs_ref[0, pl.ds(k0, tk), :]
        s = _dot_nt(q4, k)
        if causal is not None:
            s = (s.reshape(hp, tq, tk) + causal[None]).reshape(rows, tk)
        m_old = m_sc[...]
        m_new = jnp.maximum(m_old, jnp.max(s, axis=-1, keepdims=True))
        p = jnp.exp2(s - jnp.concatenate([m_new] * (tk // LANES), axis=1))
        acc_sc[...] = jnp.exp2(m_old - m_new) * acc_sc[...] + _dot(p.astype(BF16), v)
        m_sc[...] = m_new

    n_full = (i * tq) // tk

    def full_step(j, carry):
        sel_tile(pl.multiple_of(j * tk, tk), None)
        return carry

    lax.fori_loop(0, n_full, full_step, 0)
    sel_tile(pl.multiple_of(n_full * tk, tk), diag_ref[0])
    acc = acc_sc[...]
    o_sel = acc[:, :HEAD_DIM] / acc[:, HEAD_DIM:HEAD_DIM + 1]

    span = WINDOW + tq
    w0 = pl.multiple_of(jnp.clip(i * tq - WINDOW, 0, S - span), LANES)
    kw = kw_ref[0, pl.ds(w0, span), :]
    vw = vw_ref[0, pl.ds(w0, span), :]
    s = (_dot_nt(q4[:, :HEAD_DIM], kw).reshape(hp, tq, span) + wint_ref[0][None]).reshape(rows, span)
    p = jnp.exp2(s - jnp.max(s, axis=-1, keepdims=True))
    accw = _dot(p.astype(BF16), vw)
    o_win = accw[:, :HEAD_DIM] / accw[:, HEAD_DIM:HEAD_DIM + 1]

    gl = gl_ref[0]
    outs = []
    for r in range(hp):
        rs = slice(r * tq, (r + 1) * tq)
        outs.append(gl[:, 3 * r:3 * r + 1] * ocmp_ref[r]
                    + gl[:, 3 * r + 1:3 * r + 2] * o_sel[rs]
                    + gl[:, 3 * r + 2:3 * r + 3] * o_win[rs])
    o_ref[...] = jnp.concatenate(outs, axis=-1).astype(BF16)


def _attention_tables(S, tq, tk):
    a = np.arange(tq)[:, None]
    c = np.arange(tk)[None, :]
    diag = np.stack([np.where(c <= p * tq + a, 0.0, -MASK_BIG) for p in range(tk // tq)])
    span = WINDOW + tq
    win = []
    for i in range(WINDOW // tq + 1):
        t = i * tq + a
        kp = min(max(i * tq - WINDOW, 0), S - span) + np.arange(span)[None, :]
        win.append(np.where((kp <= t) & (kp > t - WINDOW), 0.0, -MASK_BIG))
    return jnp.asarray(diag.astype(np.float32)), jnp.asarray(np.stack(win).astype(np.float32))


def _sparse_attention(q8, unsel, ksx, kwin, vx, ocmp, gls, tables, B, S, tq, tk):
    diag, win = tables
    T = B * S
    nq = S // tq
    hp = HEADS_PER_GROUP
    span = WINDOW + tq
    n_win = WINDOW // tq
    q_tile = lambda width: pl.BlockSpec((hp, tq, width), lambda b, g, i: (g, b * nq + i, 0))
    group_tile = pl.BlockSpec((1, tq, LANES), lambda b, g, i: (g, b * nq + i, 0))
    seq = lambda first, width: pl.BlockSpec((1, S, width), lambda b, g, i: (first + g, b, 0))
    return pl.pallas_call(
        functools.partial(_attn_kernel, tq=tq, tk=tk, S=S),
        grid=(B, KV_GROUPS, nq),
        in_specs=[
            q_tile(LANES), group_tile,
            seq(0, LANES), seq(0, LANES), seq(0, HEAD_DIM), seq(2, LANES),
            q_tile(HEAD_DIM), group_tile,
            pl.BlockSpec((1, tq, tk), lambda b, g, i: (i % (tk // tq), 0, 0)),
            pl.BlockSpec((1, tq, span), lambda b, g, i: (jnp.minimum(i, n_win), 0, 0)),
        ],
        out_specs=pl.BlockSpec((tq, hp * HEAD_DIM), lambda b, g, i: (b * nq + i, g)),
        out_shape=jax.ShapeDtypeStruct((T, NSA_WIDTH), BF16),
        scratch_shapes=[
            pltpu.VMEM((hp * tq, LANES), F32),
            pltpu.VMEM((hp * tq, LANES), F32),
        ],
        compiler_params=_params("parallel", "parallel", "parallel"),
        name="sparse_attention",
    )(q8, unsel, ksx, vx, kwin, vx, ocmp, gls, diag, win)


_POOL_HALO = 16


def _pool_mixer(u_ref, halo_ref, w_ref, scale_ref, tm, S):
    pos0 = (pl.program_id(0) * tm) % S
    halo = jnp.where(pos0 == 0, 0.0, halo_ref[...])
    x = jnp.concatenate([halo, u_ref[...]], axis=0)
    pos = pos0 + lax.broadcasted_iota(jnp.int32, (tm, 1), 0)
    out = []
    for gi, w in enumerate(POOL_WINDOWS):
        c0 = gi * POOL_GROUP_WIDTH
        xg = x[:, c0:c0 + POOL_GROUP_WIDTH]
        tot = xg[_POOL_HALO:_POOL_HALO + tm]
        for d in range(1, w):
            tot = tot + xg[_POOL_HALO - d:_POOL_HALO - d + tm]
        cnt = jnp.minimum(pos + 1, w).astype(F32)
        pooled = tot / cnt - xg[_POOL_HALO:_POOL_HALO + tm]
        mixed = _dot(pooled.astype(BF16), w_ref[gi])
        out.append((mixed * scale_ref[:, c0:c0 + POOL_GROUP_WIDTH]).astype(BF16))
    return jnp.concatenate(out, axis=-1)


_N_MERGE_REFS = 11


def _merge_residual(onsa_ref, u_ref, halo_ref, h_ref, gmix_ref, wmg_ref, poolw_ref, pscale_ref, wn_ref, wp_ref,
                    wo_ref, *, tm, S):
    h = h_ref[...]
    a = _rms(h, gmix_ref[...]).astype(BF16)
    nsa_up = _dot(onsa_ref[...], wn_ref[...])
    pool_up = _dot(_pool_mixer(u_ref, halo_ref, poolw_ref, pscale_ref, tm, S), wp_ref[...])
    merged = (jax.nn.sigmoid(_dot(a, wmg_ref[:, :D_MODEL])) * nsa_up
              + jax.nn.sigmoid(_dot(a, wmg_ref[:, D_MODEL:])) * pool_up)
    return h + _dot(merged.astype(BF16), wo_ref[...])


def _post_dense_kernel(*refs, tm, S, fc):
    mix_refs = refs[:_N_MERGE_REFS]
    gffn_ref, wg_ref, wu_ref, wd_ref, o_ref = refs[_N_MERGE_REFS:]
    h1 = _merge_residual(*mix_refs, tm=tm, S=S)
    f = _rms(h1, gffn_ref[...]).astype(BF16)
    o_ref[...] = h1
    for c in range(wg_ref.shape[1] // fc):
        g = _dot(f, wg_ref[:, c * fc:(c + 1) * fc])
        u = _dot(f, wu_ref[:, c * fc:(c + 1) * fc])
        o_ref[...] += _dot((jax.nn.silu(g) * u).astype(BF16), wd_ref[c * fc:(c + 1) * fc, :])


def _post_moe_kernel(*refs, tm, S):
    mix_refs = refs[:_N_MERGE_REFS]
    g_ref, wr_ref, br_ref, h1_ref, f_ref, route_ref, cnt_ref, base_sc = refs[_N_MERGE_REFS:]

    @pl.when(pl.program_id(0) == 0)
    def _():
        base_sc[...] = jnp.zeros(base_sc.shape, F32)

    h1 = _merge_residual(*mix_refs, tm=tm, S=S)
    h1_ref[...] = h1
    f = _rms(h1, g_ref[...])
    for p in range(f_ref.shape[0]):
        f_ref[p] = f[:, p * f_ref.shape[2]:(p + 1) * f_ref.shape[2]]

    lane = lax.broadcasted_iota(jnp.int32, (tm, LANES), 1)
    f_hi = f.astype(BF16)
    f_lo = (f - f_hi.astype(F32)).astype(BF16)
    hi_terms = _dot(f_hi, wr_ref[...])
    logits = hi_terms[:, :LANES] + hi_terms[:, LANES:] + _dot(f_lo, wr_ref[:, :LANES]) + br_ref[...]
    lowest = jnp.finfo(F32).min
    lane_f = lane.astype(F32)
    lg = jnp.where(lane < N_EXPERTS, logits, lowest)
    m1 = jnp.max(lg, axis=-1, keepdims=True)
    i1 = jnp.min(jnp.where(lg == m1, lane_f, float(LANES)), axis=-1, keepdims=True)
    lg2 = jnp.where(lane_f == i1, lowest, lg)
    m2 = jnp.max(lg2, axis=-1, keepdims=True)
    i2 = jnp.min(jnp.where(lg2 == m2, lane_f, float(LANES)), axis=-1, keepdims=True)
    e2 = jnp.exp(m2 - m1)
    gate1 = 1.0 / (1.0 + e2)
    gate2 = e2 / (1.0 + e2)

    oh1 = lane_f == i1
    oh2 = lane_f == i2
    both = jnp.where(oh1 | oh2, 1.0, 0.0)
    r_i = lax.broadcasted_iota(jnp.int32, (tm, tm), 0)
    c_i = lax.broadcasted_iota(jnp.int32, (tm, tm), 1)
    before = jnp.where(r_i > c_i, 1.0, 0.0).astype(BF16)
    seen = _dot(before, both.astype(BF16)) + base_sc[...]
    rank1 = jnp.sum(jnp.where(oh1, seen, 0.0), axis=-1, keepdims=True)
    rank2 = jnp.sum(jnp.where(oh2, seen, 0.0), axis=-1, keepdims=True)
    base_sc[...] = base_sc[...] + jnp.sum(both, axis=0, keepdims=True)
    cnt_ref[...] = base_sc[...]

    route = jnp.zeros((tm, LANES), F32)
    for col, val in enumerate((i1, i2, gate1, gate2, rank1, rank2)):
        route = jnp.where(lane == col, val, route)
    route_ref[...] = route


def _post_mixer(onsa, u, h, mix_w, S, ffn=None, router=None, tm=512, fc=256):
    T = h.shape[0]
    g_mix, w_mg, pool_w, pool_scale, wn, wp, wo, g_ffn = mix_w
    tile = lambda w: pl.BlockSpec((tm, w), lambda i: (i, 0))
    halo_blocks = tm // _POOL_HALO
    in_specs = [
        tile(NSA_WIDTH), tile(POOL_WIDTH),
        pl.BlockSpec((_POOL_HALO, POOL_WIDTH), lambda i: (jnp.maximum(i * halo_blocks - 1, 0), 0)),
        tile(D_MODEL), _const_spec((1, D_MODEL)), _const_spec((D_MODEL, 2 * D_MODEL)),
        _const_spec((len(POOL_WINDOWS), POOL_GROUP_WIDTH, POOL_GROUP_WIDTH)), _const_spec((1, POOL_WIDTH)),
        _const_spec((NSA_WIDTH, D_MODEL)), _const_spec((POOL_WIDTH, D_MODEL)),
        _const_spec((D_MODEL, D_MODEL)), _const_spec((1, D_MODEL)),
    ]
    args = [onsa, u, u, h, g_mix, w_mg, pool_w, pool_scale, wn, wp, wo, g_ffn]
    if router is None:
        wg, wu, wd = ffn
        d_ff = wg.shape[1]
        return pl.pallas_call(
            functools.partial(_post_dense_kernel, tm=tm, S=S, fc=fc),
            grid=(T // tm,),
            in_specs=in_specs + [_const_spec((D_MODEL, d_ff)), _const_spec((D_MODEL, d_ff)),
                                 _const_spec((d_ff, D_MODEL))],
            out_specs=tile(D_MODEL),
            out_shape=jax.ShapeDtypeStruct((T, D_MODEL), F32),
            compiler_params=_params("parallel"),
            name="post_mixer_dense",
        )(*args, wg, wu, wd)
    w_router, b_router = router
    return pl.pallas_call(
        functools.partial(_post_moe_kernel, tm=tm, S=S),
        grid=(T // tm,),
        in_specs=in_specs + [_const_spec((D_MODEL, 2 * LANES)), _const_spec((1, LANES))],
        out_specs=[tile(D_MODEL), pl.BlockSpec((N_PLANES, tm, PLANE_WIDTH), lambda i: (0, i, 0)), tile(LANES),
                   pl.BlockSpec((1, LANES), lambda i: (0, 0))],
        out_shape=[
            jax.ShapeDtypeStruct((T, D_MODEL), F32),
            jax.ShapeDtypeStruct((N_PLANES, T, PLANE_WIDTH), F32),
            jax.ShapeDtypeStruct((T, LANES), F32),
            jax.ShapeDtypeStruct((1, LANES), F32),
        ],
        scratch_shapes=[pltpu.VMEM((1, LANES), F32)],
        compiler_params=_params("arbitrary"),
        name="post_mixer_moe",
    )(*args, w_router, b_router)


PLANE_WIDTH = 256
N_PLANES = D_MODEL // PLANE_WIDTH
SC_WINDOW = 128


def _sc_mesh():
    return plsc.VectorSubcoreMesh(core_axis_name="core", subcore_axis_name="subcore")


def _sc_scatter_pair(rows, idx_a, idx_b, n_out):
    n_rows, width = rows.shape

    @pl.kernel(out_type=jax.ShapeDtypeStruct((n_out, width), rows.dtype), mesh=_sc_mesh(), scratch_types=[])
    def scatter(rows_hbm, a_hbm, b_hbm, out_hbm):
        def body(rows_vmem, a_vmem, b_vmem):
            pltpu.sync_copy(rows_vmem, out_hbm.at[a_vmem.at[0]])
            pltpu.sync_copy(rows_vmem, out_hbm.at[b_vmem.at[0]])

        idx_spec = pl.BlockSpec((1, SC_WINDOW), lambda i: (0, i))
        pltpu.emit_pipeline(
            body,
            grid=(n_rows // SC_WINDOW,),
            in_specs=[pl.BlockSpec((SC_WINDOW, width), lambda i: (i, 0)), idx_spec, idx_spec],
            out_specs=[],
            core_axis_name=("core", "subcore"),
            dimension_semantics=(pltpu.PARALLEL,),
        )(rows_hbm, a_hbm, b_hbm)

    return scatter(rows, idx_a, idx_b)


def _sc_gather(rows, idx):
    n_out = idx.shape[1]
    width = rows.shape[1]

    @pl.kernel(out_type=jax.ShapeDtypeStruct((n_out, width), rows.dtype), mesh=_sc_mesh(), scratch_types=[])
    def gather(rows_hbm, idx_hbm, out_hbm):
        def body(idx_vmem, out_vmem):
            pltpu.sync_copy(rows_hbm.at[idx_vmem.at[0]], out_vmem)

        pltpu.emit_pipeline(
            body,
            grid=(n_out // SC_WINDOW,),
            in_specs=[pl.BlockSpec((1, SC_WINDOW), lambda i: (0, i))],
            out_specs=[pl.BlockSpec((SC_WINDOW, width), lambda i: (i, 0))],
            core_axis_name=("core", "subcore"),
            dimension_semantics=(pltpu.PARALLEL,),
        )(idx_hbm, out_hbm)

    return gather(rows, idx)


def _dispatch(f_planes, dest):
    n_planes, T, width = f_planes.shape
    n_rows = 2 * T
    plane_base = (jnp.arange(n_planes, dtype=jnp.int32) * n_rows)[:, None]
    idx = [(plane_base + dest[:, k][None, :]).reshape(1, n_planes * T) for k in range(2)]
    out = _sc_scatter_pair(f_planes.reshape(n_planes * T, width), idx[0], idx[1], n_planes * n_rows)
    return out.reshape(n_planes, n_rows, width)


def _undispatch(y_planes, dest):
    n_planes, n_rows, width = y_planes.shape
    T = dest.shape[0]
    plane_base = (jnp.arange(n_planes, dtype=jnp.int32) * n_rows)[None, :, None]
    idx = (plane_base + dest.T[:, None, :]).reshape(1, 2 * n_planes * T)
    out = _sc_gather(y_planes.reshape(n_planes * n_rows, width), idx)
    return out.reshape(2, n_planes, T, width)


def _combine_kernel(h_ref, route_ref, z_ref, g_ref, o_ref, *, final_norm):
    route = route_ref[...]
    z = [jnp.concatenate([z_ref[k, p] for p in range(N_PLANES)], axis=-1) for k in range(2)]
    out = h_ref[...] + (route[:, 2:3] * z[0] + route[:, 3:4] * z[1])
    if final_norm:
        out = _rms(out, g_ref[...])
    o_ref[...] = out


def _combine(h1, route, z, g_final, final_norm, tg=512):
    T = h1.shape[0]
    return pl.pallas_call(
        functools.partial(_combine_kernel, final_norm=final_norm),
        grid=(T // tg,),
        in_specs=[
            pl.BlockSpec((tg, D_MODEL), lambda i: (i, 0)),
            pl.BlockSpec((tg, LANES), lambda i: (i, 0)),
            pl.BlockSpec((2, N_PLANES, tg, PLANE_WIDTH), lambda i: (0, 0, i, 0)),
            _const_spec((1, D_MODEL)),
        ],
        out_specs=pl.BlockSpec((tg, D_MODEL), lambda i: (i, 0)),
        out_shape=jax.ShapeDtypeStruct((T, D_MODEL), F32),
        compiler_params=_params("parallel"),
        name="moe_combine",
    )(h1, route, z, g_final)


def _moe_kernel(blk_ref, exp_ref, lo_ref, hi_ref, first_ref, x_ref, wg_ref, wu_ref, wd_ref, y_ref, acc_sc, *, bm):
    v = pl.program_id(0)
    c = pl.program_id(1)

    @pl.when((first_ref[v] == 1) & (c == 0))
    def _():
        acc_sc[...] = jnp.zeros(acc_sc.shape, F32)

    lo = lo_ref[v]
    hi = hi_ref[v]

    @pl.when(hi > lo)
    def _():
        row = blk_ref[v] * bm + lax.broadcasted_iota(jnp.int32, (bm, 1), 0)
        x = jnp.concatenate([x_ref[p] for p in range(N_PLANES)], axis=-1)
        x = jnp.where((row >= lo) & (row < hi), x, 0.0).astype(BF16)
        g = _dot(x, wg_ref[0])
        u = _dot(x, wu_ref[0])
        acc_sc[...] += _dot((jax.nn.silu(g) * u).astype(BF16), wd_ref[0])

    @pl.when(c == pl.num_programs(1) - 1)
    def _():
        for p in range(N_PLANES):
            y_ref[p] = acc_sc[:, p * PLANE_WIDTH:(p + 1) * PLANE_WIDTH]


def _grouped_ffn(x_sorted, visits, wg, wu, wd, bm=1024, fc=512):
    n_rows = x_sorted.shape[1]
    n_visits = visits[0].shape[0]
    d_ff = wg.shape[2]
    rows_spec = pl.BlockSpec((N_PLANES, bm, PLANE_WIDTH), lambda v, c, blk, ex, lo, hi, fi: (0, blk[v], 0))
    grid_spec = pltpu.PrefetchScalarGridSpec(
        num_scalar_prefetch=5,
        grid=(n_visits, d_ff // fc),
        in_specs=[
            rows_spec,
            pl.BlockSpec((1, D_MODEL, fc), lambda v, c, blk, ex, lo, hi, fi: (ex[v], 0, c)),
            pl.BlockSpec((1, D_MODEL, fc), lambda v, c, blk, ex, lo, hi, fi: (ex[v], 0, c)),
            pl.BlockSpec((1, fc, D_MODEL), lambda v, c, blk, ex, lo, hi, fi: (ex[v], c, 0)),
        ],
        out_specs=rows_spec,
        scratch_shapes=[pltpu.VMEM((bm, D_MODEL), F32)],
    )
    return pl.pallas_call(
        functools.partial(_moe_kernel, bm=bm),
        grid_spec=grid_spec,
        out_shape=jax.ShapeDtypeStruct((N_PLANES, n_rows, PLANE_WIDTH), F32),
        compiler_params=_params("arbitrary", "arbitrary"),
        name="grouped_ffn",
    )(*visits, x_sorted, wg, wu, wd)


def _routing_tables(route, counts, n_rows, bm):
    counts = counts[0, :N_EXPERTS].astype(jnp.int32)
    ends = jnp.cumsum(counts)
    starts = ends - counts
    e = route[:, 0:2].astype(jnp.int32)
    seg_start = jnp.sum(jnp.where(e[..., None] == jnp.arange(N_EXPERTS), starts, 0), axis=-1)
    dest = seg_start + route[:, 4:6].astype(jnp.int32)
    n_blocks = n_rows // bm
    cuts = jnp.sort(jnp.concatenate([jnp.arange(n_blocks, dtype=jnp.int32) * bm, starts[1:]]))
    lo = cuts
    hi = jnp.concatenate([cuts[1:], jnp.array([n_rows], jnp.int32)])
    blk = jnp.minimum(lo // bm, n_blocks - 1)
    ex = jnp.minimum(jnp.sum(ends[None, :] <= lo[:, None], axis=1), N_EXPERTS - 1).astype(jnp.int32)
    first = jnp.concatenate([jnp.ones((1,), jnp.int32), (blk[1:] != blk[:-1]).astype(jnp.int32)])
    return dest, (blk, ex, lo, hi, first)


def _layer_weights(w_in, cmp_pos, cmp_w1, cmp_w2):
    w_main = jnp.concatenate([w_in[:, _Q0:_GL0], w_in[:, _U0:_MG0]], axis=1).astype(BF16)
    w_mg = w_in[:, _MG0:].astype(BF16)
    gl = w_in[:, _GL0:_U0].reshape(D_MODEL, KV_GROUPS, _GATES_PER_GROUP)
    w_gl = jnp.pad(gl, ((0, 0), (0, 0), (0, LANES - _GATES_PER_GROUP))).reshape(D_MODEL, KV_GROUPS * LANES)
    half = CMP_STRIDE * HEAD_DIM
    pos = cmp_pos.reshape(2, 2, half)
    w1 = cmp_w1.reshape(2, 2, half, CMP_HIDDEN).astype(BF16)
    return w_main, w_mg, w_gl.astype(BF16), pos, w1, cmp_w2.astype(BF16)


def _cmp_to_sel_map(S):
    n_blk = S // SEL_BLOCK
    n_cmp = (S - CMP_LEN) // CMP_STRIDE + 1
    nck = S // CMP_STRIDE
    cs = np.arange(nck) * CMP_STRIDE
    bs = np.arange(n_blk) * SEL_BLOCK
    overlap = (cs[None, :] <= bs[:, None] + SEL_BLOCK - 1) & (cs[None, :] + CMP_LEN - 1 >= bs[:, None])
    overlap &= (np.arange(nck) < n_cmp)[None, :]
    return jnp.asarray(overlap.astype(np.float32))


def kernel(x, norm_mix_g, w_in, cmp_pos, cmp_w1, cmp_w2, w_nsa_up, pool_w, pool_scale, w_pool_up, w_out,
           norm_ffn_g, ffn_w_gate, ffn_w_up, ffn_w_down, moe_w_router, moe_b_router, moe_w_gate, moe_w_up,
           moe_w_down, final_norm_g):
    B, S, D = x.shape
    T = B * S
    depth = w_in.shape[0]
    cmp_map = _cmp_to_sel_map(S)
    attn_tq, attn_tk = 256, 512
    attn_tables = _attention_tables(S, attn_tq, attn_tk)
    h = x.reshape(T, D)
    bm = 1024
    for i in range(depth):
        w_main, w_mg, w_gl, pos, w1, w2 = _layer_weights(w_in[i], cmp_pos[i], cmp_w1[i], cmp_w2[i])
        q8, xc, kwin, ksx, vx, u, gls = _in_proj(h, norm_mix_g[i][None], w_main, w_gl, S)
        kc, vc = _compress(xc, pos, w1, w2, B, S)
        ocmp, unsel = _cmp_select(q8, kc, vc, cmp_map, B, S)
        onsa = _sparse_attention(q8, unsel, ksx, kwin, vx, ocmp, gls, attn_tables, B, S, attn_tq, attn_tk)
        mix_w = (norm_mix_g[i][None], w_mg, pool_w[i].astype(BF16), pool_scale[i][None], w_nsa_up[i].astype(BF16),
                 w_pool_up[i].astype(BF16), w_out[i].astype(BF16), norm_ffn_g[i][None])
        j = i // 2
        if i % 2 == 0:
            ffn = (ffn_w_gate[j].astype(BF16), ffn_w_up[j].astype(BF16), ffn_w_down[j].astype(BF16))
            h = _post_mixer(onsa, u, h, mix_w, S, ffn=ffn)
        else:
            w_router = jnp.pad(moe_w_router[j], ((0, 0), (0, LANES - N_EXPERTS)))
            w_router_hi = w_router.astype(BF16)
            w_router_lo = (w_router - w_router_hi.astype(F32)).astype(BF16)
            w_router = jnp.concatenate([w_router_hi, w_router_lo], axis=1)
            b_router = jnp.pad(moe_b_router[j], (0, LANES - N_EXPERTS))[None]
            h1, f, route, counts = _post_mixer(onsa, u, h, mix_w, S, router=(w_router, b_router))
            dest, visits = _routing_tables(route, counts, 2 * T, bm)
            x_sorted = _dispatch(f, dest)
            y_sorted = _grouped_ffn(x_sorted, visits, moe_w_gate[j].astype(BF16), moe_w_up[j].astype(BF16),
                                    moe_w_down[j].astype(BF16), bm=bm)
            z = _undispatch(y_sorted, dest)
            h = _combine(h1, route, z, final_norm_g[None], final_norm=(i == depth - 1))
    return h.reshape(B, S, D)
```

```python
import functools

import numpy as np
import jax
import jax.numpy as jnp
from jax import lax
from jax.experimental import pallas as pl
from jax.experimental.pallas import tpu as pltpu
from jax.experimental.pallas import tpu_sc as plsc

D_MODEL = 1024
NSA_HEADS = 8
HEAD_DIM = 64
KV_GROUPS = 2
HEADS_PER_GROUP = NSA_HEADS // KV_GROUPS
NSA_WIDTH = NSA_HEADS * HEAD_DIM
KV_WIDTH = KV_GROUPS * HEAD_DIM
CMP_LEN = 32
CMP_STRIDE = 16
CMP_HIDDEN = 256
SEL_BLOCK = 64
SEL_TOPN = 8
WINDOW = 512
ATTN_SCALE = HEAD_DIM ** -0.5
POOL_WINDOWS = (2, 4, 8, 16)
POOL_GROUP_WIDTH = 128
POOL_WIDTH = 512
N_EXPERTS = 8
D_FF_EXPERT = 3584
RMS_EPS = 1e-6
NEG_INF = -1e30
FORCE = 1e9

LOG2E = 1.4426950408889634
MASK_BIG = 2.0 ** 100

LANES = 128
VMEM_LIMIT = 56 * 1024 * 1024

BF16 = jnp.bfloat16
F32 = jnp.float32

_Q0 = 0
_KV0 = NSA_WIDTH
_GL0 = _KV0 + 6 * KV_WIDTH
_U0 = _GL0 + 3 * NSA_HEADS
_MG0 = _U0 + POOL_WIDTH
_MAIN_COLS = NSA_WIDTH + 6 * KV_WIDTH + POOL_WIDTH
_GATES_PER_GROUP = 3 * HEADS_PER_GROUP


def _dot(a, b, **kw):
    return jnp.dot(a, b, preferred_element_type=F32, **kw)


def _dot_tn(a, b):
    return lax.dot_general(a, b, (((0,), (0,)), ((), ())), preferred_element_type=F32)


def _dot_nt(a, b):
    return lax.dot_general(a, b, (((1,), (1,)), ((), ())), preferred_element_type=F32)


def _rms(x, g):
    return x * lax.rsqrt(jnp.mean(x * x, axis=-1, keepdims=True) + RMS_EPS) * g


def _params(*sem):
    return pltpu.CompilerParams(dimension_semantics=sem, vmem_limit_bytes=VMEM_LIMIT)


def _const_spec(shape):
    nd = len(shape)
    return pl.BlockSpec(shape, lambda *_: (0,) * nd, pipeline_mode=pl.Buffered(1))


def _in_kernel(h_ref, g_ref, w_ref, wgl_ref, chunk_ref, q_ref, xc_ref, kw_ref, ksx_ref, vx_ref, u_ref, gl_ref, *, S):
    a = _rms(h_ref[...], g_ref[...]).astype(BF16)
    tm = a.shape[0]
    zeros = jnp.zeros((tm, HEAD_DIM), F32)
    ones = jnp.ones((tm, HEAD_DIM), F32)
    pos = (pl.program_id(0) * tm + lax.broadcasted_iota(jnp.int32, (tm, HEAD_DIM), 0)) % S
    lane = lax.broadcasted_iota(jnp.int32, (tm, HEAD_DIM), 1)
    block_code = jnp.where(pos // SEL_BLOCK == lane, -MASK_BIG, 0.0)

    def wide(lo, hi):
        return jnp.concatenate([lo, hi], axis=-1).astype(BF16)

    q = _dot(a, w_ref[:, 0:NSA_WIDTH]) * (ATTN_SCALE * LOG2E)
    for h in range(NSA_HEADS):
        q_ref[h] = wide(q[:, h * HEAD_DIM:(h + 1) * HEAD_DIM], zeros)
    for c in range(3):
        c0 = NSA_WIDTH + c * 256
        kv = _dot(a, w_ref[:, c0:c0 + 256])
        pieces = [kv[:, i * HEAD_DIM:(i + 1) * HEAD_DIM] for i in range(4)]
        if c == 0:
            n_chunks = tm // CMP_STRIDE
            for i in range(4):
                by_offset = _dot(chunk_ref[...], pieces[i].astype(BF16))
                for l in range(CMP_STRIDE):
                    xc_ref[i, :, l * HEAD_DIM:(l + 1) * HEAD_DIM] = (
                        by_offset[l * n_chunks:(l + 1) * n_chunks].astype(BF16))
            continue
        for g in range(KV_GROUPS):
            if c == 1:
                ksx_ref[g] = wide(pieces[g], block_code)
            else:
                kw_ref[g] = pieces[g].astype(BF16)
            vx_ref[2 * (c - 1) + g] = wide(pieces[2 + g], ones)
    u0 = NSA_WIDTH + 6 * KV_WIDTH
    u_ref[...] = _dot(a, w_ref[:, u0:u0 + POOL_WIDTH])
    gl = jax.nn.sigmoid(_dot(a, wgl_ref[...]))
    for g in range(KV_GROUPS):
        gl_ref[g] = gl[:, g * LANES:(g + 1) * LANES]


def _in_proj(h, g, w_main, w_gl, S, tm=512):
    T = h.shape[0]
    return pl.pallas_call(
        functools.partial(_in_kernel, S=S),
        grid=(T // tm,),
        in_specs=[
            pl.BlockSpec((tm, D_MODEL), lambda i: (i, 0)),
            _const_spec((1, D_MODEL)),
            _const_spec((D_MODEL, _MAIN_COLS)),
            _const_spec((D_MODEL, KV_GROUPS * LANES)),
            _const_spec((tm, tm)),
        ],
        out_specs=[
            pl.BlockSpec((NSA_HEADS, tm, LANES), lambda i: (0, i, 0)),
            pl.BlockSpec((4, tm // CMP_STRIDE, CMP_STRIDE * HEAD_DIM), lambda i: (0, i, 0)),
            pl.BlockSpec((KV_GROUPS, tm, HEAD_DIM), lambda i: (0, i, 0)),
            pl.BlockSpec((KV_GROUPS, tm, LANES), lambda i: (0, i, 0)),
            pl.BlockSpec((4, tm, LANES), lambda i: (0, i, 0)),
            pl.BlockSpec((tm, POOL_WIDTH), lambda i: (i, 0)),
            pl.BlockSpec((KV_GROUPS, tm, LANES), lambda i: (0, i, 0)),
        ],
        out_shape=[
            jax.ShapeDtypeStruct((NSA_HEADS, T, LANES), BF16),
            jax.ShapeDtypeStruct((4, T // CMP_STRIDE, CMP_STRIDE * HEAD_DIM), BF16),
            jax.ShapeDtypeStruct((KV_GROUPS, T, HEAD_DIM), BF16),
            jax.ShapeDtypeStruct((KV_GROUPS, T, LANES), BF16),
            jax.ShapeDtypeStruct((4, T, LANES), BF16),
            jax.ShapeDtypeStruct((T, POOL_WIDTH), F32),
            jax.ShapeDtypeStruct((KV_GROUPS, T, LANES), F32),
        ],
        compiler_params=_params("parallel"),
        name="in_proj",
    )(h, g, w_main, w_gl, _chunk_permutation(tm))


def _chunk_permutation(tm):
    n_chunks = tm // CMP_STRIDE
    perm = np.zeros((tm, tm), np.float32)
    l, n = np.meshgrid(np.arange(CMP_STRIDE), np.arange(n_chunks), indexing="ij")
    perm[(l * n_chunks + n).ravel(), (CMP_STRIDE * n + l).ravel()] = 1.0
    return jnp.asarray(perm, dtype=BF16)


def _cmp_kernel(x_ref, pos_ref, w1_ref, w2_ref, kc_ref, vc_ref):
    n_chunks = x_ref.shape[1]
    for j in range(2):
        out_ref = kc_ref if j == 0 else vc_ref
        for g in range(KV_GROUPS):
            x = x_ref[j * KV_GROUPS + g].astype(F32)
            lo = (x + pos_ref[j, 0:1, :]).astype(BF16)
            hi = (x + pos_ref[j, 1:2, :]).astype(BF16)
            hid = _dot(lo, w1_ref[j, 0]) + pltpu.roll(_dot(hi, w1_ref[j, 1]), n_chunks - 1, axis=0)
            act = jax.nn.gelu(hid).astype(BF16)
            out_ref[0, g] = _dot(act, w2_ref[j]).astype(BF16)


def _compress(x, pos, w1, w2, B, S):
    nck = S // CMP_STRIDE
    half = CMP_STRIDE * HEAD_DIM
    out = jax.ShapeDtypeStruct((B, KV_GROUPS, nck, HEAD_DIM), BF16)
    return pl.pallas_call(
        _cmp_kernel,
        grid=(B,),
        in_specs=[
            pl.BlockSpec((4, nck, half), lambda b: (0, b, 0)),
            _const_spec((2, 2, half)),
            _const_spec((2, 2, half, CMP_HIDDEN)),
            _const_spec((2, CMP_HIDDEN, HEAD_DIM)),
        ],
        out_specs=[pl.BlockSpec((1, KV_GROUPS, nck, HEAD_DIM), lambda b: (b, 0, 0, 0))] * 2,
        out_shape=[out, out],
        compiler_params=_params("parallel"),
        name="compress",
    )(x, pos, w1, w2)


def _sel_kernel(q_ref, kc_ref, vc_ref, map_ref, ocmp_ref, unsel_ref, *, tq, n_blk):
    i = pl.program_id(2)
    kc = kc_ref[0, 0]
    vc = vc_ref[0, 0]
    nck = kc.shape[0]
    n_idx = lax.broadcasted_iota(jnp.int32, (nck, tq), 0)
    t_idx = i * tq + lax.broadcasted_iota(jnp.int32, (nck, tq), 1)
    valid = n_idx * CMP_STRIDE + (CMP_LEN - 1) <= t_idx
    psum = jnp.zeros((nck, tq), F32)
    for r in range(HEADS_PER_GROUP):
        s = jnp.where(valid, _dot_nt(kc, q_ref[r][:, :HEAD_DIM]), NEG_INF)
        e = jnp.exp2(s - jnp.max(s, axis=0, keepdims=True))
        p = jnp.where(valid, e / jnp.sum(e, axis=0, keepdims=True), 0.0)
        ocmp_ref[r] = _dot_tn(p.astype(BF16), vc)
        psum = psum + p
    imp = _dot(map_ref[...], psum, precision=lax.Precision.HIGHEST)
    blk = lax.broadcasted_iota(jnp.int32, (n_blk, tq), 0)
    cur = (i * tq + lax.broadcasted_iota(jnp.int32, (n_blk, tq), 1)) // SEL_BLOCK
    imp = jnp.where(blk > cur, -FORCE, imp)
    imp = jnp.where((blk == 0) | (blk == cur), FORCE, imp)
    rank = jnp.zeros((n_blk, tq), F32)
    for j in range(n_blk):
        row = imp[j:j + 1, :]
        ahead = (row > imp) | ((row == imp) & (blk > j))
        rank = rank + jnp.where(ahead, 1.0, 0.0)
    unsel = jnp.where(rank < SEL_TOPN, 0.0, 1.0)
    table = jnp.concatenate([jnp.zeros((HEAD_DIM, tq), F32), unsel,
                             jnp.zeros((LANES - HEAD_DIM - n_blk, tq), F32)], axis=0).astype(BF16)
    eye = jnp.where(lax.broadcasted_iota(jnp.int32, (LANES, LANES), 0)
                    == lax.broadcasted_iota(jnp.int32, (LANES, LANES), 1), 1.0, 0.0).astype(BF16)
    unsel_ref[0] = _dot_tn(table, eye).astype(BF16)


def _cmp_select(q8, kc, vc, cmp_map, B, S, tq=512):
    T = B * S
    n_blk = S // SEL_BLOCK
    nck = S // CMP_STRIDE
    nq = S // tq
    hp = HEADS_PER_GROUP
    return pl.pallas_call(
        functools.partial(_sel_kernel, tq=tq, n_blk=n_blk),
        grid=(B, KV_GROUPS, nq),
        in_specs=[
            pl.BlockSpec((hp, tq, LANES), lambda b, g, i: (g, b * nq + i, 0)),
            pl.BlockSpec((1, 1, nck, HEAD_DIM), lambda b, g, i: (b, g, 0, 0)),
            pl.BlockSpec((1, 1, nck, HEAD_DIM), lambda b, g, i: (b, g, 0, 0)),
            _const_spec((n_blk, nck)),
        ],
        out_specs=[
            pl.BlockSpec((hp, tq, HEAD_DIM), lambda b, g, i: (g, b * nq + i, 0)),
            pl.BlockSpec((1, tq, LANES), lambda b, g, i: (g, b * nq + i, 0)),
        ],
        out_shape=[
            jax.ShapeDtypeStruct((NSA_HEADS, T, HEAD_DIM), F32),
            jax.ShapeDtypeStruct((KV_GROUPS, T, LANES), BF16),
        ],
        compiler_params=_params("parallel", "parallel", "parallel"),
        name="cmp_select",
    )(q8, kc, vc, cmp_map)


def _attn_kernel(q_ref, unsel_ref, ks_ref, vs_ref, kw_ref, vw_ref, ocmp_ref, gl_ref, diag_ref, wint_ref,
                 o_ref, m_sc, acc_sc, *, tq, tk, S):
    i = pl.program_id(2)
    hp = HEADS_PER_GROUP
    rows = hp * tq
    q4 = (q_ref[...] + unsel_ref[...]).reshape(rows, LANES)

    m_sc[...] = jnp.full(m_sc.shape, NEG_INF, F32)
    acc_sc[...] = jnp.zeros(acc_sc.shape, F32)

    def sel_tile(k0, causal):
        k = ks_ref[0, pl.ds(k0, tk), :]
        v = vs_ref[0, pl.ds(k0, tk), :]
        s = _dot_nt(q4, k)
        if causal is not None:
            s = (s.reshape(hp, tq, tk) + causal[None]).reshape(rows, tk)
        m_old = m_sc[...]
        m_new = jnp.maximum(m_old, jnp.max(s, axis=-1, keepdims=True))
        p = jnp.exp2(s - jnp.concatenate([m_new] * (tk // LANES), axis=1))
        acc_sc[...] = jnp.exp2(m_old - m_new) * acc_sc[...] + _dot(p.astype(BF16), v)
        m_sc[...] = m_new

    n_full = (i * tq) // tk

    def full_step(j, carry):
        sel_tile(pl.multiple_of(j * tk, tk), None)
        return carry

    lax.fori_loop(0, n_full, full_step, 0)
    sel_tile(pl.multiple_of(n_full * tk, tk), diag_ref[0])
    acc = acc_sc[...]
    o_sel = acc[:, :HEAD_DIM] / acc[:, HEAD_DIM:HEAD_DIM + 1]

    span = WINDOW + tq
    w0 = pl.multiple_of(jnp.clip(i * tq - WINDOW, 0, S - span), LANES)
    kw = kw_ref[0, pl.ds(w0, span), :]
    vw = vw_ref[0, pl.ds(w0, span), :]
    s = (_dot_nt(q4[:, :HEAD_DIM], kw).reshape(hp, tq, span) + wint_ref[0][None]).reshape(rows, span)
    p = jnp.exp2(s - jnp.max(s, axis=-1, keepdims=True))
    accw = _dot(p.astype(BF16), vw)
    o_win = accw[:, :HEAD_DIM] / accw[:, HEAD_DIM:HEAD_DIM + 1]

    gl = gl_ref[0]
    outs = []
    for r in range(hp):
        rs = slice(r * tq, (r + 1) * tq)
        outs.append(gl[:, 3 * r:3 * r + 1] * ocmp_ref[r]
                    + gl[:, 3 * r + 1:3 * r + 2] * o_sel[rs]
                    + gl[:, 3 * r + 2:3 * r + 3] * o_win[rs])
    o_ref[...] = jnp.concatenate(outs, axis=-1).astype(BF16)


def _attention_tables(S, tq, tk):
    a = np.arange(tq)[:, None]
    c = np.arange(tk)[None, :]
    diag = np.stack([np.where(c <= p * tq + a, 0.0, -MASK_BIG) for p in range(tk // tq)])
    span = WINDOW + tq
    win = []
    for i in range(WINDOW // tq + 1):
        t = i * tq + a
        kp = min(max(i * tq - WINDOW, 0), S - span) + np.arange(span)[None, :]
        win.append(np.where((kp <= t) & (kp > t - WINDOW), 0.0, -MASK_BIG))
    return jnp.asarray(diag.astype(np.float32)), jnp.asarray(np.stack(win).astype(np.float32))


def _sparse_attention(q8, unsel, ksx, kwin, vx, ocmp, gls, tables, B, S, tq, tk):
    diag, win = tables
    T = B * S
    nq = S // tq
    hp = HEADS_PER_GROUP
    span = WINDOW + tq
    n_win = WINDOW // tq
    q_tile = lambda width: pl.BlockSpec((hp, tq, width), lambda b, g, i: (g, b * nq + i, 0))
    group_tile = pl.BlockSpec((1, tq, LANES), lambda b, g, i: (g, b * nq + i, 0))
    seq = lambda first, width: pl.BlockSpec((1, S, width), lambda b, g, i: (first + g, b, 0))
    return pl.pallas_call(
        functools.partial(_attn_kernel, tq=tq, tk=tk, S=S),
        grid=(B, KV_GROUPS, nq),
        in_specs=[
            q_tile(LANES), group_tile,
            seq(0, LANES), seq(0, LANES), seq(0, HEAD_DIM), seq(2, LANES),
            q_tile(HEAD_DIM), group_tile,
            pl.BlockSpec((1, tq, tk), lambda b, g, i: (i % (tk // tq), 0, 0)),
            pl.BlockSpec((1, tq, span), lambda b, g, i: (jnp.minimum(i, n_win), 0, 0)),
        ],
        out_specs=pl.BlockSpec((tq, hp * HEAD_DIM), lambda b, g, i: (b * nq + i, g)),
        out_shape=jax.ShapeDtypeStruct((T, NSA_WIDTH), BF16),
        scratch_shapes=[
            pltpu.VMEM((hp * tq, LANES), F32),
            pltpu.VMEM((hp * tq, LANES), F32),
        ],
        compiler_params=_params("parallel", "parallel", "parallel"),
        name="sparse_attention",
    )(q8, unsel, ksx, vx, kwin, vx, ocmp, gls, diag, win)


_POOL_HALO = 16


def _pool_mixer(u_ref, halo_ref, w_ref, scale_ref, tm, S):
    pos0 = (pl.program_id(0) * tm) % S
    halo = jnp.where(pos0 == 0, 0.0, halo_ref[...])
    x = jnp.concatenate([halo, u_ref[...]], axis=0)
    pos = pos0 + lax.broadcasted_iota(jnp.int32, (tm, 1), 0)
    out = []
    for gi, w in enumerate(POOL_WINDOWS):
        c0 = gi * POOL_GROUP_WIDTH
        xg = x[:, c0:c0 + POOL_GROUP_WIDTH]
        tot = xg[_POOL_HALO:_POOL_HALO + tm]
        for d in range(1, w):
            tot = tot + xg[_POOL_HALO - d:_POOL_HALO - d + tm]
        cnt = jnp.minimum(pos + 1, w).astype(F32)
        pooled = tot / cnt - xg[_POOL_HALO:_POOL_HALO + tm]
        mixed = _dot(pooled.astype(BF16), w_ref[gi])
        out.append((mixed * scale_ref[:, c0:c0 + POOL_GROUP_WIDTH]).astype(BF16))
    return jnp.concatenate(out, axis=-1)


_N_MERGE_REFS = 11


def _merge_residual(onsa_ref, u_ref, halo_ref, h_ref, gmix_ref, wmg_ref, poolw_ref, pscale_ref, wn_ref, wp_ref,
                    wo_ref, *, tm, S):
    h = h_ref[...]
    a = _rms(h, gmix_ref[...]).astype(BF16)
    nsa_up = _dot(onsa_ref[...], wn_ref[...])
    pool_up = _dot(_pool_mixer(u_ref, halo_ref, poolw_ref, pscale_ref, tm, S), wp_ref[...])
    merged = (jax.nn.sigmoid(_dot(a, wmg_ref[:, :D_MODEL])) * nsa_up
              + jax.nn.sigmoid(_dot(a, wmg_ref[:, D_MODEL:])) * pool_up)
    return h + _dot(merged.astype(BF16), wo_ref[...])


def _post_dense_kernel(*refs, tm, S, fc):
    mix_refs = refs[:_N_MERGE_REFS]
    gffn_ref, wg_ref, wu_ref, wd_ref, o_ref = refs[_N_MERGE_REFS:]
    h1 = _merge_residual(*mix_refs, tm=tm, S=S)
    f = _rms(h1, gffn_ref[...]).astype(BF16)
    o_ref[...] = h1
    for c in range(wg_ref.shape[1] // fc):
        g = _dot(f, wg_ref[:, c * fc:(c + 1) * fc])
        u = _dot(f, wu_ref[:, c * fc:(c + 1) * fc])
        o_ref[...] += _dot((jax.nn.silu(g) * u).astype(BF16), wd_ref[c * fc:(c + 1) * fc, :])


def _post_moe_kernel(*refs, tm, S):
    mix_refs = refs[:_N_MERGE_REFS]
    g_ref, wr_ref, br_ref, h1_ref, f_ref, route_ref, cnt_ref, base_sc = refs[_N_MERGE_REFS:]

    @pl.when(pl.program_id(0) == 0)
    def _():
        base_sc[...] = jnp.zeros(base_sc.shape, F32)

    h1 = _merge_residual(*mix_refs, tm=tm, S=S)
    h1_ref[...] = h1
    f = _rms(h1, g_ref[...])
    width = f_ref.shape[2]
    for p in range(f_ref.shape[0]):
        f_ref[p] = _pack_bf16_pairs(f[:, 2 * p * width:(2 * p + 1) * width],
                                    f[:, (2 * p + 1) * width:(2 * p + 2) * width])

    lane = lax.broadcasted_iota(jnp.int32, (tm, LANES), 1)
    f_hi = f.astype(BF16)
    f_lo = (f - f_hi.astype(F32)).astype(BF16)
    hi_terms = _dot(f_hi, wr_ref[...])
    logits = hi_terms[:, :LANES] + hi_terms[:, LANES:] + _dot(f_lo, wr_ref[:, :LANES]) + br_ref[...]
    lowest = jnp.finfo(F32).min
    lane_f = lane.astype(F32)
    lg = jnp.where(lane < N_EXPERTS, logits, lowest)
    m1 = jnp.max(lg, axis=-1, keepdims=True)
    i1 = jnp.min(jnp.where(lg == m1, lane_f, float(LANES)), axis=-1, keepdims=True)
    lg2 = jnp.where(lane_f == i1, lowest, lg)
    m2 = jnp.max(lg2, axis=-1, keepdims=True)
    i2 = jnp.min(jnp.where(lg2 == m2, lane_f, float(LANES)), axis=-1, keepdims=True)
    e2 = jnp.exp(m2 - m1)
    gate1 = 1.0 / (1.0 + e2)
    gate2 = e2 / (1.0 + e2)

    oh1 = lane_f == i1
    oh2 = lane_f == i2
    both = jnp.where(oh1 | oh2, 1.0, 0.0)
    r_i = lax.broadcasted_iota(jnp.int32, (tm, tm), 0)
    c_i = lax.broadcasted_iota(jnp.int32, (tm, tm), 1)
    before = jnp.where(r_i > c_i, 1.0, 0.0).astype(BF16)
    seen = _dot(before, both.astype(BF16)) + base_sc[...]
    rank1 = jnp.sum(jnp.where(oh1, seen, 0.0), axis=-1, keepdims=True)
    rank2 = jnp.sum(jnp.where(oh2, seen, 0.0), axis=-1, keepdims=True)
    base_sc[...] = base_sc[...] + jnp.sum(both, axis=0, keepdims=True)
    cnt_ref[...] = base_sc[...]

    route = jnp.zeros((tm, LANES), F32)
    for col, val in enumerate((i1, i2, gate1, gate2, rank1, rank2)):
        route = jnp.where(lane == col, val, route)
    route_ref[...] = route


def _post_mixer(onsa, u, h, mix_w, S, ffn=None, router=None, tm=512, fc=256):
    T = h.shape[0]
    g_mix, w_mg, pool_w, pool_scale, wn, wp, wo, g_ffn = mix_w
    tile = lambda w: pl.BlockSpec((tm, w), lambda i: (i, 0))
    halo_blocks = tm // _POOL_HALO
    in_specs = [
        tile(NSA_WIDTH), tile(POOL_WIDTH),
        pl.BlockSpec((_POOL_HALO, POOL_WIDTH), lambda i: (jnp.maximum(i * halo_blocks - 1, 0), 0)),
        tile(D_MODEL), _const_spec((1, D_MODEL)), _const_spec((D_MODEL, 2 * D_MODEL)),
        _const_spec((len(POOL_WINDOWS), POOL_GROUP_WIDTH, POOL_GROUP_WIDTH)), _const_spec((1, POOL_WIDTH)),
        _const_spec((NSA_WIDTH, D_MODEL)), _const_spec((POOL_WIDTH, D_MODEL)),
        _const_spec((D_MODEL, D_MODEL)), _const_spec((1, D_MODEL)),
    ]
    args = [onsa, u, u, h, g_mix, w_mg, pool_w, pool_scale, wn, wp, wo, g_ffn]
    if router is None:
        wg, wu, wd = ffn
        d_ff = wg.shape[1]
        return pl.pallas_call(
            functools.partial(_post_dense_kernel, tm=tm, S=S, fc=fc),
            grid=(T // tm,),
            in_specs=in_specs + [_const_spec((D_MODEL, d_ff)), _const_spec((D_MODEL, d_ff)),
                                 _const_spec((d_ff, D_MODEL))],
            out_specs=tile(D_MODEL),
            out_shape=jax.ShapeDtypeStruct((T, D_MODEL), F32),
            compiler_params=_params("parallel"),
            name="post_mixer_dense",
        )(*args, wg, wu, wd)
    w_router, b_router = router
    return pl.pallas_call(
        functools.partial(_post_moe_kernel, tm=tm, S=S),
        grid=(T // tm,),
        in_specs=in_specs + [_const_spec((D_MODEL, 2 * LANES)), _const_spec((1, LANES))],
        out_specs=[tile(D_MODEL), pl.BlockSpec((N_PLANES // 2, tm, PLANE_WIDTH), lambda i: (0, i, 0)), tile(LANES),
                   pl.BlockSpec((1, LANES), lambda i: (0, 0))],
        out_shape=[
            jax.ShapeDtypeStruct((T, D_MODEL), F32),
            jax.ShapeDtypeStruct((N_PLANES // 2, T, PLANE_WIDTH), jnp.uint32),
            jax.ShapeDtypeStruct((T, LANES), F32),
            jax.ShapeDtypeStruct((1, LANES), F32),
        ],
        scratch_shapes=[pltpu.VMEM((1, LANES), F32)],
        compiler_params=_params("arbitrary"),
        name="post_mixer_moe",
    )(*args, w_router, b_router)


PLANE_WIDTH = 256
N_PLANES = D_MODEL // PLANE_WIDTH
SC_WINDOW = 128


def _sc_mesh():
    return plsc.VectorSubcoreMesh(core_axis_name="core", subcore_axis_name="subcore")


def _pack_bf16_pairs(lo, hi):
    lo_bits = lax.bitcast_convert_type(lo.astype(BF16).astype(F32), jnp.uint32)
    hi_bits = lax.bitcast_convert_type(hi.astype(BF16).astype(F32), jnp.uint32)
    return (lo_bits >> 16) | (hi_bits & jnp.uint32(0xFFFF0000))


def _unpack_bf16_pairs(words):
    lo = lax.bitcast_convert_type(words << 16, F32)
    hi = lax.bitcast_convert_type(words & jnp.uint32(0xFFFF0000), F32)
    return lo, hi


def _sc_scatter_pair(rows, idx_a, idx_b, n_out):
    n_rows, width = rows.shape

    @pl.kernel(out_type=jax.ShapeDtypeStruct((n_out, width), rows.dtype), mesh=_sc_mesh(), scratch_types=[])
    def scatter(rows_hbm, a_hbm, b_hbm, out_hbm):
        def body(rows_vmem, a_vmem, b_vmem):
            pltpu.sync_copy(rows_vmem, out_hbm.at[a_vmem.at[0]])
            pltpu.sync_copy(rows_vmem, out_hbm.at[b_vmem.at[0]])

        idx_spec = pl.BlockSpec((1, SC_WINDOW), lambda i: (0, i))
        pltpu.emit_pipeline(
            body,
            grid=(n_rows // SC_WINDOW,),
            in_specs=[pl.BlockSpec((SC_WINDOW, width), lambda i: (i, 0)), idx_spec, idx_spec],
            out_specs=[],
            core_axis_name=("core", "subcore"),
            dimension_semantics=(pltpu.PARALLEL,),
        )(rows_hbm, a_hbm, b_hbm)

    return scatter(rows, idx_a, idx_b)


def _sc_gather(rows, idx):
    n_out = idx.shape[1]
    width = rows.shape[1]

    @pl.kernel(out_type=jax.ShapeDtypeStruct((n_out, width), rows.dtype), mesh=_sc_mesh(), scratch_types=[])
    def gather(rows_hbm, idx_hbm, out_hbm):
        def body(idx_vmem, out_vmem):
            pltpu.sync_copy(rows_hbm.at[idx_vmem.at[0]], out_vmem)

        pltpu.emit_pipeline(
            body,
            grid=(n_out // SC_WINDOW,),
            in_specs=[pl.BlockSpec((1, SC_WINDOW), lambda i: (0, i))],
            out_specs=[pl.BlockSpec((SC_WINDOW, width), lambda i: (i, 0))],
            core_axis_name=("core", "subcore"),
            dimension_semantics=(pltpu.PARALLEL,),
        )(idx_hbm, out_hbm)

    return gather(rows, idx)


def _dispatch(f_planes, dest):
    n_planes, T, width = f_planes.shape
    n_rows = 2 * T
    plane_base = (jnp.arange(n_planes, dtype=jnp.int32) * n_rows)[:, None]
    idx = [(plane_base + dest[:, k][None, :]).reshape(1, n_planes * T) for k in range(2)]
    out = _sc_scatter_pair(f_planes.reshape(n_planes * T, width), idx[0], idx[1], n_planes * n_rows)
    return out.reshape(n_planes, n_rows, width)


def _undispatch(y_planes, dest):
    n_planes, n_rows, width = y_planes.shape
    T = dest.shape[0]
    plane_base = (jnp.arange(n_planes, dtype=jnp.int32) * n_rows)[None, :, None]
    idx = (plane_base + dest.T[:, None, :]).reshape(1, 2 * n_planes * T)
    out = _sc_gather(y_planes.reshape(n_planes * n_rows, width), idx)
    return out.reshape(2, n_planes, T, width)


def _combine_kernel(h_ref, route_ref, z_ref, g_ref, o_ref, *, final_norm):
    route = route_ref[...]
    z = [jnp.concatenate([z_ref[k, p] for p in range(N_PLANES)], axis=-1) for k in range(2)]
    out = h_ref[...] + (route[:, 2:3] * z[0] + route[:, 3:4] * z[1])
    if final_norm:
        out = _rms(out, g_ref[...])
    o_ref[...] = out


def _combine(h1, route, z, g_final, final_norm, tg=512):
    T = h1.shape[0]
    return pl.pallas_call(
        functools.partial(_combine_kernel, final_norm=final_norm),
        grid=(T // tg,),
        in_specs=[
            pl.BlockSpec((tg, D_MODEL), lambda i: (i, 0)),
            pl.BlockSpec((tg, LANES), lambda i: (i, 0)),
            pl.BlockSpec((2, N_PLANES, tg, PLANE_WIDTH), lambda i: (0, 0, i, 0)),
            _const_spec((1, D_MODEL)),
        ],
        out_specs=pl.BlockSpec((tg, D_MODEL), lambda i: (i, 0)),
        out_shape=jax.ShapeDtypeStruct((T, D_MODEL), F32),
        compiler_params=_params("parallel"),
        name="moe_combine",
    )(h1, route, z, g_final)


def _moe_kernel(blk_ref, exp_ref, lo_ref, hi_ref, first_ref, x_ref, wg_ref, wu_ref, wd_ref, y_ref, acc_sc, *, bm):
    v = pl.program_id(0)
    c = pl.program_id(1)

    @pl.when((first_ref[v] == 1) & (c == 0))
    def _():
        acc_sc[...] = jnp.zeros(acc_sc.shape, F32)

    lo = lo_ref[v]
    hi = hi_ref[v]

    @pl.when(hi > lo)
    def _():
        row = blk_ref[v] * bm + lax.broadcasted_iota(jnp.int32, (bm, 1), 0)
        x = jnp.concatenate([half for p in range(x_ref.shape[0]) for half in _unpack_bf16_pairs(x_ref[p])], axis=-1)
        x = jnp.where((row >= lo) & (row < hi), x, 0.0).astype(BF16)
        g = _dot(x, wg_ref[0].astype(BF16))
        u = _dot(x, wu_ref[0].astype(BF16))
        acc_sc[...] += _dot((jax.nn.silu(g) * u).astype(BF16), wd_ref[0].astype(BF16))

    @pl.when(c == pl.num_programs(1) - 1)
    def _():
        for p in range(N_PLANES):
            y_ref[p] = acc_sc[:, p * PLANE_WIDTH:(p + 1) * PLANE_WIDTH]


def _grouped_ffn(x_sorted, visits, wg, wu, wd, bm=1024, fc=512):
    n_rows = x_sorted.shape[1]
    n_visits = visits[0].shape[0]
    d_ff = wg.shape[2]
    rows_spec = pl.BlockSpec((N_PLANES, bm, PLANE_WIDTH), lambda v, c, blk, ex, lo, hi, fi: (0, blk[v], 0))
    grid_spec = pltpu.PrefetchScalarGridSpec(
        num_scalar_prefetch=5,
        grid=(n_visits, d_ff // fc),
        in_specs=[
            pl.BlockSpec((x_sorted.shape[0], bm, PLANE_WIDTH), lambda v, c, blk, ex, lo, hi, fi: (0, blk[v], 0)),
            pl.BlockSpec((1, D_MODEL, fc), lambda v, c, blk, ex, lo, hi, fi: (ex[v], 0, c)),
            pl.BlockSpec((1, D_MODEL, fc), lambda v, c, blk, ex, lo, hi, fi: (ex[v], 0, c)),
            pl.BlockSpec((1, fc, D_MODEL), lambda v, c, blk, ex, lo, hi, fi: (ex[v], c, 0)),
        ],
        out_specs=rows_spec,
        scratch_shapes=[pltpu.VMEM((bm, D_MODEL), F32)],
    )
    return pl.pallas_call(
        functools.partial(_moe_kernel, bm=bm),
        grid_spec=grid_spec,
        out_shape=jax.ShapeDtypeStruct((N_PLANES, n_rows, PLANE_WIDTH), F32),
        compiler_params=_params("arbitrary", "arbitrary"),
        name="grouped_ffn",
    )(*visits, x_sorted, wg, wu, wd)


def _routing_tables(route, counts, n_rows, bm):
    counts = counts[0, :N_EXPERTS].astype(jnp.int32)
    ends = jnp.cumsum(counts)
    starts = ends - counts
    e = route[:, 0:2].astype(jnp.int32)
    seg_start = jnp.sum(jnp.where(e[..., None] == jnp.arange(N_EXPERTS), starts, 0), axis=-1)
    dest = seg_start + route[:, 4:6].astype(jnp.int32)
    n_blocks = n_rows // bm
    cuts = jnp.sort(jnp.concatenate([jnp.arange(n_blocks, dtype=jnp.int32) * bm, starts[1:]]))
    lo = cuts
    hi = jnp.concatenate([cuts[1:], jnp.array([n_rows], jnp.int32)])
    blk = jnp.minimum(lo // bm, n_blocks - 1)
    ex = jnp.minimum(jnp.sum(ends[None, :] <= lo[:, None], axis=1), N_EXPERTS - 1).astype(jnp.int32)
    first = jnp.concatenate([jnp.ones((1,), jnp.int32), (blk[1:] != blk[:-1]).astype(jnp.int32)])
    return dest, (blk, ex, lo, hi, first)


def _layer_weights(w_in, cmp_pos, cmp_w1, cmp_w2):
    w_main = jnp.concatenate([w_in[:, _Q0:_GL0], w_in[:, _U0:_MG0]], axis=1).astype(BF16)
    w_mg = w_in[:, _MG0:].astype(BF16)
    gl = w_in[:, _GL0:_U0].reshape(D_MODEL, KV_GROUPS, _GATES_PER_GROUP)
    w_gl = jnp.pad(gl, ((0, 0), (0, 0), (0, LANES - _GATES_PER_GROUP))).reshape(D_MODEL, KV_GROUPS * LANES)
    half = CMP_STRIDE * HEAD_DIM
    pos = cmp_pos.reshape(2, 2, half)
    w1 = cmp_w1.reshape(2, 2, half, CMP_HIDDEN).astype(BF16)
    return w_main, w_mg, w_gl.astype(BF16), pos, w1, cmp_w2.astype(BF16)


def _cmp_to_sel_map(S):
    n_blk = S // SEL_BLOCK
    n_cmp = (S - CMP_LEN) // CMP_STRIDE + 1
    nck = S // CMP_STRIDE
    cs = np.arange(nck) * CMP_STRIDE
    bs = np.arange(n_blk) * SEL_BLOCK
    overlap = (cs[None, :] <= bs[:, None] + SEL_BLOCK - 1) & (cs[None, :] + CMP_LEN - 1 >= bs[:, None])
    overlap &= (np.arange(nck) < n_cmp)[None, :]
    return jnp.asarray(overlap.astype(np.float32))


def kernel(x, norm_mix_g, w_in, cmp_pos, cmp_w1, cmp_w2, w_nsa_up, pool_w, pool_scale, w_pool_up, w_out,
           norm_ffn_g, ffn_w_gate, ffn_w_up, ffn_w_down, moe_w_router, moe_b_router, moe_w_gate, moe_w_up,
           moe_w_down, final_norm_g):
    B, S, D = x.shape
    T = B * S
    depth = w_in.shape[0]
    cmp_map = _cmp_to_sel_map(S)
    attn_tq, attn_tk = 256, 512
    attn_tables = _attention_tables(S, attn_tq, attn_tk)
    h = x.reshape(T, D)
    bm = 1024
    for i in range(depth):
        w_main, w_mg, w_gl, pos, w1, w2 = _layer_weights(w_in[i], cmp_pos[i], cmp_w1[i], cmp_w2[i])
        q8, xc, kwin, ksx, vx, u, gls = _in_proj(h, norm_mix_g[i][None], w_main, w_gl, S)
        kc, vc = _compress(xc, pos, w1, w2, B, S)
        ocmp, unsel = _cmp_select(q8, kc, vc, cmp_map, B, S)
        onsa = _sparse_attention(q8, unsel, ksx, kwin, vx, ocmp, gls, attn_tables, B, S, attn_tq, attn_tk)
        mix_w = (norm_mix_g[i][None], w_mg, pool_w[i].astype(BF16), pool_scale[i][None], w_nsa_up[i].astype(BF16),
                 w_pool_up[i].astype(BF16), w_out[i].astype(BF16), norm_ffn_g[i][None])
        j = i // 2
        if i % 2 == 0:
            ffn = (ffn_w_gate[j].astype(BF16), ffn_w_up[j].astype(BF16), ffn_w_down[j].astype(BF16))
            h = _post_mixer(onsa, u, h, mix_w, S, ffn=ffn)
        else:
            w_router = jnp.pad(moe_w_router[j], ((0, 0), (0, LANES - N_EXPERTS)))
            w_router_hi = w_router.astype(BF16)
            w_router_lo = (w_router - w_router_hi.astype(F32)).astype(BF16)
            w_router = jnp.concatenate([w_router_hi, w_router_lo], axis=1)
            b_router = jnp.pad(moe_b_router[j], (0, LANES - N_EXPERTS))[None]
            h1, f, route, counts = _post_mixer(onsa, u, h, mix_w, S, router=(w_router, b_router))
            dest, visits = _routing_tables(route, counts, 2 * T, bm)
            x_sorted = _dispatch(f, dest)
            y_sorted = _grouped_ffn(x_sorted, visits, moe_w_gate[j], moe_w_up[j], moe_w_down[j], bm=bm)
            z = _undispatch(y_sorted, dest)
            h = _combine(h1, route, z, final_norm_g[None], final_norm=(i == depth - 1))
    return h.reshape(B, S, D)
```

```python
import functools

import numpy as np
import jax
import jax.numpy as jnp
from jax import lax
from jax.experimental import pallas as pl
from jax.experimental.pallas import tpu as pltpu
from jax.experimental.pallas import tpu_sc as plsc

D_MODEL = 1024
NSA_HEADS = 8
HEAD_DIM = 64
KV_GROUPS = 2
HEADS_PER_GROUP = NSA_HEADS // KV_GROUPS
NSA_WIDTH = NSA_HEADS * HEAD_DIM
KV_WIDTH = KV_GROUPS * HEAD_DIM
CMP_LEN = 32
CMP_STRIDE = 16
CMP_HIDDEN = 256
SEL_BLOCK = 64
SEL_TOPN = 8
WINDOW = 512
ATTN_SCALE = HEAD_DIM ** -0.5
POOL_WINDOWS = (2, 4, 8, 16)
POOL_GROUP_WIDTH = 128
POOL_WIDTH = 512
N_EXPERTS = 8
D_FF_EXPERT = 3584
RMS_EPS = 1e-6
NEG_INF = -1e30
FORCE = 1e9

LOG2E = 1.4426950408889634
MASK_BIG = 2.0 ** 100

LANES = 128
VMEM_LIMIT = 56 * 1024 * 1024

BF16 = jnp.bfloat16
F32 = jnp.float32

_Q0 = 0
_KV0 = NSA_WIDTH
_GL0 = _KV0 + 6 * KV_WIDTH
_U0 = _GL0 + 3 * NSA_HEADS
_MG0 = _U0 + POOL_WIDTH
_MAIN_COLS = NSA_WIDTH + 6 * KV_WIDTH + POOL_WIDTH
_GATES_PER_GROUP = 3 * HEADS_PER_GROUP


def _dot(a, b, **kw):
    return jnp.dot(a, b, preferred_element_type=F32, **kw)


def _dot_tn(a, b):
    return lax.dot_general(a, b, (((0,), (0,)), ((), ())), preferred_element_type=F32)


def _dot_nt(a, b):
    return lax.dot_general(a, b, (((1,), (1,)), ((), ())), preferred_element_type=F32)


def _rms(x, g):
    return x * lax.rsqrt(jnp.mean(x * x, axis=-1, keepdims=True) + RMS_EPS) * g


def _params(*sem):
    return pltpu.CompilerParams(dimension_semantics=sem, vmem_limit_bytes=VMEM_LIMIT)


def _const_spec(shape):
    nd = len(shape)
    return pl.BlockSpec(shape, lambda *_: (0,) * nd, pipeline_mode=pl.Buffered(1))


def _in_kernel(h_ref, g_ref, w_ref, wgl_ref, chunk_ref, q_ref, xc_ref, kw_ref, ksx_ref, vx_ref, u_ref, gl_ref, *, S):
    a = _rms(h_ref[...], g_ref[...]).astype(BF16)
    tm = a.shape[0]
    zeros = jnp.zeros((tm, HEAD_DIM), F32)
    ones = jnp.ones((tm, HEAD_DIM), F32)
    pos = (pl.program_id(0) * tm + lax.broadcasted_iota(jnp.int32, (tm, HEAD_DIM), 0)) % S
    lane = lax.broadcasted_iota(jnp.int32, (tm, HEAD_DIM), 1)
    block_code = jnp.where(pos // SEL_BLOCK == lane, -MASK_BIG, 0.0)

    def wide(lo, hi):
        return jnp.concatenate([lo, hi], axis=-1).astype(BF16)

    q = _dot(a, w_ref[:, 0:NSA_WIDTH]) * (ATTN_SCALE * LOG2E)
    for h in range(NSA_HEADS):
        q_ref[h] = wide(q[:, h * HEAD_DIM:(h + 1) * HEAD_DIM], zeros)
    for c in range(3):
        c0 = NSA_WIDTH + c * 256
        kv = _dot(a, w_ref[:, c0:c0 + 256])
        pieces = [kv[:, i * HEAD_DIM:(i + 1) * HEAD_DIM] for i in range(4)]
        if c == 0:
            n_chunks = tm // CMP_STRIDE
            for i in range(4):
                by_offset = _dot(chunk_ref[...], pieces[i].astype(BF16))
                for l in range(CMP_STRIDE):
                    xc_ref[i, :, l * HEAD_DIM:(l + 1) * HEAD_DIM] = (
                        by_offset[l * n_chunks:(l + 1) * n_chunks].astype(BF16))
            continue
        for g in range(KV_GROUPS):
            if c == 1:
                ksx_ref[g] = wide(pieces[g], block_code)
            else:
                kw_ref[g] = pieces[g].astype(BF16)
            vx_ref[2 * (c - 1) + g] = wide(pieces[2 + g], ones)
    u0 = NSA_WIDTH + 6 * KV_WIDTH
    u_ref[...] = _dot(a, w_ref[:, u0:u0 + POOL_WIDTH])
    gl = jax.nn.sigmoid(_dot(a, wgl_ref[...]))
    for g in range(KV_GROUPS):
        gl_ref[g] = gl[:, g * LANES:(g + 1) * LANES]


def _in_proj(h, g, w_main, w_gl, S, tm=512):
    T = h.shape[0]
    return pl.pallas_call(
        functools.partial(_in_kernel, S=S),
        grid=(T // tm,),
        in_specs=[
            pl.BlockSpec((tm, D_MODEL), lambda i: (i, 0)),
            _const_spec((1, D_MODEL)),
            _const_spec((D_MODEL, _MAIN_COLS)),
            _const_spec((D_MODEL, KV_GROUPS * LANES)),
            _const_spec((tm, tm)),
        ],
        out_specs=[
            pl.BlockSpec((NSA_HEADS, tm, LANES), lambda i: (0, i, 0)),
            pl.BlockSpec((4, tm // CMP_STRIDE, CMP_STRIDE * HEAD_DIM), lambda i: (0, i, 0)),
            pl.BlockSpec((KV_GROUPS, tm, HEAD_DIM), lambda i: (0, i, 0)),
            pl.BlockSpec((KV_GROUPS, tm, LANES), lambda i: (0, i, 0)),
            pl.BlockSpec((4, tm, LANES), lambda i: (0, i, 0)),
            pl.BlockSpec((tm, POOL_WIDTH), lambda i: (i, 0)),
            pl.BlockSpec((KV_GROUPS, tm, LANES), lambda i: (0, i, 0)),
        ],
        out_shape=[
            jax.ShapeDtypeStruct((NSA_HEADS, T, LANES), BF16),
            jax.ShapeDtypeStruct((4, T // CMP_STRIDE, CMP_STRIDE * HEAD_DIM), BF16),
            jax.ShapeDtypeStruct((KV_GROUPS, T, HEAD_DIM), BF16),
            jax.ShapeDtypeStruct((KV_GROUPS, T, LANES), BF16),
            jax.ShapeDtypeStruct((4, T, LANES), BF16),
            jax.ShapeDtypeStruct((T, POOL_WIDTH), F32),
            jax.ShapeDtypeStruct((KV_GROUPS, T, LANES), F32),
        ],
        compiler_params=_params("parallel"),
        name="in_proj",
    )(h, g, w_main, w_gl, _chunk_permutation(tm))


def _chunk_permutation(tm):
    n_chunks = tm // CMP_STRIDE
    perm = np.zeros((tm, tm), np.float32)
    l, n = np.meshgrid(np.arange(CMP_STRIDE), np.arange(n_chunks), indexing="ij")
    perm[(l * n_chunks + n).ravel(), (CMP_STRIDE * n + l).ravel()] = 1.0
    return jnp.asarray(perm, dtype=BF16)


def _cmp_kernel(x_ref, pos_ref, w1_ref, w2_ref, kc_ref, vc_ref):
    n_chunks = x_ref.shape[1]
    for j in range(2):
        out_ref = kc_ref if j == 0 else vc_ref
        for g in range(KV_GROUPS):
            x = x_ref[j * KV_GROUPS + g].astype(F32)
            lo = (x + pos_ref[j, 0:1, :]).astype(BF16)
            hi = (x + pos_ref[j, 1:2, :]).astype(BF16)
            hid = _dot(lo, w1_ref[j, 0]) + pltpu.roll(_dot(hi, w1_ref[j, 1]), n_chunks - 1, axis=0)
            act = jax.nn.gelu(hid).astype(BF16)
            out_ref[0, g] = _dot(act, w2_ref[j]).astype(BF16)


def _compress(x, pos, w1, w2, B, S):
    nck = S // CMP_STRIDE
    half = CMP_STRIDE * HEAD_DIM
    out = jax.ShapeDtypeStruct((B, KV_GROUPS, nck, HEAD_DIM), BF16)
    return pl.pallas_call(
        _cmp_kernel,
        grid=(B,),
        in_specs=[
            pl.BlockSpec((4, nck, half), lambda b: (0, b, 0)),
            _const_spec((2, 2, half)),
            _const_spec((2, 2, half, CMP_HIDDEN)),
            _const_spec((2, CMP_HIDDEN, HEAD_DIM)),
        ],
        out_specs=[pl.BlockSpec((1, KV_GROUPS, nck, HEAD_DIM), lambda b: (b, 0, 0, 0))] * 2,
        out_shape=[out, out],
        compiler_params=_params("parallel"),
        name="compress",
    )(x, pos, w1, w2)


def _sel_kernel(q_ref, kc_ref, vc_ref, map_ref, ocmp_ref, unsel_ref, *, tq, n_blk):
    i = pl.program_id(2)
    kc = kc_ref[0, 0]
    vc = vc_ref[0, 0]
    nck = kc.shape[0]
    n_idx = lax.broadcasted_iota(jnp.int32, (nck, tq), 0)
    t_idx = i * tq + lax.broadcasted_iota(jnp.int32, (nck, tq), 1)
    valid = n_idx * CMP_STRIDE + (CMP_LEN - 1) <= t_idx
    psum = jnp.zeros((nck, tq), F32)
    for r in range(HEADS_PER_GROUP):
        s = jnp.where(valid, _dot_nt(kc, q_ref[r][:, :HEAD_DIM]), NEG_INF)
        e = jnp.exp2(s - jnp.max(s, axis=0, keepdims=True))
        p = jnp.where(valid, e / jnp.sum(e, axis=0, keepdims=True), 0.0)
        ocmp_ref[r] = _dot_tn(p.astype(BF16), vc)
        psum = psum + p
    imp = _dot(map_ref[...], psum, precision=lax.Precision.HIGHEST)
    blk = lax.broadcasted_iota(jnp.int32, (n_blk, tq), 0)
    cur = (i * tq + lax.broadcasted_iota(jnp.int32, (n_blk, tq), 1)) // SEL_BLOCK
    imp = jnp.where(blk > cur, -FORCE, imp)
    imp = jnp.where((blk == 0) | (blk == cur), FORCE, imp)
    rank = jnp.zeros((n_blk, tq), F32)
    for j in range(n_blk):
        row = imp[j:j + 1, :]
        ahead = (row > imp) | ((row == imp) & (blk > j))
        rank = rank + jnp.where(ahead, 1.0, 0.0)
    unsel = jnp.where(rank < SEL_TOPN, 0.0, 1.0)
    table = jnp.concatenate([jnp.zeros((HEAD_DIM, tq), F32), unsel,
                             jnp.zeros((LANES - HEAD_DIM - n_blk, tq), F32)], axis=0).astype(BF16)
    eye = jnp.where(lax.broadcasted_iota(jnp.int32, (LANES, LANES), 0)
                    == lax.broadcasted_iota(jnp.int32, (LANES, LANES), 1), 1.0, 0.0).astype(BF16)
    unsel_ref[0] = _dot_tn(table, eye).astype(BF16)


def _cmp_select(q8, kc, vc, cmp_map, B, S, tq=512):
    T = B * S
    n_blk = S // SEL_BLOCK
    nck = S // CMP_STRIDE
    nq = S // tq
    hp = HEADS_PER_GROUP
    return pl.pallas_call(
        functools.partial(_sel_kernel, tq=tq, n_blk=n_blk),
        grid=(B, KV_GROUPS, nq),
        in_specs=[
            pl.BlockSpec((hp, tq, LANES), lambda b, g, i: (g, b * nq + i, 0)),
            pl.BlockSpec((1, 1, nck, HEAD_DIM), lambda b, g, i: (b, g, 0, 0)),
            pl.BlockSpec((1, 1, nck, HEAD_DIM), lambda b, g, i: (b, g, 0, 0)),
            _const_spec((n_blk, nck)),
        ],
        out_specs=[
            pl.BlockSpec((hp, tq, HEAD_DIM), lambda b, g, i: (g, b * nq + i, 0)),
            pl.BlockSpec((1, tq, LANES), lambda b, g, i: (g, b * nq + i, 0)),
        ],
        out_shape=[
            jax.ShapeDtypeStruct((NSA_HEADS, T, HEAD_DIM), F32),
            jax.ShapeDtypeStruct((KV_GROUPS, T, LANES), BF16),
        ],
        compiler_params=_params("parallel", "parallel", "parallel"),
        name="cmp_select",
    )(q8, kc, vc, cmp_map)


def _attn_kernel(q_ref, unsel_ref, ks_ref, vs_ref, kw_ref, vw_ref, ocmp_ref, gl_ref, diag_ref, wint_ref,
                 o_ref, m_sc, acc_sc, *, tq, tk, S):
    i = pl.program_id(2)
    hp = HEADS_PER_GROUP
    rows = hp * tq
    q4 = (q_ref[...] + unsel_ref[...]).reshape(rows, LANES)

    m_sc[...] = jnp.full(m_sc.shape, NEG_INF, F32)
    acc_sc[...] = jnp.zeros(acc_sc.shape, F32)

    def sel_tile(k0, width, causal=None):
        k = ks_ref[0, pl.ds(k0, width), :]
        v = vs_ref[0, pl.ds(k0, width), :]
        s = _dot_nt(q4, k)
        if causal is not None:
            s = (s.reshape(hp, tq, width) + causal[None]).reshape(rows, width)
        m_old = m_sc[...]
        m_new = jnp.maximum(m_old, jnp.max(s, axis=-1, keepdims=True))
        p = jnp.exp2(s - jnp.concatenate([m_new] * (width // LANES), axis=1))
        acc_sc[...] = jnp.exp2(m_old - m_new) * acc_sc[...] + _dot(p.astype(BF16), v)
        m_sc[...] = m_new

    n_full = (i * tq) // tk

    def pair_step(j, carry):
        sel_tile(pl.multiple_of(j * 2 * tk, 2 * tk), 2 * tk)
        return carry

    lax.fori_loop(0, n_full // 2, pair_step, 0)

    @pl.when(n_full % 2 == 1)
    def _():
        sel_tile(pl.multiple_of((n_full - 1) * tk, tk), tk)

    sel_tile(pl.multiple_of(n_full * tk, tk), tk, diag_ref[0])
    acc = acc_sc[...]
    o_sel = acc[:, :HEAD_DIM] / acc[:, HEAD_DIM:HEAD_DIM + 1]

    span = WINDOW + tq
    w0 = pl.multiple_of(jnp.clip(i * tq - WINDOW, 0, S - span), LANES)
    kw = kw_ref[0, pl.ds(w0, span), :]
    vw = vw_ref[0, pl.ds(w0, span), :]
    s = (_dot_nt(q4[:, :HEAD_DIM], kw).reshape(hp, tq, span) + wint_ref[0][None]).reshape(rows, span)
    p = jnp.exp2(s - jnp.max(s, axis=-1, keepdims=True))
    accw = _dot(p.astype(BF16), vw)
    o_win = accw[:, :HEAD_DIM] / accw[:, HEAD_DIM:HEAD_DIM + 1]

    gl = gl_ref[0]
    outs = []
    for r in range(hp):
        rs = slice(r * tq, (r + 1) * tq)
        outs.append(gl[:, 3 * r:3 * r + 1] * ocmp_ref[r]
                    + gl[:, 3 * r + 1:3 * r + 2] * o_sel[rs]
                    + gl[:, 3 * r + 2:3 * r + 3] * o_win[rs])
    o_ref[...] = jnp.concatenate(outs, axis=-1).astype(BF16)


def _attention_tables(S, tq, tk):
    a = np.arange(tq)[:, None]
    c = np.arange(tk)[None, :]
    diag = np.stack([np.where(c <= p * tq + a, 0.0, -MASK_BIG) for p in range(tk // tq)])
    span = WINDOW + tq
    win = []
    for i in range(WINDOW // tq + 1):
        t = i * tq + a
        kp = min(max(i * tq - WINDOW, 0), S - span) + np.arange(span)[None, :]
        win.append(np.where((kp <= t) & (kp > t - WINDOW), 0.0, -MASK_BIG))
    return jnp.asarray(diag.astype(np.float32)), jnp.asarray(np.stack(win).astype(np.float32))


def _sparse_attention(q8, unsel, ksx, kwin, vx, ocmp, gls, tables, B, S, tq, tk):
    diag, win = tables
    T = B * S
    nq = S // tq
    hp = HEADS_PER_GROUP
    span = WINDOW + tq
    n_win = WINDOW // tq
    q_tile = lambda width: pl.BlockSpec((hp, tq, width), lambda b, g, i: (g, b * nq + i, 0))
    group_tile = pl.BlockSpec((1, tq, LANES), lambda b, g, i: (g, b * nq + i, 0))
    seq = lambda first, width: pl.BlockSpec((1, S, width), lambda b, g, i: (first + g, b, 0))
    return pl.pallas_call(
        functools.partial(_attn_kernel, tq=tq, tk=tk, S=S),
        grid=(B, KV_GROUPS, nq),
        in_specs=[
            q_tile(LANES), group_tile,
            seq(0, LANES), seq(0, LANES), seq(0, HEAD_DIM), seq(2, LANES),
            q_tile(HEAD_DIM), group_tile,
            pl.BlockSpec((1, tq, tk), lambda b, g, i: (i % (tk // tq), 0, 0)),
            pl.BlockSpec((1, tq, span), lambda b, g, i: (jnp.minimum(i, n_win), 0, 0)),
        ],
        out_specs=pl.BlockSpec((tq, hp * HEAD_DIM), lambda b, g, i: (b * nq + i, g)),
        out_shape=jax.ShapeDtypeStruct((T, NSA_WIDTH), BF16),
        scratch_shapes=[
            pltpu.VMEM((hp * tq, LANES), F32),
            pltpu.VMEM((hp * tq, LANES), F32),
        ],
        compiler_params=_params("parallel", "parallel", "parallel"),
        name="sparse_attention",
    )(q8, unsel, ksx, vx, kwin, vx, ocmp, gls, diag, win)


_POOL_HALO = 16


def _pool_mixer(u_ref, halo_ref, w_ref, scale_ref, tm, S):
    pos0 = (pl.program_id(0) * tm) % S
    halo = jnp.where(pos0 == 0, 0.0, halo_ref[...])
    x = jnp.concatenate([halo, u_ref[...]], axis=0)
    pos = pos0 + lax.broadcasted_iota(jnp.int32, (tm, 1), 0)
    out = []
    for gi, w in enumerate(POOL_WINDOWS):
        c0 = gi * POOL_GROUP_WIDTH
        xg = x[:, c0:c0 + POOL_GROUP_WIDTH]
        tot = xg[_POOL_HALO:_POOL_HALO + tm]
        for d in range(1, w):
            tot = tot + xg[_POOL_HALO - d:_POOL_HALO - d + tm]
        cnt = jnp.minimum(pos + 1, w).astype(F32)
        pooled = tot / cnt - xg[_POOL_HALO:_POOL_HALO + tm]
        mixed = _dot(pooled.astype(BF16), w_ref[gi])
        out.append((mixed * scale_ref[:, c0:c0 + POOL_GROUP_WIDTH]).astype(BF16))
    return jnp.concatenate(out, axis=-1)


_N_MERGE_REFS = 11


def _merge_residual(onsa_ref, u_ref, halo_ref, h_ref, gmix_ref, wmg_ref, poolw_ref, pscale_ref, wn_ref, wp_ref,
                    wo_ref, *, tm, S):
    h = h_ref[...]
    a = _rms(h, gmix_ref[...]).astype(BF16)
    nsa_up = _dot(onsa_ref[...], wn_ref[...])
    pool_up = _dot(_pool_mixer(u_ref, halo_ref, poolw_ref, pscale_ref, tm, S), wp_ref[...])
    merged = (jax.nn.sigmoid(_dot(a, wmg_ref[:, :D_MODEL])) * nsa_up
              + jax.nn.sigmoid(_dot(a, wmg_ref[:, D_MODEL:])) * pool_up)
    return h + _dot(merged.astype(BF16), wo_ref[...])


def _post_dense_kernel(*refs, tm, S, fc):
    mix_refs = refs[:_N_MERGE_REFS]
    gffn_ref, wg_ref, wu_ref, wd_ref, o_ref = refs[_N_MERGE_REFS:]
    h1 = _merge_residual(*mix_refs, tm=tm, S=S)
    f = _rms(h1, gffn_ref[...]).astype(BF16)
    o_ref[...] = h1
    for c in range(wg_ref.shape[1] // fc):
        g = _dot(f, wg_ref[:, c * fc:(c + 1) * fc])
        u = _dot(f, wu_ref[:, c * fc:(c + 1) * fc])
        o_ref[...] += _dot((jax.nn.silu(g) * u).astype(BF16), wd_ref[c * fc:(c + 1) * fc, :])


def _post_moe_kernel(*refs, tm, S):
    mix_refs = refs[:_N_MERGE_REFS]
    g_ref, wr_ref, br_ref, h1_ref, f_ref, route_ref, cnt_ref, base_sc = refs[_N_MERGE_REFS:]

    @pl.when(pl.program_id(0) == 0)
    def _():
        base_sc[...] = jnp.zeros(base_sc.shape, F32)

    h1 = _merge_residual(*mix_refs, tm=tm, S=S)
    h1_ref[...] = h1
    f = _rms(h1, g_ref[...])
    width = f_ref.shape[2]
    for p in range(f_ref.shape[0]):
        f_ref[p] = _pack_bf16_pairs(f[:, 2 * p * width:(2 * p + 1) * width],
                                    f[:, (2 * p + 1) * width:(2 * p + 2) * width])

    lane = lax.broadcasted_iota(jnp.int32, (tm, LANES), 1)
    f_hi = f.astype(BF16)
    f_lo = (f - f_hi.astype(F32)).astype(BF16)
    hi_terms = _dot(f_hi, wr_ref[...])
    logits = hi_terms[:, :LANES] + hi_terms[:, LANES:] + _dot(f_lo, wr_ref[:, :LANES]) + br_ref[...]
    lowest = jnp.finfo(F32).min
    lane_f = lane.astype(F32)
    lg = jnp.where(lane < N_EXPERTS, logits, lowest)
    m1 = jnp.max(lg, axis=-1, keepdims=True)
    i1 = jnp.min(jnp.where(lg == m1, lane_f, float(LANES)), axis=-1, keepdims=True)
    lg2 = jnp.where(lane_f == i1, lowest, lg)
    m2 = jnp.max(lg2, axis=-1, keepdims=True)
    i2 = jnp.min(jnp.where(lg2 == m2, lane_f, float(LANES)), axis=-1, keepdims=True)
    e2 = jnp.exp(m2 - m1)
    gate1 = 1.0 / (1.0 + e2)
    gate2 = e2 / (1.0 + e2)

    oh1 = lane_f == i1
    oh2 = lane_f == i2
    both = jnp.where(oh1 | oh2, 1.0, 0.0)
    r_i = lax.broadcasted_iota(jnp.int32, (tm, tm), 0)
    c_i = lax.broadcasted_iota(jnp.int32, (tm, tm), 1)
    before = jnp.where(r_i > c_i, 1.0, 0.0).astype(BF16)
    seen = _dot(before, both.astype(BF16)) + base_sc[...]
    rank1 = jnp.sum(jnp.where(oh1, seen, 0.0), axis=-1, keepdims=True)
    rank2 = jnp.sum(jnp.where(oh2, seen, 0.0), axis=-1, keepdims=True)
    base_sc[...] = base_sc[...] + jnp.sum(both, axis=0, keepdims=True)
    cnt_ref[...] = base_sc[...]

    route = jnp.zeros((tm, LANES), F32)
    for col, val in enumerate((i1, i2, gate1, gate2, rank1, rank2)):
        route = jnp.where(lane == col, val, route)
    route_ref[...] = route


def _post_mixer(onsa, u, h, mix_w, S, ffn=None, router=None, tm=512, fc=256):
    T = h.shape[0]
    g_mix, w_mg, pool_w, pool_scale, wn, wp, wo, g_ffn = mix_w
    tile = lambda w: pl.BlockSpec((tm, w), lambda i: (i, 0))
    halo_blocks = tm // _POOL_HALO
    in_specs = [
        tile(NSA_WIDTH), tile(POOL_WIDTH),
        pl.BlockSpec((_POOL_HALO, POOL_WIDTH), lambda i: (jnp.maximum(i * halo_blocks - 1, 0), 0)),
        tile(D_MODEL), _const_spec((1, D_MODEL)), _const_spec((D_MODEL, 2 * D_MODEL)),
        _const_spec((len(POOL_WINDOWS), POOL_GROUP_WIDTH, POOL_GROUP_WIDTH)), _const_spec((1, POOL_WIDTH)),
        _const_spec((NSA_WIDTH, D_MODEL)), _const_spec((POOL_WIDTH, D_MODEL)),
        _const_spec((D_MODEL, D_MODEL)), _const_spec((1, D_MODEL)),
    ]
    args = [onsa, u, u, h, g_mix, w_mg, pool_w, pool_scale, wn, wp, wo, g_ffn]
    if router is None:
        wg, wu, wd = ffn
        d_ff = wg.shape[1]
        return pl.pallas_call(
            functools.partial(_post_dense_kernel, tm=tm, S=S, fc=fc),
            grid=(T // tm,),
            in_specs=in_specs + [_const_spec((D_MODEL, d_ff)), _const_spec((D_MODEL, d_ff)),
                                 _const_spec((d_ff, D_MODEL))],
            out_specs=tile(D_MODEL),
            out_shape=jax.ShapeDtypeStruct((T, D_MODEL), F32),
            compiler_params=_params("parallel"),
            name="post_mixer_dense",
        )(*args, wg, wu, wd)
    w_router, b_router = router
    return pl.pallas_call(
        functools.partial(_post_moe_kernel, tm=tm, S=S),
        grid=(T // tm,),
        in_specs=in_specs + [_const_spec((D_MODEL, 2 * LANES)), _const_spec((1, LANES))],
        out_specs=[tile(D_MODEL), pl.BlockSpec((N_PLANES, tm, PLANE_WIDTH), lambda i: (0, i, 0)), tile(LANES),
                   pl.BlockSpec((1, LANES), lambda i: (0, 0))],
        out_shape=[
            jax.ShapeDtypeStruct((T, D_MODEL), F32),
            jax.ShapeDtypeStruct((N_PLANES, T, PLANE_WIDTH), jnp.uint32),
            jax.ShapeDtypeStruct((T, LANES), F32),
            jax.ShapeDtypeStruct((1, LANES), F32),
        ],
        scratch_shapes=[pltpu.VMEM((1, LANES), F32)],
        compiler_params=_params("arbitrary"),
        name="post_mixer_moe",
    )(*args, w_router, b_router)


PLANE_WIDTH = 256
N_PLANES = D_MODEL // (2 * PLANE_WIDTH)
SC_WINDOW = 128


def _sc_mesh():
    return plsc.VectorSubcoreMesh(core_axis_name="core", subcore_axis_name="subcore")


def _pack_bf16_pairs(lo, hi):
    lo_bits = lax.bitcast_convert_type(lo.astype(BF16).astype(F32), jnp.uint32)
    hi_bits = lax.bitcast_convert_type(hi.astype(BF16).astype(F32), jnp.uint32)
    return (lo_bits >> 16) | (hi_bits & jnp.uint32(0xFFFF0000))


def _unpack_bf16_pairs(words):
    lo = lax.bitcast_convert_type(words << 16, F32)
    hi = lax.bitcast_convert_type(words & jnp.uint32(0xFFFF0000), F32)
    return lo, hi


def _sc_scatter_pair(rows, idx_a, idx_b, n_out):
    n_rows, width = rows.shape

    @pl.kernel(out_type=jax.ShapeDtypeStruct((n_out, width), rows.dtype), mesh=_sc_mesh(), scratch_types=[])
    def scatter(rows_hbm, a_hbm, b_hbm, out_hbm):
        def body(rows_vmem, a_vmem, b_vmem):
            pltpu.sync_copy(rows_vmem, out_hbm.at[a_vmem.at[0]])
            pltpu.sync_copy(rows_vmem, out_hbm.at[b_vmem.at[0]])

        idx_spec = pl.BlockSpec((1, SC_WINDOW), lambda i: (0, i))
        pltpu.emit_pipeline(
            body,
            grid=(n_rows // SC_WINDOW,),
            in_specs=[pl.BlockSpec((SC_WINDOW, width), lambda i: (i, 0)), idx_spec, idx_spec],
            out_specs=[],
            core_axis_name=("core", "subcore"),
            dimension_semantics=(pltpu.PARALLEL,),
        )(rows_hbm, a_hbm, b_hbm)

    return scatter(rows, idx_a, idx_b)


def _sc_gather(rows, idx):
    n_out = idx.shape[1]
    width = rows.shape[1]

    @pl.kernel(out_type=jax.ShapeDtypeStruct((n_out, width), rows.dtype), mesh=_sc_mesh(), scratch_types=[])
    def gather(rows_hbm, idx_hbm, out_hbm):
        def body(idx_vmem, out_vmem):
            pltpu.sync_copy(rows_hbm.at[idx_vmem.at[0]], out_vmem)

        pltpu.emit_pipeline(
            body,
            grid=(n_out // SC_WINDOW,),
            in_specs=[pl.BlockSpec((1, SC_WINDOW), lambda i: (0, i))],
            out_specs=[pl.BlockSpec((SC_WINDOW, width), lambda i: (i, 0))],
            core_axis_name=("core", "subcore"),
            dimension_semantics=(pltpu.PARALLEL,),
        )(idx_hbm, out_hbm)

    return gather(rows, idx)


def _dispatch(f_planes, dest):
    n_planes, T, width = f_planes.shape
    n_rows = 2 * T
    plane_base = (jnp.arange(n_planes, dtype=jnp.int32) * n_rows)[:, None]
    idx = [(plane_base + dest[:, k][None, :]).reshape(1, n_planes * T) for k in range(2)]
    out = _sc_scatter_pair(f_planes.reshape(n_planes * T, width), idx[0], idx[1], n_planes * n_rows)
    return out.reshape(n_planes, n_rows, width)


def _undispatch(y_planes, dest):
    n_planes, n_rows, width = y_planes.shape
    T = dest.shape[0]
    plane_base = (jnp.arange(n_planes, dtype=jnp.int32) * n_rows)[None, :, None]
    idx = (plane_base + dest.T[:, None, :]).reshape(1, 2 * n_planes * T)
    out = _sc_gather(y_planes.reshape(n_planes * n_rows, width), idx)
    return out.reshape(2, n_planes, T, width)


def _combine_kernel(h_ref, route_ref, z_ref, g_ref, o_ref, *, final_norm):
    route = route_ref[...]
    z = [jnp.concatenate([half for p in range(z_ref.shape[1]) for half in _unpack_bf16_pairs(z_ref[k, p])], axis=-1)
         for k in range(2)]
    out = h_ref[...] + (route[:, 2:3] * z[0] + route[:, 3:4] * z[1])
    if final_norm:
        out = _rms(out, g_ref[...])
    o_ref[...] = out


def _combine(h1, route, z, g_final, final_norm, tg=512):
    T = h1.shape[0]
    return pl.pallas_call(
        functools.partial(_combine_kernel, final_norm=final_norm),
        grid=(T // tg,),
        in_specs=[
            pl.BlockSpec((tg, D_MODEL), lambda i: (i, 0)),
            pl.BlockSpec((tg, LANES), lambda i: (i, 0)),
            pl.BlockSpec((2, z.shape[1], tg, z.shape[3]), lambda i: (0, 0, i, 0)),
            _const_spec((1, D_MODEL)),
        ],
        out_specs=pl.BlockSpec((tg, D_MODEL), lambda i: (i, 0)),
        out_shape=jax.ShapeDtypeStruct((T, D_MODEL), F32),
        compiler_params=_params("parallel"),
        name="moe_combine",
    )(h1, route, z, g_final)


def _moe_kernel(blk_ref, exp_ref, lo_ref, hi_ref, first_ref, x_ref, wg_ref, wu_ref, wd_ref, y_ref, acc_sc, *, bm):
    v = pl.program_id(0)
    c = pl.program_id(1)

    @pl.when((first_ref[v] == 1) & (c == 0))
    def _():
        acc_sc[...] = jnp.zeros(acc_sc.shape, F32)

    lo = lo_ref[v]
    hi = hi_ref[v]

    @pl.when(hi > lo)
    def _():
        row = blk_ref[v] * bm + lax.broadcasted_iota(jnp.int32, (bm, 1), 0)
        x = jnp.concatenate([half for p in range(x_ref.shape[0]) for half in _unpack_bf16_pairs(x_ref[p])], axis=-1)
        x = jnp.where((row >= lo) & (row < hi), x, 0.0).astype(BF16)
        g = _dot(x, wg_ref[0].astype(BF16))
        u = _dot(x, wu_ref[0].astype(BF16))
        acc_sc[...] += _dot((jax.nn.silu(g) * u).astype(BF16), wd_ref[0].astype(BF16))

    @pl.when(c == pl.num_programs(1) - 1)
    def _():
        width = y_ref.shape[2]
        for p in range(y_ref.shape[0]):
            y_ref[p] = _pack_bf16_pairs(acc_sc[:, 2 * p * width:(2 * p + 1) * width],
                                        acc_sc[:, (2 * p + 1) * width:(2 * p + 2) * width])


def _grouped_ffn(x_sorted, visits, wg, wu, wd, layer, bm=1024, fc=512):
    n_planes, n_rows, width = x_sorted.shape
    n_visits = visits[0].shape[0]
    n_exp, _, d_ff = wg.shape[1:]
    first = layer * n_exp
    flat = lambda w: w.reshape((w.shape[0] * n_exp,) + w.shape[2:])
    rows_spec = pl.BlockSpec((n_planes, bm, width), lambda v, c, blk, ex, lo, hi, fi: (0, blk[v], 0))
    grid_spec = pltpu.PrefetchScalarGridSpec(
        num_scalar_prefetch=5,
        grid=(n_visits, d_ff // fc),
        in_specs=[
            rows_spec,
            pl.BlockSpec((1, D_MODEL, fc), lambda v, c, blk, ex, lo, hi, fi: (first + ex[v], 0, c)),
            pl.BlockSpec((1, D_MODEL, fc), lambda v, c, blk, ex, lo, hi, fi: (first + ex[v], 0, c)),
            pl.BlockSpec((1, fc, D_MODEL), lambda v, c, blk, ex, lo, hi, fi: (first + ex[v], c, 0)),
        ],
        out_specs=rows_spec,
        scratch_shapes=[pltpu.VMEM((bm, D_MODEL), F32)],
    )
    return pl.pallas_call(
        functools.partial(_moe_kernel, bm=bm),
        grid_spec=grid_spec,
        out_shape=jax.ShapeDtypeStruct((n_planes, n_rows, width), jnp.uint32),
        compiler_params=_params("arbitrary", "arbitrary"),
        name="grouped_ffn",
    )(*visits, x_sorted, flat(wg), flat(wu), flat(wd))


def _routing_tables(route, counts, n_rows, bm):
    counts = counts[0, :N_EXPERTS].astype(jnp.int32)
    ends = jnp.cumsum(counts)
    starts = ends - counts
    e = route[:, 0:2].astype(jnp.int32)
    seg_start = jnp.sum(jnp.where(e[..., None] == jnp.arange(N_EXPERTS), starts, 0), axis=-1)
    dest = seg_start + route[:, 4:6].astype(jnp.int32)
    n_blocks = n_rows // bm
    cuts = jnp.sort(jnp.concatenate([jnp.arange(n_blocks, dtype=jnp.int32) * bm, starts[1:]]))
    lo = cuts
    hi = jnp.concatenate([cuts[1:], jnp.array([n_rows], jnp.int32)])
    blk = jnp.minimum(lo // bm, n_blocks - 1)
    ex = jnp.minimum(jnp.sum(ends[None, :] <= lo[:, None], axis=1), N_EXPERTS - 1).astype(jnp.int32)
    first = jnp.concatenate([jnp.ones((1,), jnp.int32), (blk[1:] != blk[:-1]).astype(jnp.int32)])
    return dest, (blk, ex, lo, hi, first)


def _layer_weights(w_in, cmp_pos, cmp_w1, cmp_w2):
    w_main = jnp.concatenate([w_in[:, _Q0:_GL0], w_in[:, _U0:_MG0]], axis=1).astype(BF16)
    w_mg = w_in[:, _MG0:].astype(BF16)
    gl = w_in[:, _GL0:_U0].reshape(D_MODEL, KV_GROUPS, _GATES_PER_GROUP)
    w_gl = jnp.pad(gl, ((0, 0), (0, 0), (0, LANES - _GATES_PER_GROUP))).reshape(D_MODEL, KV_GROUPS * LANES)
    half = CMP_STRIDE * HEAD_DIM
    pos = cmp_pos.reshape(2, 2, half)
    w1 = cmp_w1.reshape(2, 2, half, CMP_HIDDEN).astype(BF16)
    return w_main, w_mg, w_gl.astype(BF16), pos, w1, cmp_w2.astype(BF16)


def _cmp_to_sel_map(S):
    n_blk = S // SEL_BLOCK
    n_cmp = (S - CMP_LEN) // CMP_STRIDE + 1
    nck = S // CMP_STRIDE
    cs = np.arange(nck) * CMP_STRIDE
    bs = np.arange(n_blk) * SEL_BLOCK
    overlap = (cs[None, :] <= bs[:, None] + SEL_BLOCK - 1) & (cs[None, :] + CMP_LEN - 1 >= bs[:, None])
    overlap &= (np.arange(nck) < n_cmp)[None, :]
    return jnp.asarray(overlap.astype(np.float32))


def kernel(x, norm_mix_g, w_in, cmp_pos, cmp_w1, cmp_w2, w_nsa_up, pool_w, pool_scale, w_pool_up, w_out,
           norm_ffn_g, ffn_w_gate, ffn_w_up, ffn_w_down, moe_w_router, moe_b_router, moe_w_gate, moe_w_up,
           moe_w_down, final_norm_g):
    B, S, D = x.shape
    T = B * S
    depth = w_in.shape[0]
    cmp_map = _cmp_to_sel_map(S)
    attn_tq, attn_tk = 256, 512
    attn_tables = _attention_tables(S, attn_tq, attn_tk)
    h = x.reshape(T, D)
    bm = 1024
    for i in range(depth):
        w_main, w_mg, w_gl, pos, w1, w2 = _layer_weights(w_in[i], cmp_pos[i], cmp_w1[i], cmp_w2[i])
        q8, xc, kwin, ksx, vx, u, gls = _in_proj(h, norm_mix_g[i][None], w_main, w_gl, S)
        kc, vc = _compress(xc, pos, w1, w2, B, S)
        ocmp, unsel = _cmp_select(q8, kc, vc, cmp_map, B, S)
        onsa = _sparse_attention(q8, unsel, ksx, kwin, vx, ocmp, gls, attn_tables, B, S, attn_tq, attn_tk)
        mix_w = (norm_mix_g[i][None], w_mg, pool_w[i].astype(BF16), pool_scale[i][None], w_nsa_up[i].astype(BF16),
                 w_pool_up[i].astype(BF16), w_out[i].astype(BF16), norm_ffn_g[i][None])
        j = i // 2
        if i % 2 == 0:
            ffn = (ffn_w_gate[j].astype(BF16), ffn_w_up[j].astype(BF16), ffn_w_down[j].astype(BF16))
            h = _post_mixer(onsa, u, h, mix_w, S, ffn=ffn)
        else:
            w_router = jnp.pad(moe_w_router[j], ((0, 0), (0, LANES - N_EXPERTS)))
            w_router_hi = w_router.astype(BF16)
            w_router_lo = (w_router - w_router_hi.astype(F32)).astype(BF16)
            w_router = jnp.concatenate([w_router_hi, w_router_lo], axis=1)
            b_router = jnp.pad(moe_b_router[j], (0, LANES - N_EXPERTS))[None]
            h1, f, route, counts = _post_mixer(onsa, u, h, mix_w, S, router=(w_router, b_router))
            dest, visits = _routing_tables(route, counts, 2 * T, bm)
            x_sorted = _dispatch(f, dest)
            y_sorted = _grouped_ffn(x_sorted, visits, moe_w_gate, moe_w_up, moe_w_down, j, bm=bm)
            z = _undispatch(y_sorted, dest)
            h = _combine(h1, route, z, final_norm_g[None], final_norm=(i == depth - 1))
    return h.reshape(B, S, D)
```

```python
import functools

import numpy as np
import jax
import jax.numpy as jnp
from jax import lax
from jax.experimental import pallas as pl
from jax.experimental.pallas import tpu as pltpu
from jax.experimental.pallas import tpu_sc as plsc

D_MODEL = 1024
NSA_HEADS = 8
HEAD_DIM = 64
KV_GROUPS = 2
HEADS_PER_GROUP = NSA_HEADS // KV_GROUPS
NSA_WIDTH = NSA_HEADS * HEAD_DIM
KV_WIDTH = KV_GROUPS * HEAD_DIM
CMP_LEN = 32
CMP_STRIDE = 16
CMP_HIDDEN = 256
SEL_BLOCK = 64
SEL_TOPN = 8
WINDOW = 512
ATTN_SCALE = HEAD_DIM ** -0.5
POOL_WINDOWS = (2, 4, 8, 16)
POOL_GROUP_WIDTH = 128
POOL_WIDTH = 512
N_EXPERTS = 8
D_FF_EXPERT = 3584
RMS_EPS = 1e-6
NEG_INF = -1e30
FORCE = 1e9

LOG2E = 1.4426950408889634
MASK_BIG = 2.0 ** 100

LANES = 128
VMEM_LIMIT = 56 * 1024 * 1024

BF16 = jnp.bfloat16
F32 = jnp.float32

_Q0 = 0
_KV0 = NSA_WIDTH
_GL0 = _KV0 + 6 * KV_WIDTH
_U0 = _GL0 + 3 * NSA_HEADS
_MG0 = _U0 + POOL_WIDTH
_MAIN_COLS = NSA_WIDTH + 6 * KV_WIDTH + POOL_WIDTH
_GATES_PER_GROUP = 3 * HEADS_PER_GROUP


def _dot(a, b, **kw):
    return jnp.dot(a, b, preferred_element_type=F32, **kw)


def _dot_tn(a, b):
    return lax.dot_general(a, b, (((0,), (0,)), ((), ())), preferred_element_type=F32)


def _dot_nt(a, b):
    return lax.dot_general(a, b, (((1,), (1,)), ((), ())), preferred_element_type=F32)


def _rms(x, g):
    return x * lax.rsqrt(jnp.mean(x * x, axis=-1, keepdims=True) + RMS_EPS) * g


def _params(*sem):
    return pltpu.CompilerParams(dimension_semantics=sem, vmem_limit_bytes=VMEM_LIMIT)


def _const_spec(shape):
    nd = len(shape)
    return pl.BlockSpec(shape, lambda *_: (0,) * nd, pipeline_mode=pl.Buffered(1))


def _in_kernel(h_ref, g_ref, w_ref, wgl_ref, chunk_ref, q_ref, xc_ref, kw_ref, ksx_ref, vx_ref, u_ref, gl_ref, *, S):
    a = _rms(h_ref[...], g_ref[...]).astype(BF16)
    tm = a.shape[0]
    zeros = jnp.zeros((tm, HEAD_DIM), F32)
    ones = jnp.ones((tm, HEAD_DIM), F32)
    pos = (pl.program_id(0) * tm + lax.broadcasted_iota(jnp.int32, (tm, HEAD_DIM), 0)) % S
    lane = lax.broadcasted_iota(jnp.int32, (tm, HEAD_DIM), 1)
    block_code = jnp.where(pos // SEL_BLOCK == lane, -MASK_BIG, 0.0)

    def wide(lo, hi):
        return jnp.concatenate([lo, hi], axis=-1).astype(BF16)

    q = _dot(a, w_ref[:, 0:NSA_WIDTH]) * (ATTN_SCALE * LOG2E)
    for h in range(NSA_HEADS):
        q_ref[h] = wide(q[:, h * HEAD_DIM:(h + 1) * HEAD_DIM], zeros)
    for c in range(3):
        c0 = NSA_WIDTH + c * 256
        kv = _dot(a, w_ref[:, c0:c0 + 256])
        pieces = [kv[:, i * HEAD_DIM:(i + 1) * HEAD_DIM] for i in range(4)]
        if c == 0:
            n_chunks = tm // CMP_STRIDE
            for i in range(4):
                by_offset = _dot(chunk_ref[...], pieces[i].astype(BF16))
                for l in range(CMP_STRIDE):
                    xc_ref[i, :, l * HEAD_DIM:(l + 1) * HEAD_DIM] = (
                        by_offset[l * n_chunks:(l + 1) * n_chunks].astype(BF16))
            continue
        for g in range(KV_GROUPS):
            if c == 1:
                ksx_ref[g] = wide(pieces[g], block_code)
            else:
                kw_ref[g] = pieces[g].astype(BF16)
            vx_ref[2 * (c - 1) + g] = wide(pieces[2 + g], ones)
    u0 = NSA_WIDTH + 6 * KV_WIDTH
    u_ref[...] = _dot(a, w_ref[:, u0:u0 + POOL_WIDTH])
    gl = jax.nn.sigmoid(_dot(a, wgl_ref[...]))
    for g in range(KV_GROUPS):
        gl_ref[g] = gl[:, g * LANES:(g + 1) * LANES]


def _in_proj(h, g, w_main, w_gl, S, tm=512):
    T = h.shape[0]
    return pl.pallas_call(
        functools.partial(_in_kernel, S=S),
        grid=(T // tm,),
        in_specs=[
            pl.BlockSpec((tm, D_MODEL), lambda i: (i, 0)),
            _const_spec((1, D_MODEL)),
            _const_spec((D_MODEL, _MAIN_COLS)),
            _const_spec((D_MODEL, KV_GROUPS * LANES)),
            _const_spec((tm, tm)),
        ],
        out_specs=[
            pl.BlockSpec((NSA_HEADS, tm, LANES), lambda i: (0, i, 0)),
            pl.BlockSpec((4, tm // CMP_STRIDE, CMP_STRIDE * HEAD_DIM), lambda i: (0, i, 0)),
            pl.BlockSpec((KV_GROUPS, tm, HEAD_DIM), lambda i: (0, i, 0)),
            pl.BlockSpec((KV_GROUPS, tm, LANES), lambda i: (0, i, 0)),
            pl.BlockSpec((4, tm, LANES), lambda i: (0, i, 0)),
            pl.BlockSpec((tm, POOL_WIDTH), lambda i: (i, 0)),
            pl.BlockSpec((KV_GROUPS, tm, LANES), lambda i: (0, i, 0)),
        ],
        out_shape=[
            jax.ShapeDtypeStruct((NSA_HEADS, T, LANES), BF16),
            jax.ShapeDtypeStruct((4, T // CMP_STRIDE, CMP_STRIDE * HEAD_DIM), BF16),
            jax.ShapeDtypeStruct((KV_GROUPS, T, HEAD_DIM), BF16),
            jax.ShapeDtypeStruct((KV_GROUPS, T, LANES), BF16),
            jax.ShapeDtypeStruct((4, T, LANES), BF16),
            jax.ShapeDtypeStruct((T, POOL_WIDTH), F32),
            jax.ShapeDtypeStruct((KV_GROUPS, T, LANES), F32),
        ],
        compiler_params=_params("parallel"),
        name="in_proj",
    )(h, g, w_main, w_gl, _chunk_permutation(tm))


def _chunk_permutation(tm):
    n_chunks = tm // CMP_STRIDE
    perm = np.zeros((tm, tm), np.float32)
    l, n = np.meshgrid(np.arange(CMP_STRIDE), np.arange(n_chunks), indexing="ij")
    perm[(l * n_chunks + n).ravel(), (CMP_STRIDE * n + l).ravel()] = 1.0
    return jnp.asarray(perm, dtype=BF16)


def _cmp_kernel(x_ref, pos_ref, w1_ref, w2_ref, kc_ref, vc_ref):
    n_chunks = x_ref.shape[1]
    for j in range(2):
        out_ref = kc_ref if j == 0 else vc_ref
        for g in range(KV_GROUPS):
            x = x_ref[j * KV_GROUPS + g].astype(F32)
            lo = (x + pos_ref[j, 0:1, :]).astype(BF16)
            hi = (x + pos_ref[j, 1:2, :]).astype(BF16)
            hid = _dot(lo, w1_ref[j, 0]) + pltpu.roll(_dot(hi, w1_ref[j, 1]), n_chunks - 1, axis=0)
            act = jax.nn.gelu(hid).astype(BF16)
            out_ref[0, g] = _dot(act, w2_ref[j]).astype(BF16)


def _compress(x, pos, w1, w2, B, S):
    nck = S // CMP_STRIDE
    half = CMP_STRIDE * HEAD_DIM
    out = jax.ShapeDtypeStruct((B, KV_GROUPS, nck, HEAD_DIM), BF16)
    return pl.pallas_call(
        _cmp_kernel,
        grid=(B,),
        in_specs=[
            pl.BlockSpec((4, nck, half), lambda b: (0, b, 0)),
            _const_spec((2, 2, half)),
            _const_spec((2, 2, half, CMP_HIDDEN)),
            _const_spec((2, CMP_HIDDEN, HEAD_DIM)),
        ],
        out_specs=[pl.BlockSpec((1, KV_GROUPS, nck, HEAD_DIM), lambda b: (b, 0, 0, 0))] * 2,
        out_shape=[out, out],
        compiler_params=_params("parallel"),
        name="compress",
    )(x, pos, w1, w2)


def _sel_kernel(q_ref, kc_ref, vc_ref, map_ref, ocmp_ref, unsel_ref, *, tq, n_blk):
    i = pl.program_id(2)
    kc = kc_ref[0, 0]
    vc = vc_ref[0, 0]
    nck = kc.shape[0]
    n_idx = lax.broadcasted_iota(jnp.int32, (nck, tq), 0)
    t_idx = i * tq + lax.broadcasted_iota(jnp.int32, (nck, tq), 1)
    valid = n_idx * CMP_STRIDE + (CMP_LEN - 1) <= t_idx
    sees_any = t_idx[0:1, :] >= CMP_LEN - 1
    psum = jnp.zeros((nck, tq), F32)
    for r in range(HEADS_PER_GROUP):
        s = jnp.where(valid, _dot_nt(kc, q_ref[r][:, :HEAD_DIM]), NEG_INF)
        e = jnp.exp2(s - jnp.max(s, axis=0, keepdims=True))
        p = e * jnp.where(sees_any, 1.0 / jnp.sum(e, axis=0, keepdims=True), 0.0)
        ocmp_ref[r] = _dot_tn(p.astype(BF16), vc)
        psum = psum + p
    imp = _dot(map_ref[...], psum, precision=lax.Precision.HIGHEST)
    blk = lax.broadcasted_iota(jnp.int32, (n_blk, tq), 0)
    t_blk = i * tq + lax.broadcasted_iota(jnp.int32, (n_blk, tq), 1)
    cur = lax.shift_right_logical(t_blk, SEL_BLOCK.bit_length() - 1)
    imp = jnp.where(blk > cur, -FORCE, imp)
    imp = jnp.where((blk == 0) | (blk == cur), FORCE, imp)
    rank = jnp.zeros((n_blk, tq), F32)
    for j in range(n_blk):
        row = imp[j:j + 1, :]
        ahead = (row > imp) | ((row == imp) & (blk > j))
        rank = rank + jnp.where(ahead, 1.0, 0.0)
    unsel = jnp.where(rank < SEL_TOPN, 0.0, 1.0)
    table = jnp.concatenate([jnp.zeros((HEAD_DIM, tq), F32), unsel,
                             jnp.zeros((LANES - HEAD_DIM - n_blk, tq), F32)], axis=0).astype(BF16)
    eye = jnp.where(lax.broadcasted_iota(jnp.int32, (LANES, LANES), 0)
                    == lax.broadcasted_iota(jnp.int32, (LANES, LANES), 1), 1.0, 0.0).astype(BF16)
    unsel_ref[0] = _dot_tn(table, eye).astype(BF16)


def _cmp_select(q8, kc, vc, cmp_map, B, S, tq=512):
    T = B * S
    n_blk = S // SEL_BLOCK
    nck = S // CMP_STRIDE
    nq = S // tq
    hp = HEADS_PER_GROUP
    return pl.pallas_call(
        functools.partial(_sel_kernel, tq=tq, n_blk=n_blk),
        grid=(B, KV_GROUPS, nq),
        in_specs=[
            pl.BlockSpec((hp, tq, LANES), lambda b, g, i: (g, b * nq + i, 0)),
            pl.BlockSpec((1, 1, nck, HEAD_DIM), lambda b, g, i: (b, g, 0, 0)),
            pl.BlockSpec((1, 1, nck, HEAD_DIM), lambda b, g, i: (b, g, 0, 0)),
            _const_spec((n_blk, nck)),
        ],
        out_specs=[
            pl.BlockSpec((hp, tq, HEAD_DIM), lambda b, g, i: (g, b * nq + i, 0)),
            pl.BlockSpec((1, tq, LANES), lambda b, g, i: (g, b * nq + i, 0)),
        ],
        out_shape=[
            jax.ShapeDtypeStruct((NSA_HEADS, T, HEAD_DIM), F32),
            jax.ShapeDtypeStruct((KV_GROUPS, T, LANES), BF16),
        ],
        compiler_params=_params("parallel", "parallel", "parallel"),
        name="cmp_select",
    )(q8, kc, vc, cmp_map)


def _attn_kernel(q_ref, unsel_ref, ks_ref, vs_ref, kw_ref, vw_ref, ocmp_ref, gl_ref, diag_ref, wint_ref,
                 o_ref, m_sc, acc_sc, *, tq, tk, S):
    i = pl.program_id(2)
    hp = HEADS_PER_GROUP
    rows = hp * tq
    span = WINDOW + tq

    def sel_tile(q4, k0, width, causal=None):
        k = ks_ref[0, k0:k0 + width, :]
        v = vs_ref[0, k0:k0 + width, :]
        s = _dot_nt(q4, k)
        if causal is not None:
            s = (s.reshape(hp, tq, width) + causal[None]).reshape(rows, width)
        m_old = m_sc[...]
        m_new = jnp.maximum(m_old, jnp.max(s, axis=-1, keepdims=True))
        p = jnp.exp2(s - jnp.concatenate([m_new] * (width // LANES), axis=1))
        acc_sc[...] = jnp.exp2(m_old - m_new) * acc_sc[...] + _dot(p.astype(BF16), v)
        m_sc[...] = m_new

    def step(n_full):
        q4 = (q_ref[...] + unsel_ref[...]).reshape(rows, LANES)
        m_sc[...] = jnp.full(m_sc.shape, NEG_INF, F32)
        acc_sc[...] = jnp.zeros(acc_sc.shape, F32)
        for j in range(n_full // 2):
            sel_tile(q4, j * 2 * tk, 2 * tk)
        if n_full % 2:
            sel_tile(q4, (n_full - 1) * tk, tk)
        sel_tile(q4, n_full * tk, tk, diag_ref[0])
        acc = acc_sc[...]
        o_sel = acc[:, :HEAD_DIM] / acc[:, HEAD_DIM:HEAD_DIM + 1]

        w0 = pl.multiple_of(jnp.clip(i * tq - WINDOW, 0, S - span), LANES)
        kw = kw_ref[0, pl.ds(w0, span), :]
        vw = vw_ref[0, pl.ds(w0, span), :]
        s = (_dot_nt(q4[:, :HEAD_DIM], kw).reshape(hp, tq, span) + wint_ref[0][None]).reshape(rows, span)
        p = jnp.exp2(s - jnp.max(s, axis=-1, keepdims=True))
        accw = _dot(p.astype(BF16), vw)
        o_win = accw[:, :HEAD_DIM] / accw[:, HEAD_DIM:HEAD_DIM + 1]

        gl = gl_ref[0]
        outs = []
        for r in range(hp):
            rs = slice(r * tq, (r + 1) * tq)
            outs.append(gl[:, 3 * r:3 * r + 1] * ocmp_ref[r]
                        + gl[:, 3 * r + 1:3 * r + 2] * o_sel[rs]
                        + gl[:, 3 * r + 2:3 * r + 3] * o_win[rs])
        o_ref[...] = jnp.concatenate(outs, axis=-1).astype(BF16)

    for n_full in range(((S - tq) // tk) + 1):
        pl.when((i * tq) // tk == n_full)(functools.partial(step, n_full))


def _attention_tables(S, tq, tk):
    a = np.arange(tq)[:, None]
    c = np.arange(tk)[None, :]
    diag = np.stack([np.where(c <= p * tq + a, 0.0, -MASK_BIG) for p in range(tk // tq)])
    span = WINDOW + tq
    win = []
    for i in range(WINDOW // tq + 1):
        t = i * tq + a
        kp = min(max(i * tq - WINDOW, 0), S - span) + np.arange(span)[None, :]
        win.append(np.where((kp <= t) & (kp > t - WINDOW), 0.0, -MASK_BIG))
    return jnp.asarray(diag.astype(np.float32)), jnp.asarray(np.stack(win).astype(np.float32))


def _sparse_attention(q8, unsel, ksx, kwin, vx, ocmp, gls, tables, B, S, tq, tk):
    diag, win = tables
    T = B * S
    nq = S // tq
    hp = HEADS_PER_GROUP
    span = WINDOW + tq
    n_win = WINDOW // tq
    q_tile = lambda width: pl.BlockSpec((hp, tq, width), lambda b, g, i: (g, b * nq + i, 0))
    group_tile = pl.BlockSpec((1, tq, LANES), lambda b, g, i: (g, b * nq + i, 0))
    seq = lambda first, width: pl.BlockSpec((1, S, width), lambda b, g, i: (first + g, b, 0))
    return pl.pallas_call(
        functools.partial(_attn_kernel, tq=tq, tk=tk, S=S),
        grid=(B, KV_GROUPS, nq),
        in_specs=[
            q_tile(LANES), group_tile,
            seq(0, LANES), seq(0, LANES), seq(0, HEAD_DIM), seq(2, LANES),
            q_tile(HEAD_DIM), group_tile,
            pl.BlockSpec((1, tq, tk), lambda b, g, i: (i % (tk // tq), 0, 0)),
            pl.BlockSpec((1, tq, span), lambda b, g, i: (jnp.minimum(i, n_win), 0, 0)),
        ],
        out_specs=pl.BlockSpec((tq, hp * HEAD_DIM), lambda b, g, i: (b * nq + i, g)),
        out_shape=jax.ShapeDtypeStruct((T, NSA_WIDTH), BF16),
        scratch_shapes=[
            pltpu.VMEM((hp * tq, LANES), F32),
            pltpu.VMEM((hp * tq, LANES), F32),
        ],
        compiler_params=_params("parallel", "parallel", "parallel"),
        name="sparse_attention",
    )(q8, unsel, ksx, vx, kwin, vx, ocmp, gls, diag, win)


_POOL_HALO = 16


def _pool_mixer(u_ref, halo_ref, w_ref, scale_ref, tm, S):
    pos0 = (pl.program_id(0) * tm) % S
    halo = jnp.where(pos0 == 0, 0.0, halo_ref[...])
    x = jnp.concatenate([halo, u_ref[...]], axis=0)
    pos = pos0 + lax.broadcasted_iota(jnp.int32, (tm, 1), 0)
    out = []
    for gi, w in enumerate(POOL_WINDOWS):
        c0 = gi * POOL_GROUP_WIDTH
        xg = x[:, c0:c0 + POOL_GROUP_WIDTH]
        tot = xg[_POOL_HALO:_POOL_HALO + tm]
        for d in range(1, w):
            tot = tot + xg[_POOL_HALO - d:_POOL_HALO - d + tm]
        cnt = jnp.minimum(pos + 1, w).astype(F32)
        pooled = tot / cnt - xg[_POOL_HALO:_POOL_HALO + tm]
        mixed = _dot(pooled.astype(BF16), w_ref[gi])
        out.append((mixed * scale_ref[:, c0:c0 + POOL_GROUP_WIDTH]).astype(BF16))
    return jnp.concatenate(out, axis=-1)


_N_MERGE_REFS = 11


def _merge_residual(onsa_ref, u_ref, halo_ref, h_ref, gmix_ref, wmg_ref, poolw_ref, pscale_ref, wn_ref, wp_ref,
                    wo_ref, *, tm, S):
    h = h_ref[...]
    a = _rms(h, gmix_ref[...]).astype(BF16)
    nsa_up = _dot(onsa_ref[...], wn_ref[...])
    pool_up = _dot(_pool_mixer(u_ref, halo_ref, poolw_ref, pscale_ref, tm, S), wp_ref[...])
    merged = (jax.nn.sigmoid(_dot(a, wmg_ref[:, :D_MODEL])) * nsa_up
              + jax.nn.sigmoid(_dot(a, wmg_ref[:, D_MODEL:])) * pool_up)
    return h + _dot(merged.astype(BF16), wo_ref[...])


def _post_dense_kernel(*refs, tm, S, fc):
    mix_refs = refs[:_N_MERGE_REFS]
    gffn_ref, wg_ref, wu_ref, wd_ref, o_ref = refs[_N_MERGE_REFS:]
    h1 = _merge_residual(*mix_refs, tm=tm, S=S)
    f = _rms(h1, gffn_ref[...]).astype(BF16)
    o_ref[...] = h1
    for c in range(wg_ref.shape[1] // fc):
        g = _dot(f, wg_ref[:, c * fc:(c + 1) * fc])
        u = _dot(f, wu_ref[:, c * fc:(c + 1) * fc])
        o_ref[...] += _dot((jax.nn.silu(g) * u).astype(BF16), wd_ref[c * fc:(c + 1) * fc, :])


def _post_moe_kernel(*refs, tm, S):
    mix_refs = refs[:_N_MERGE_REFS]
    g_ref, wr_ref, br_ref, h1_ref, f_ref, route_ref, cnt_ref, base_sc = refs[_N_MERGE_REFS:]

    @pl.when(pl.program_id(0) == 0)
    def _():
        base_sc[...] = jnp.zeros(base_sc.shape, F32)

    h1 = _merge_residual(*mix_refs, tm=tm, S=S)
    h1_ref[...] = h1
    f = _rms(h1, g_ref[...])
    width = f_ref.shape[2]
    for p in range(f_ref.shape[0]):
        f_ref[p] = _pack_bf16_pairs(f[:, 2 * p * width:(2 * p + 1) * width],
                                    f[:, (2 * p + 1) * width:(2 * p + 2) * width])

    lane = lax.broadcasted_iota(jnp.int32, (tm, LANES), 1)
    f_hi = f.astype(BF16)
    f_lo = (f - f_hi.astype(F32)).astype(BF16)
    hi_terms = _dot(f_hi, wr_ref[...])
    logits = hi_terms[:, :LANES] + hi_terms[:, LANES:] + _dot(f_lo, wr_ref[:, :LANES]) + br_ref[...]
    lowest = jnp.finfo(F32).min
    lane_f = lane.astype(F32)
    lg = jnp.where(lane < N_EXPERTS, logits, lowest)
    m1 = jnp.max(lg, axis=-1, keepdims=True)
    i1 = jnp.min(jnp.where(lg == m1, lane_f, float(LANES)), axis=-1, keepdims=True)
    lg2 = jnp.where(lane_f == i1, lowest, lg)
    m2 = jnp.max(lg2, axis=-1, keepdims=True)
    i2 = jnp.min(jnp.where(lg2 == m2, lane_f, float(LANES)), axis=-1, keepdims=True)
    e2 = jnp.exp(m2 - m1)
    gate1 = 1.0 / (1.0 + e2)
    gate2 = e2 / (1.0 + e2)

    oh1 = lane_f == i1
    oh2 = lane_f == i2
    both = jnp.where(oh1 | oh2, 1.0, 0.0)
    r_i = lax.broadcasted_iota(jnp.int32, (tm, tm), 0)
    c_i = lax.broadcasted_iota(jnp.int32, (tm, tm), 1)
    before = jnp.where(r_i > c_i, 1.0, 0.0).astype(BF16)
    seen = _dot(before, both.astype(BF16)) + base_sc[...]
    rank1 = jnp.sum(jnp.where(oh1, seen, 0.0), axis=-1, keepdims=True)
    rank2 = jnp.sum(jnp.where(oh2, seen, 0.0), axis=-1, keepdims=True)
    base_sc[...] = base_sc[...] + jnp.sum(both, axis=0, keepdims=True)
    cnt_ref[...] = base_sc[...]

    route = jnp.zeros((tm, LANES), F32)
    for col, val in enumerate((i1, i2, gate1, gate2, rank1, rank2)):
        route = jnp.where(lane == col, val, route)
    route_ref[...] = route


def _post_mixer(onsa, u, h, mix_w, S, ffn=None, router=None, tm=512, fc=256):
    T = h.shape[0]
    g_mix, w_mg, pool_w, pool_scale, wn, wp, wo, g_ffn = mix_w
    tile = lambda w: pl.BlockSpec((tm, w), lambda i: (i, 0))
    halo_blocks = tm // _POOL_HALO
    in_specs = [
        tile(NSA_WIDTH), tile(POOL_WIDTH),
        pl.BlockSpec((_POOL_HALO, POOL_WIDTH), lambda i: (jnp.maximum(i * halo_blocks - 1, 0), 0)),
        tile(D_MODEL), _const_spec((1, D_MODEL)), _const_spec((D_MODEL, 2 * D_MODEL)),
        _const_spec((len(POOL_WINDOWS), POOL_GROUP_WIDTH, POOL_GROUP_WIDTH)), _const_spec((1, POOL_WIDTH)),
        _const_spec((NSA_WIDTH, D_MODEL)), _const_spec((POOL_WIDTH, D_MODEL)),
        _const_spec((D_MODEL, D_MODEL)), _const_spec((1, D_MODEL)),
    ]
    args = [onsa, u, u, h, g_mix, w_mg, pool_w, pool_scale, wn, wp, wo, g_ffn]
    if router is None:
        wg, wu, wd = ffn
        d_ff = wg.shape[1]
        return pl.pallas_call(
            functools.partial(_post_dense_kernel, tm=tm, S=S, fc=fc),
            grid=(T // tm,),
            in_specs=in_specs + [_const_spec((D_MODEL, d_ff)), _const_spec((D_MODEL, d_ff)),
                                 _const_spec((d_ff, D_MODEL))],
            out_specs=tile(D_MODEL),
            out_shape=jax.ShapeDtypeStruct((T, D_MODEL), F32),
            compiler_params=_params("parallel"),
            name="post_mixer_dense",
        )(*args, wg, wu, wd)
    w_router, b_router = router
    return pl.pallas_call(
        functools.partial(_post_moe_kernel, tm=tm, S=S),
        grid=(T // tm,),
        in_specs=in_specs + [_const_spec((D_MODEL, 2 * LANES)), _const_spec((1, LANES))],
        out_specs=[tile(D_MODEL), pl.BlockSpec((N_PLANES, tm, PLANE_WIDTH), lambda i: (0, i, 0)), tile(LANES),
                   pl.BlockSpec((1, LANES), lambda i: (0, 0))],
        out_shape=[
            jax.ShapeDtypeStruct((T, D_MODEL), F32),
            jax.ShapeDtypeStruct((N_PLANES, T, PLANE_WIDTH), jnp.uint32),
            jax.ShapeDtypeStruct((T, LANES), F32),
            jax.ShapeDtypeStruct((1, LANES), F32),
        ],
        scratch_shapes=[pltpu.VMEM((1, LANES), F32)],
        compiler_params=_params("arbitrary"),
        name="post_mixer_moe",
    )(*args, w_router, b_router)


PLANE_WIDTH = 256
N_PLANES = D_MODEL // (2 * PLANE_WIDTH)
SC_WINDOW = 128


def _sc_mesh():
    return plsc.VectorSubcoreMesh(core_axis_name="core", subcore_axis_name="subcore")


def _pack_bf16_pairs(lo, hi):
    lo_bits = lax.bitcast_convert_type(lo.astype(BF16).astype(F32), jnp.uint32)
    hi_bits = lax.bitcast_convert_type(hi.astype(BF16).astype(F32), jnp.uint32)
    return (lo_bits >> 16) | (hi_bits & jnp.uint32(0xFFFF0000))


def _unpack_bf16_pairs(words):
    lo = lax.bitcast_convert_type(words << 16, F32)
    hi = lax.bitcast_convert_type(words & jnp.uint32(0xFFFF0000), F32)
    return lo, hi


def _sc_scatter_pair(rows, idx_a, idx_b, n_out):
    n_rows, width = rows.shape

    @pl.kernel(out_type=jax.ShapeDtypeStruct((n_out, width), rows.dtype), mesh=_sc_mesh(), scratch_types=[])
    def scatter(rows_hbm, a_hbm, b_hbm, out_hbm):
        def body(rows_vmem, a_vmem, b_vmem):
            pltpu.sync_copy(rows_vmem, out_hbm.at[a_vmem.at[0]])
            pltpu.sync_copy(rows_vmem, out_hbm.at[b_vmem.at[0]])

        idx_spec = pl.BlockSpec((1, SC_WINDOW), lambda i: (0, i))
        pltpu.emit_pipeline(
            body,
            grid=(n_rows // SC_WINDOW,),
            in_specs=[pl.BlockSpec((SC_WINDOW, width), lambda i: (i, 0)), idx_spec, idx_spec],
            out_specs=[],
            core_axis_name=("core", "subcore"),
            dimension_semantics=(pltpu.PARALLEL,),
        )(rows_hbm, a_hbm, b_hbm)

    return scatter(rows, idx_a, idx_b)


def _sc_gather(rows, idx):
    n_out = idx.shape[1]
    width = rows.shape[1]

    @pl.kernel(out_type=jax.ShapeDtypeStruct((n_out, width), rows.dtype), mesh=_sc_mesh(), scratch_types=[])
    def gather(rows_hbm, idx_hbm, out_hbm):
        def body(idx_vmem, out_vmem):
            pltpu.sync_copy(rows_hbm.at[idx_vmem.at[0]], out_vmem)

        pltpu.emit_pipeline(
            body,
            grid=(n_out // SC_WINDOW,),
            in_specs=[pl.BlockSpec((1, SC_WINDOW), lambda i: (0, i))],
            out_specs=[pl.BlockSpec((SC_WINDOW, width), lambda i: (i, 0))],
            core_axis_name=("core", "subcore"),
            dimension_semantics=(pltpu.PARALLEL,),
        )(idx_hbm, out_hbm)

    return gather(rows, idx)


def _dispatch(f_planes, dest):
    n_planes, T, width = f_planes.shape
    n_rows = 2 * T
    plane_base = (jnp.arange(n_planes, dtype=jnp.int32) * n_rows)[:, None]
    idx = [(plane_base + dest[:, k][None, :]).reshape(1, n_planes * T) for k in range(2)]
    out = _sc_scatter_pair(f_planes.reshape(n_planes * T, width), idx[0], idx[1], n_planes * n_rows)
    return out.reshape(n_planes, n_rows, width)


def _undispatch(y_planes, dest):
    n_planes, n_rows, width = y_planes.shape
    T = dest.shape[0]
    plane_base = (jnp.arange(n_planes, dtype=jnp.int32) * n_rows)[None, :, None]
    idx = (plane_base + dest.T[:, None, :]).reshape(1, 2 * n_planes * T)
    out = _sc_gather(y_planes.reshape(n_planes * n_rows, width), idx)
    return out.reshape(2, n_planes, T, width)


def _combine_kernel(h_ref, route_ref, z_ref, g_ref, o_ref, *, final_norm):
    route = route_ref[...]
    z = [jnp.concatenate([half for p in range(z_ref.shape[1]) for half in _unpack_bf16_pairs(z_ref[k, p])], axis=-1)
         for k in range(2)]
    out = h_ref[...] + (route[:, 2:3] * z[0] + route[:, 3:4] * z[1])
    if final_norm:
        out = _rms(out, g_ref[...])
    o_ref[...] = out


def _combine(h1, route, z, g_final, final_norm, tg=512):
    T = h1.shape[0]
    return pl.pallas_call(
        functools.partial(_combine_kernel, final_norm=final_norm),
        grid=(T // tg,),
        in_specs=[
            pl.BlockSpec((tg, D_MODEL), lambda i: (i, 0)),
            pl.BlockSpec((tg, LANES), lambda i: (i, 0)),
            pl.BlockSpec((2, z.shape[1], tg, z.shape[3]), lambda i: (0, 0, i, 0)),
            _const_spec((1, D_MODEL)),
        ],
        out_specs=pl.BlockSpec((tg, D_MODEL), lambda i: (i, 0)),
        out_shape=jax.ShapeDtypeStruct((T, D_MODEL), F32),
        compiler_params=_params("parallel"),
        name="moe_combine",
    )(h1, route, z, g_final)


def _moe_kernel(blk_ref, exp_ref, lo_ref, hi_ref, first_ref, x_ref, wg_ref, wu_ref, wd_ref, y_ref, acc_sc, *, bm):
    v = pl.program_id(0)
    c = pl.program_id(1)

    @pl.when((first_ref[v] == 1) & (c == 0))
    def _():
        acc_sc[...] = jnp.zeros(acc_sc.shape, F32)

    lo = lo_ref[v]
    hi = hi_ref[v]

    @pl.when(hi > lo)
    def _():
        row = blk_ref[v] * bm + lax.broadcasted_iota(jnp.int32, (bm, 1), 0)
        x = jnp.concatenate([half for p in range(x_ref.shape[0]) for half in _unpack_bf16_pairs(x_ref[p])], axis=-1)
        x = jnp.where((row >= lo) & (row < hi), x, 0.0).astype(BF16)
        g = _dot(x, wg_ref[0].astype(BF16))
        u = _dot(x, wu_ref[0].astype(BF16))
        acc_sc[...] += _dot((jax.nn.silu(g) * u).astype(BF16), wd_ref[0].astype(BF16))

    @pl.when(c == pl.num_programs(1) - 1)
    def _():
        width = y_ref.shape[2]
        for p in range(y_ref.shape[0]):
            y_ref[p] = _pack_bf16_pairs(acc_sc[:, 2 * p * width:(2 * p + 1) * width],
                                        acc_sc[:, (2 * p + 1) * width:(2 * p + 2) * width])


def _grouped_ffn(x_sorted, visits, wg, wu, wd, layer, bm=1024, fc=512):
    n_planes, n_rows, width = x_sorted.shape
    n_visits = visits[0].shape[0]
    n_exp, _, d_ff = wg.shape[1:]
    first = layer * n_exp
    flat = lambda w: w.reshape((w.shape[0] * n_exp,) + w.shape[2:])
    rows_spec = pl.BlockSpec((n_planes, bm, width), lambda v, c, blk, ex, lo, hi, fi: (0, blk[v], 0))
    grid_spec = pltpu.PrefetchScalarGridSpec(
        num_scalar_prefetch=5,
        grid=(n_visits, d_ff // fc),
        in_specs=[
            rows_spec,
            pl.BlockSpec((1, D_MODEL, fc), lambda v, c, blk, ex, lo, hi, fi: (first + ex[v], 0, c)),
            pl.BlockSpec((1, D_MODEL, fc), lambda v, c, blk, ex, lo, hi, fi: (first + ex[v], 0, c)),
            pl.BlockSpec((1, fc, D_MODEL), lambda v, c, blk, ex, lo, hi, fi: (first + ex[v], c, 0)),
        ],
        out_specs=rows_spec,
        scratch_shapes=[pltpu.VMEM((bm, D_MODEL), F32)],
    )
    return pl.pallas_call(
        functools.partial(_moe_kernel, bm=bm),
        grid_spec=grid_spec,
        out_shape=jax.ShapeDtypeStruct((n_planes, n_rows, width), jnp.uint32),
        compiler_params=_params("arbitrary", "arbitrary"),
        name="grouped_ffn",
    )(*visits, x_sorted, flat(wg), flat(wu), flat(wd))


def _routing_tables(route, counts, n_rows, bm):
    counts = counts[0, :N_EXPERTS].astype(jnp.int32)
    ends = jnp.cumsum(counts)
    starts = ends - counts
    e = route[:, 0:2].astype(jnp.int32)
    seg_start = jnp.sum(jnp.where(e[..., None] == jnp.arange(N_EXPERTS), starts, 0), axis=-1)
    dest = seg_start + route[:, 4:6].astype(jnp.int32)
    n_blocks = n_rows // bm
    cuts = jnp.sort(jnp.concatenate([jnp.arange(n_blocks, dtype=jnp.int32) * bm, starts[1:]]))
    lo = cuts
    hi = jnp.concatenate([cuts[1:], jnp.array([n_rows], jnp.int32)])
    blk = jnp.minimum(lo // bm, n_blocks - 1)
    ex = jnp.minimum(jnp.sum(ends[None, :] <= lo[:, None], axis=1), N_EXPERTS - 1).astype(jnp.int32)
    first = jnp.concatenate([jnp.ones((1,), jnp.int32), (blk[1:] != blk[:-1]).astype(jnp.int32)])
    return dest, (blk, ex, lo, hi, first)


def _layer_weights(w_in, cmp_pos, cmp_w1, cmp_w2):
    w_main = jnp.concatenate([w_in[:, _Q0:_GL0], w_in[:, _U0:_MG0]], axis=1).astype(BF16)
    w_mg = w_in[:, _MG0:].astype(BF16)
    gl = w_in[:, _GL0:_U0].reshape(D_MODEL, KV_GROUPS, _GATES_PER_GROUP)
    w_gl = jnp.pad(gl, ((0, 0), (0, 0), (0, LANES - _GATES_PER_GROUP))).reshape(D_MODEL, KV_GROUPS * LANES)
    half = CMP_STRIDE * HEAD_DIM
    pos = cmp_pos.reshape(2, 2, half)
    w1 = cmp_w1.reshape(2, 2, half, CMP_HIDDEN).astype(BF16)
    return w_main, w_mg, w_gl.astype(BF16), pos, w1, cmp_w2.astype(BF16)


def _cmp_to_sel_map(S):
    n_blk = S // SEL_BLOCK
    n_cmp = (S - CMP_LEN) // CMP_STRIDE + 1
    nck = S // CMP_STRIDE
    cs = np.arange(nck) * CMP_STRIDE
    bs = np.arange(n_blk) * SEL_BLOCK
    overlap = (cs[None, :] <= bs[:, None] + SEL_BLOCK - 1) & (cs[None, :] + CMP_LEN - 1 >= bs[:, None])
    overlap &= (np.arange(nck) < n_cmp)[None, :]
    return jnp.asarray(overlap.astype(np.float32))


def kernel(x, norm_mix_g, w_in, cmp_pos, cmp_w1, cmp_w2, w_nsa_up, pool_w, pool_scale, w_pool_up, w_out,
           norm_ffn_g, ffn_w_gate, ffn_w_up, ffn_w_down, moe_w_router, moe_b_router, moe_w_gate, moe_w_up,
           moe_w_down, final_norm_g):
    B, S, D = x.shape
    T = B * S
    depth = w_in.shape[0]
    cmp_map = _cmp_to_sel_map(S)
    attn_tq, attn_tk = 256, 512
    attn_tables = _attention_tables(S, attn_tq, attn_tk)
    h = x.reshape(T, D)
    bm = 1024
    for i in range(depth):
        w_main, w_mg, w_gl, pos, w1, w2 = _layer_weights(w_in[i], cmp_pos[i], cmp_w1[i], cmp_w2[i])
        q8, xc, kwin, ksx, vx, u, gls = _in_proj(h, norm_mix_g[i][None], w_main, w_gl, S)
        kc, vc = _compress(xc, pos, w1, w2, B, S)
        ocmp, unsel = _cmp_select(q8, kc, vc, cmp_map, B, S)
        onsa = _sparse_attention(q8, unsel, ksx, kwin, vx, ocmp, gls, attn_tables, B, S, attn_tq, attn_tk)
        mix_w = (norm_mix_g[i][None], w_mg, pool_w[i].astype(BF16), pool_scale[i][None], w_nsa_up[i].astype(BF16),
                 w_pool_up[i].astype(BF16), w_out[i].astype(BF16), norm_ffn_g[i][None])
        j = i // 2
        if i % 2 == 0:
            ffn = (ffn_w_gate[j].astype(BF16), ffn_w_up[j].astype(BF16), ffn_w_down[j].astype(BF16))
            h = _post_mixer(onsa, u, h, mix_w, S, ffn=ffn)
        else:
            w_router = jnp.pad(moe_w_router[j], ((0, 0), (0, LANES - N_EXPERTS)))
            w_router_hi = w_router.astype(BF16)
            w_router_lo = (w_router - w_router_hi.astype(F32)).astype(BF16)
            w_router = jnp.concatenate([w_router_hi, w_router_lo], axis=1)
            b_router = jnp.pad(moe_b_router[j], (0, LANES - N_EXPERTS))[None]
            h1, f, route, counts = _post_mixer(onsa, u, h, mix_w, S, router=(w_router, b_router))
            dest, visits = _routing_tables(route, counts, 2 * T, bm)
            x_sorted = _dispatch(f, dest)
            y_sorted = _grouped_ffn(x_sorted, visits, moe_w_gate, moe_w_up, moe_w_down, j, bm=bm)
            z = _undispatch(y_sorted, dest)
            h = _combine(h1, route, z, final_norm_g[None], final_norm=(i == depth - 1))
    return h.reshape(B, S, D)
```

```python
import functools

import numpy as np
import jax
import jax.numpy as jnp
from jax import lax
from jax.experimental import pallas as pl
from jax.experimental.pallas import tpu as pltpu
from jax.experimental.pallas import tpu_sc as plsc

D_MODEL = 1024
NSA_HEADS = 8
HEAD_DIM = 64
KV_GROUPS = 2
HEADS_PER_GROUP = NSA_HEADS // KV_GROUPS
NSA_WIDTH = NSA_HEADS * HEAD_DIM
KV_WIDTH = KV_GROUPS * HEAD_DIM
CMP_LEN = 32
CMP_STRIDE = 16
CMP_HIDDEN = 256
SEL_BLOCK = 64
SEL_TOPN = 8
WINDOW = 512
ATTN_SCALE = HEAD_DIM ** -0.5
POOL_WINDOWS = (2, 4, 8, 16)
POOL_GROUP_WIDTH = 128
POOL_WIDTH = 512
N_EXPERTS = 8
D_FF_EXPERT = 3584
RMS_EPS = 1e-6
NEG_INF = -1e30
FORCE = 1e9

LOG2E = 1.4426950408889634
MASK_BIG = 2.0 ** 100

LANES = 128
VMEM_LIMIT = 56 * 1024 * 1024

BF16 = jnp.bfloat16
F32 = jnp.float32

_Q0 = 0
_KV0 = NSA_WIDTH
_GL0 = _KV0 + 6 * KV_WIDTH
_U0 = _GL0 + 3 * NSA_HEADS
_MG0 = _U0 + POOL_WIDTH
_MAIN_COLS = NSA_WIDTH + 6 * KV_WIDTH + POOL_WIDTH
_GATES_PER_GROUP = 3 * HEADS_PER_GROUP


def _dot(a, b, **kw):
    return jnp.dot(a, b, preferred_element_type=F32, **kw)


def _dot_tn(a, b):
    return lax.dot_general(a, b, (((0,), (0,)), ((), ())), preferred_element_type=F32)


def _dot_nt(a, b):
    return lax.dot_general(a, b, (((1,), (1,)), ((), ())), preferred_element_type=F32)


def _rms(x, g):
    return x * lax.rsqrt(jnp.mean(x * x, axis=-1, keepdims=True) + RMS_EPS) * g


def _params(*sem):
    return pltpu.CompilerParams(dimension_semantics=sem, vmem_limit_bytes=VMEM_LIMIT)


def _const_spec(shape):
    nd = len(shape)
    return pl.BlockSpec(shape, lambda *_: (0,) * nd, pipeline_mode=pl.Buffered(1))


def _in_kernel(h_ref, g_ref, w_ref, wgl_ref, chunk_ref, q_ref, xc_ref, kw_ref, ksx_ref, vx_ref, u_ref, gl_ref, *, S):
    a = _rms(h_ref[...], g_ref[...]).astype(BF16)
    tm = a.shape[0]
    zeros = jnp.zeros((tm, HEAD_DIM), F32)
    ones = jnp.ones((tm, HEAD_DIM), F32)
    pos = (pl.program_id(0) * tm + lax.broadcasted_iota(jnp.int32, (tm, HEAD_DIM), 0)) % S
    lane = lax.broadcasted_iota(jnp.int32, (tm, HEAD_DIM), 1)
    block_code = jnp.where(pos // SEL_BLOCK == lane, -MASK_BIG, 0.0)

    def wide(lo, hi):
        return jnp.concatenate([lo, hi], axis=-1).astype(BF16)

    q = _dot(a, w_ref[:, 0:NSA_WIDTH]) * (ATTN_SCALE * LOG2E)
    for h in range(NSA_HEADS):
        q_ref[h] = wide(q[:, h * HEAD_DIM:(h + 1) * HEAD_DIM], zeros)
    for c in range(3):
        c0 = NSA_WIDTH + c * 256
        kv = _dot(a, w_ref[:, c0:c0 + 256])
        pieces = [kv[:, i * HEAD_DIM:(i + 1) * HEAD_DIM] for i in range(4)]
        if c == 0:
            n_chunks = tm // CMP_STRIDE
            for i in range(4):
                by_offset = _dot(chunk_ref[...], pieces[i].astype(BF16))
                for l in range(CMP_STRIDE):
                    xc_ref[i, :, l * HEAD_DIM:(l + 1) * HEAD_DIM] = (
                        by_offset[l * n_chunks:(l + 1) * n_chunks].astype(BF16))
            continue
        for g in range(KV_GROUPS):
            if c == 1:
                ksx_ref[g] = wide(pieces[g], block_code)
            else:
                kw_ref[g] = pieces[g].astype(BF16)
            vx_ref[2 * (c - 1) + g] = wide(pieces[2 + g], ones)
    u0 = NSA_WIDTH + 6 * KV_WIDTH
    u_ref[...] = _dot(a, w_ref[:, u0:u0 + POOL_WIDTH])
    gl = jax.nn.sigmoid(_dot(a, wgl_ref[...]))
    for g in range(KV_GROUPS):
        gl_ref[g] = gl[:, g * LANES:(g + 1) * LANES]


def _in_proj(h, g, w_main, w_gl, S, tm=512):
    T = h.shape[0]
    return pl.pallas_call(
        functools.partial(_in_kernel, S=S),
        grid=(T // tm,),
        in_specs=[
            pl.BlockSpec((tm, D_MODEL), lambda i: (i, 0)),
            _const_spec((1, D_MODEL)),
            _const_spec((D_MODEL, _MAIN_COLS)),
            _const_spec((D_MODEL, KV_GROUPS * LANES)),
            _const_spec((tm, tm)),
        ],
        out_specs=[
            pl.BlockSpec((NSA_HEADS, tm, LANES), lambda i: (0, i, 0)),
            pl.BlockSpec((4, tm // CMP_STRIDE, CMP_STRIDE * HEAD_DIM), lambda i: (0, i, 0)),
            pl.BlockSpec((KV_GROUPS, tm, HEAD_DIM), lambda i: (0, i, 0)),
            pl.BlockSpec((KV_GROUPS, tm, LANES), lambda i: (0, i, 0)),
            pl.BlockSpec((4, tm, LANES), lambda i: (0, i, 0)),
            pl.BlockSpec((tm, POOL_WIDTH), lambda i: (i, 0)),
            pl.BlockSpec((KV_GROUPS, tm, LANES), lambda i: (0, i, 0)),
        ],
        out_shape=[
            jax.ShapeDtypeStruct((NSA_HEADS, T, LANES), BF16),
            jax.ShapeDtypeStruct((4, T // CMP_STRIDE, CMP_STRIDE * HEAD_DIM), BF16),
            jax.ShapeDtypeStruct((KV_GROUPS, T, HEAD_DIM), BF16),
            jax.ShapeDtypeStruct((KV_GROUPS, T, LANES), BF16),
            jax.ShapeDtypeStruct((4, T, LANES), BF16),
            jax.ShapeDtypeStruct((T, POOL_WIDTH), F32),
            jax.ShapeDtypeStruct((KV_GROUPS, T, LANES), F32),
        ],
        compiler_params=_params("parallel"),
        name="in_proj",
    )(h, g, w_main, w_gl, _chunk_permutation(tm))


def _chunk_permutation(tm):
    n_chunks = tm // CMP_STRIDE
    perm = np.zeros((tm, tm), np.float32)
    l, n = np.meshgrid(np.arange(CMP_STRIDE), np.arange(n_chunks), indexing="ij")
    perm[(l * n_chunks + n).ravel(), (CMP_STRIDE * n + l).ravel()] = 1.0
    return jnp.asarray(perm, dtype=BF16)


def _cmp_kernel(x_ref, pos_ref, w1_ref, w2_ref, kc_ref, vc_ref):
    n_chunks = x_ref.shape[1]
    for j in range(2):
        out_ref = kc_ref if j == 0 else vc_ref
        for g in range(KV_GROUPS):
            x = x_ref[j * KV_GROUPS + g].astype(F32)
            lo = (x + pos_ref[j, 0:1, :]).astype(BF16)
            hi = (x + pos_ref[j, 1:2, :]).astype(BF16)
            hid = _dot(lo, w1_ref[j, 0]) + pltpu.roll(_dot(hi, w1_ref[j, 1]), n_chunks - 1, axis=0)
            act = jax.nn.gelu(hid).astype(BF16)
            out_ref[0, g] = _dot(act, w2_ref[j]).astype(BF16)


def _compress(x, pos, w1, w2, B, S):
    nck = S // CMP_STRIDE
    half = CMP_STRIDE * HEAD_DIM
    out = jax.ShapeDtypeStruct((B, KV_GROUPS, nck, HEAD_DIM), BF16)
    return pl.pallas_call(
        _cmp_kernel,
        grid=(B,),
        in_specs=[
            pl.BlockSpec((4, nck, half), lambda b: (0, b, 0)),
            _const_spec((2, 2, half)),
            _const_spec((2, 2, half, CMP_HIDDEN)),
            _const_spec((2, CMP_HIDDEN, HEAD_DIM)),
        ],
        out_specs=[pl.BlockSpec((1, KV_GROUPS, nck, HEAD_DIM), lambda b: (b, 0, 0, 0))] * 2,
        out_shape=[out, out],
        compiler_params=_params("parallel"),
        name="compress",
    )(x, pos, w1, w2)


def _sel_kernel(q_ref, kc_ref, vc_ref, map_ref, ocmp_ref, unsel_ref, *, tq, n_blk):
    i = pl.program_id(2)
    kc = kc_ref[0, 0]
    vc = vc_ref[0, 0]
    nck = kc.shape[0]
    n_idx = lax.broadcasted_iota(jnp.int32, (nck, tq), 0)
    t_idx = i * tq + lax.broadcasted_iota(jnp.int32, (nck, tq), 1)
    valid = n_idx * CMP_STRIDE + (CMP_LEN - 1) <= t_idx
    sees_any = t_idx[0:1, :] >= CMP_LEN - 1
    psum = jnp.zeros((nck, tq), F32)
    for r in range(HEADS_PER_GROUP):
        s = jnp.where(valid, _dot_nt(kc, q_ref[r][:, :HEAD_DIM]), NEG_INF)
        e = jnp.exp2(s - jnp.max(s, axis=0, keepdims=True))
        p = e * jnp.where(sees_any, 1.0 / jnp.sum(e, axis=0, keepdims=True), 0.0)
        ocmp_ref[r] = _dot_tn(p.astype(BF16), vc)
        psum = psum + p
    imp = _dot(map_ref[...], psum, precision=lax.Precision.HIGHEST)
    blk = lax.broadcasted_iota(jnp.int32, (n_blk, tq), 0)
    t_blk = i * tq + lax.broadcasted_iota(jnp.int32, (n_blk, tq), 1)
    cur = lax.shift_right_logical(t_blk, SEL_BLOCK.bit_length() - 1)
    imp = jnp.where(blk > cur, -FORCE, imp)
    imp = jnp.where((blk == 0) | (blk == cur), FORCE, imp)
    rank = jnp.zeros((n_blk, tq), F32)
    for j in range(n_blk):
        row = imp[j:j + 1, :]
        ahead = (row > imp) | ((row == imp) & (blk > j))
        rank = rank + jnp.where(ahead, 1.0, 0.0)
    unsel = jnp.where(rank < SEL_TOPN, 0.0, 1.0)
    table = jnp.concatenate([jnp.zeros((HEAD_DIM, tq), F32), unsel,
                             jnp.zeros((LANES - HEAD_DIM - n_blk, tq), F32)], axis=0).astype(BF16)
    eye = jnp.where(lax.broadcasted_iota(jnp.int32, (LANES, LANES), 0)
                    == lax.broadcasted_iota(jnp.int32, (LANES, LANES), 1), 1.0, 0.0).astype(BF16)
    unsel_ref[0] = _dot_tn(table, eye).astype(BF16)


def _cmp_select(q8, kc, vc, cmp_map, B, S, tq=512):
    T = B * S
    n_blk = S // SEL_BLOCK
    nck = S // CMP_STRIDE
    nq = S // tq
    hp = HEADS_PER_GROUP
    return pl.pallas_call(
        functools.partial(_sel_kernel, tq=tq, n_blk=n_blk),
        grid=(B, KV_GROUPS, nq),
        in_specs=[
            pl.BlockSpec((hp, tq, LANES), lambda b, g, i: (g, b * nq + i, 0)),
            pl.BlockSpec((1, 1, nck, HEAD_DIM), lambda b, g, i: (b, g, 0, 0)),
            pl.BlockSpec((1, 1, nck, HEAD_DIM), lambda b, g, i: (b, g, 0, 0)),
            _const_spec((n_blk, nck)),
        ],
        out_specs=[
            pl.BlockSpec((hp, tq, HEAD_DIM), lambda b, g, i: (g, b * nq + i, 0)),
            pl.BlockSpec((1, tq, LANES), lambda b, g, i: (g, b * nq + i, 0)),
        ],
        out_shape=[
            jax.ShapeDtypeStruct((NSA_HEADS, T, HEAD_DIM), F32),
            jax.ShapeDtypeStruct((KV_GROUPS, T, LANES), BF16),
        ],
        compiler_params=_params("parallel", "parallel", "parallel"),
        name="cmp_select",
    )(q8, kc, vc, cmp_map)


def _attn_kernel(q_ref, unsel_ref, ks_ref, vs_ref, kw_ref, vw_ref, ocmp_ref, gl_ref, diag_ref, wint_ref,
                 o_ref, *, tq, tk, S):
    i = pl.program_id(2)
    hp = HEADS_PER_GROUP
    rows = hp * tq
    span = WINDOW + tq

    def sel_tile(q4, k0, width, causal=None, state=None):
        s = _dot_nt(q4, ks_ref[0, k0:k0 + width, :])
        if causal is not None:
            s = (s.reshape(hp, tq, width) + causal[None]).reshape(rows, width)
        m_new = jnp.broadcast_to(jnp.max(s, axis=-1, keepdims=True), (rows, LANES))
        if state is not None:
            m_new = jnp.maximum(state[0], m_new)
        p = jnp.exp2(s - jnp.concatenate([m_new] * (width // LANES), axis=1))
        pv = _dot(p.astype(BF16), vs_ref[0, k0:k0 + width, :])
        return m_new, pv if state is None else jnp.exp2(state[0] - m_new) * state[1] + pv

    def step(n_full):
        q4 = (q_ref[...] + unsel_ref[...]).reshape(rows, LANES)

        w0 = pl.multiple_of(jnp.clip(i * tq - WINDOW, 0, S - span), LANES)
        kw = kw_ref[0, pl.ds(w0, span), :]
        vw = vw_ref[0, pl.ds(w0, span), :]
        s = (_dot_nt(q4[:, :HEAD_DIM], kw).reshape(hp, tq, span) + wint_ref[0][None]).reshape(rows, span)
        p = jnp.exp2(s - jnp.max(s, axis=-1, keepdims=True))
        accw = _dot(p.astype(BF16), vw)
        o_win = accw[:, :HEAD_DIM] / accw[:, HEAD_DIM:HEAD_DIM + 1]

        state = sel_tile(q4, n_full * tk, tk, causal=diag_ref[0])
        for j in range(n_full // 2):
            state = sel_tile(q4, j * 2 * tk, 2 * tk, state=state)
        if n_full % 2:
            state = sel_tile(q4, (n_full - 1) * tk, tk, state=state)
        acc = state[1]
        o_sel = acc[:, :HEAD_DIM] / acc[:, HEAD_DIM:HEAD_DIM + 1]

        gl = gl_ref[0]
        outs = []
        for r in range(hp):
            rs = slice(r * tq, (r + 1) * tq)
            outs.append(gl[:, 3 * r:3 * r + 1] * ocmp_ref[r]
                        + gl[:, 3 * r + 1:3 * r + 2] * o_sel[rs]
                        + gl[:, 3 * r + 2:3 * r + 3] * o_win[rs])
        o_ref[...] = jnp.concatenate(outs, axis=-1).astype(BF16)

    for n_full in range(((S - tq) // tk) + 1):
        pl.when((i * tq) // tk == n_full)(functools.partial(step, n_full))


def _attention_tables(S, tq, tk):
    a = np.arange(tq)[:, None]
    c = np.arange(tk)[None, :]
    diag = np.stack([np.where(c <= p * tq + a, 0.0, -MASK_BIG) for p in range(tk // tq)])
    span = WINDOW + tq
    win = []
    for i in range(WINDOW // tq + 1):
        t = i * tq + a
        kp = min(max(i * tq - WINDOW, 0), S - span) + np.arange(span)[None, :]
        win.append(np.where((kp <= t) & (kp > t - WINDOW), 0.0, -MASK_BIG))
    return jnp.asarray(diag.astype(np.float32)), jnp.asarray(np.stack(win).astype(np.float32))


def _sparse_attention(q8, unsel, ksx, kwin, vx, ocmp, gls, tables, B, S, tq, tk):
    diag, win = tables
    T = B * S
    nq = S // tq
    hp = HEADS_PER_GROUP
    span = WINDOW + tq
    n_win = WINDOW // tq
    q_tile = lambda width: pl.BlockSpec((hp, tq, width), lambda b, g, i: (g, b * nq + i, 0))
    group_tile = pl.BlockSpec((1, tq, LANES), lambda b, g, i: (g, b * nq + i, 0))
    seq = lambda first, width: pl.BlockSpec((1, S, width), lambda b, g, i: (first + g, b, 0))
    return pl.pallas_call(
        functools.partial(_attn_kernel, tq=tq, tk=tk, S=S),
        grid=(B, KV_GROUPS, nq),
        in_specs=[
            q_tile(LANES), group_tile,
            seq(0, LANES), seq(0, LANES), seq(0, HEAD_DIM), seq(2, LANES),
            q_tile(HEAD_DIM), group_tile,
            pl.BlockSpec((1, tq, tk), lambda b, g, i: (i % (tk // tq), 0, 0)),
            pl.BlockSpec((1, tq, span), lambda b, g, i: (jnp.minimum(i, n_win), 0, 0)),
        ],
        out_specs=pl.BlockSpec((tq, hp * HEAD_DIM), lambda b, g, i: (b * nq + i, g)),
        out_shape=jax.ShapeDtypeStruct((T, NSA_WIDTH), BF16),
        compiler_params=_params("parallel", "parallel", "parallel"),
        name="sparse_attention",
    )(q8, unsel, ksx, vx, kwin, vx, ocmp, gls, diag, win)


_POOL_HALO = 16


def _pool_mixer(u_ref, halo_ref, w_ref, scale_ref, tm, S):
    pos0 = (pl.program_id(0) * tm) % S
    halo = jnp.where(pos0 == 0, 0.0, halo_ref[...])
    x = jnp.concatenate([halo, u_ref[...]], axis=0)
    pos = pos0 + lax.broadcasted_iota(jnp.int32, (tm, 1), 0)
    out = []
    for gi, w in enumerate(POOL_WINDOWS):
        c0 = gi * POOL_GROUP_WIDTH
        xg = x[:, c0:c0 + POOL_GROUP_WIDTH]
        tot = xg[_POOL_HALO:_POOL_HALO + tm]
        for d in range(1, w):
            tot = tot + xg[_POOL_HALO - d:_POOL_HALO - d + tm]
        cnt = jnp.minimum(pos + 1, w).astype(F32)
        pooled = tot / cnt - xg[_POOL_HALO:_POOL_HALO + tm]
        mixed = _dot(pooled.astype(BF16), w_ref[gi])
        out.append((mixed * scale_ref[:, c0:c0 + POOL_GROUP_WIDTH]).astype(BF16))
    return jnp.concatenate(out, axis=-1)


_N_MERGE_REFS = 11


def _merge_residual(onsa_ref, u_ref, halo_ref, h_ref, gmix_ref, wmg_ref, poolw_ref, pscale_ref, wn_ref, wp_ref,
                    wo_ref, *, tm, S):
    h = h_ref[...]
    a = _rms(h, gmix_ref[...]).astype(BF16)
    nsa_up = _dot(onsa_ref[...], wn_ref[...])
    pool_up = _dot(_pool_mixer(u_ref, halo_ref, poolw_ref, pscale_ref, tm, S), wp_ref[...])
    merged = (jax.nn.sigmoid(_dot(a, wmg_ref[:, :D_MODEL])) * nsa_up
              + jax.nn.sigmoid(_dot(a, wmg_ref[:, D_MODEL:])) * pool_up)
    return h + _dot(merged.astype(BF16), wo_ref[...])


def _post_dense_kernel(*refs, tm, S, fc):
    mix_refs = refs[:_N_MERGE_REFS]
    gffn_ref, wg_ref, wu_ref, wd_ref, o_ref = refs[_N_MERGE_REFS:]
    h1 = _merge_residual(*mix_refs, tm=tm, S=S)
    f = _rms(h1, gffn_ref[...]).astype(BF16)
    o_ref[...] = h1
    for c in range(wg_ref.shape[1] // fc):
        g = _dot(f, wg_ref[:, c * fc:(c + 1) * fc])
        u = _dot(f, wu_ref[:, c * fc:(c + 1) * fc])
        o_ref[...] += _dot((jax.nn.silu(g) * u).astype(BF16), wd_ref[c * fc:(c + 1) * fc, :])


def _post_moe_kernel(*refs, tm, S):
    mix_refs = refs[:_N_MERGE_REFS]
    g_ref, wr_ref, br_ref, h1_ref, f_ref, route_ref, cnt_ref, base_sc = refs[_N_MERGE_REFS:]

    @pl.when(pl.program_id(0) == 0)
    def _():
        base_sc[...] = jnp.zeros(base_sc.shape, F32)

    h1 = _merge_residual(*mix_refs, tm=tm, S=S)
    h1_ref[...] = h1
    f = _rms(h1, g_ref[...])
    width = f_ref.shape[2]
    for p in range(f_ref.shape[0]):
        f_ref[p] = _pack_bf16_pairs(f[:, 2 * p * width:(2 * p + 1) * width],
                                    f[:, (2 * p + 1) * width:(2 * p + 2) * width])

    lane = lax.broadcasted_iota(jnp.int32, (tm, LANES), 1)
    f_hi = f.astype(BF16)
    f_lo = (f - f_hi.astype(F32)).astype(BF16)
    hi_terms = _dot(f_hi, wr_ref[...])
    logits = hi_terms[:, :LANES] + hi_terms[:, LANES:] + _dot(f_lo, wr_ref[:, :LANES]) + br_ref[...]
    lowest = jnp.finfo(F32).min
    lane_f = lane.astype(F32)
    lg = jnp.where(lane < N_EXPERTS, logits, lowest)
    m1 = jnp.max(lg, axis=-1, keepdims=True)
    i1 = jnp.min(jnp.where(lg == m1, lane_f, float(LANES)), axis=-1, keepdims=True)
    lg2 = jnp.where(lane_f == i1, lowest, lg)
    m2 = jnp.max(lg2, axis=-1, keepdims=True)
    i2 = jnp.min(jnp.where(lg2 == m2, lane_f, float(LANES)), axis=-1, keepdims=True)
    e2 = jnp.exp(m2 - m1)
    gate1 = 1.0 / (1.0 + e2)
    gate2 = e2 / (1.0 + e2)

    oh1 = lane_f == i1
    oh2 = lane_f == i2
    both = jnp.where(oh1 | oh2, 1.0, 0.0)
    r_i = lax.broadcasted_iota(jnp.int32, (tm, tm), 0)
    c_i = lax.broadcasted_iota(jnp.int32, (tm, tm), 1)
    before = jnp.where(r_i > c_i, 1.0, 0.0).astype(BF16)
    seen = _dot(before, both.astype(BF16)) + base_sc[...]
    rank1 = jnp.sum(jnp.where(oh1, seen, 0.0), axis=-1, keepdims=True)
    rank2 = jnp.sum(jnp.where(oh2, seen, 0.0), axis=-1, keepdims=True)
    base_sc[...] = base_sc[...] + jnp.sum(both, axis=0, keepdims=True)
    cnt_ref[...] = base_sc[...]

    route = jnp.zeros((tm, LANES), F32)
    for col, val in enumerate((i1, i2, gate1, gate2, rank1, rank2)):
        route = jnp.where(lane == col, val, route)
    route_ref[...] = route


def _post_mixer(onsa, u, h, mix_w, S, ffn=None, router=None, tm=512, fc=256):
    T = h.shape[0]
    g_mix, w_mg, pool_w, pool_scale, wn, wp, wo, g_ffn = mix_w
    tile = lambda w: pl.BlockSpec((tm, w), lambda i: (i, 0))
    halo_blocks = tm // _POOL_HALO
    in_specs = [
        tile(NSA_WIDTH), tile(POOL_WIDTH),
        pl.BlockSpec((_POOL_HALO, POOL_WIDTH), lambda i: (jnp.maximum(i * halo_blocks - 1, 0), 0)),
        tile(D_MODEL), _const_spec((1, D_MODEL)), _const_spec((D_MODEL, 2 * D_MODEL)),
        _const_spec((len(POOL_WINDOWS), POOL_GROUP_WIDTH, POOL_GROUP_WIDTH)), _const_spec((1, POOL_WIDTH)),
        _const_spec((NSA_WIDTH, D_MODEL)), _const_spec((POOL_WIDTH, D_MODEL)),
        _const_spec((D_MODEL, D_MODEL)), _const_spec((1, D_MODEL)),
    ]
    args = [onsa, u, u, h, g_mix, w_mg, pool_w, pool_scale, wn, wp, wo, g_ffn]
    if router is None:
        wg, wu, wd = ffn
        d_ff = wg.shape[1]
        return pl.pallas_call(
            functools.partial(_post_dense_kernel, tm=tm, S=S, fc=fc),
            grid=(T // tm,),
            in_specs=in_specs + [_const_spec((D_MODEL, d_ff)), _const_spec((D_MODEL, d_ff)),
                                 _const_spec((d_ff, D_MODEL))],
            out_specs=tile(D_MODEL),
            out_shape=jax.ShapeDtypeStruct((T, D_MODEL), F32),
            compiler_params=_params("parallel"),
            name="post_mixer_dense",
        )(*args, wg, wu, wd)
    w_router, b_router = router
    return pl.pallas_call(
        functools.partial(_post_moe_kernel, tm=tm, S=S),
        grid=(T // tm,),
        in_specs=in_specs + [_const_spec((D_MODEL, 2 * LANES)), _const_spec((1, LANES))],
        out_specs=[tile(D_MODEL), pl.BlockSpec((N_PLANES, tm, PLANE_WIDTH), lambda i: (0, i, 0)), tile(LANES),
                   pl.BlockSpec((1, LANES), lambda i: (0, 0))],
        out_shape=[
            jax.ShapeDtypeStruct((T, D_MODEL), F32),
            jax.ShapeDtypeStruct((N_PLANES, T, PLANE_WIDTH), jnp.uint32),
            jax.ShapeDtypeStruct((T, LANES), F32),
            jax.ShapeDtypeStruct((1, LANES), F32),
        ],
        scratch_shapes=[pltpu.VMEM((1, LANES), F32)],
        compiler_params=_params("arbitrary"),
        name="post_mixer_moe",
    )(*args, w_router, b_router)


PLANE_WIDTH = 256
N_PLANES = D_MODEL // (2 * PLANE_WIDTH)
SC_WINDOW = 128


def _sc_mesh():
    return plsc.VectorSubcoreMesh(core_axis_name="core", subcore_axis_name="subcore")


def _pack_bf16_pairs(lo, hi):
    lo_bits = lax.bitcast_convert_type(lo.astype(BF16).astype(F32), jnp.uint32)
    hi_bits = lax.bitcast_convert_type(hi.astype(BF16).astype(F32), jnp.uint32)
    return (lo_bits >> 16) | (hi_bits & jnp.uint32(0xFFFF0000))


def _unpack_bf16_pairs(words):
    lo = lax.bitcast_convert_type(words << 16, F32)
    hi = lax.bitcast_convert_type(words & jnp.uint32(0xFFFF0000), F32)
    return lo, hi


def _sc_scatter_pair(rows, idx_a, idx_b, n_out):
    n_rows, width = rows.shape

    @pl.kernel(out_type=jax.ShapeDtypeStruct((n_out, width), rows.dtype), mesh=_sc_mesh(), scratch_types=[])
    def scatter(rows_hbm, a_hbm, b_hbm, out_hbm):
        def body(rows_vmem, a_vmem, b_vmem):
            pltpu.sync_copy(rows_vmem, out_hbm.at[a_vmem.at[0]])
            pltpu.sync_copy(rows_vmem, out_hbm.at[b_vmem.at[0]])

        idx_spec = pl.BlockSpec((1, SC_WINDOW), lambda i: (0, i))
        pltpu.emit_pipeline(
            body,
            grid=(n_rows // SC_WINDOW,),
            in_specs=[pl.BlockSpec((SC_WINDOW, width), lambda i: (i, 0)), idx_spec, idx_spec],
            out_specs=[],
            core_axis_name=("core", "subcore"),
            dimension_semantics=(pltpu.PARALLEL,),
        )(rows_hbm, a_hbm, b_hbm)

    return scatter(rows, idx_a, idx_b)


def _sc_gather(rows, idx):
    n_out = idx.shape[1]
    width = rows.shape[1]

    @pl.kernel(out_type=jax.ShapeDtypeStruct((n_out, width), rows.dtype), mesh=_sc_mesh(), scratch_types=[])
    def gather(rows_hbm, idx_hbm, out_hbm):
        def body(idx_vmem, out_vmem):
            pltpu.sync_copy(rows_hbm.at[idx_vmem.at[0]], out_vmem)

        pltpu.emit_pipeline(
            body,
            grid=(n_out // SC_WINDOW,),
            in_specs=[pl.BlockSpec((1, SC_WINDOW), lambda i: (0, i))],
            out_specs=[pl.BlockSpec((SC_WINDOW, width), lambda i: (i, 0))],
            core_axis_name=("core", "subcore"),
            dimension_semantics=(pltpu.PARALLEL,),
        )(idx_hbm, out_hbm)

    return gather(rows, idx)


def _dispatch(f_planes, dest):
    n_planes, T, width = f_planes.shape
    n_rows = 2 * T
    plane_base = (jnp.arange(n_planes, dtype=jnp.int32) * n_rows)[:, None]
    idx = [(plane_base + dest[:, k][None, :]).reshape(1, n_planes * T) for k in range(2)]
    out = _sc_scatter_pair(f_planes.reshape(n_planes * T, width), idx[0], idx[1], n_planes * n_rows)
    return out.reshape(n_planes, n_rows, width)


def _undispatch(y_planes, dest):
    n_planes, n_rows, width = y_planes.shape
    T = dest.shape[0]
    plane_base = (jnp.arange(n_planes, dtype=jnp.int32) * n_rows)[None, :, None]
    idx = (plane_base + dest.T[:, None, :]).reshape(1, 2 * n_planes * T)
    out = _sc_gather(y_planes.reshape(n_planes * n_rows, width), idx)
    return out.reshape(2, n_planes, T, width)


def _combine_kernel(h_ref, route_ref, z_ref, g_ref, o_ref, *, final_norm):
    route = route_ref[...]
    z = [jnp.concatenate([half for p in range(z_ref.shape[1]) for half in _unpack_bf16_pairs(z_ref[k, p])], axis=-1)
         for k in range(2)]
    out = h_ref[...] + (route[:, 2:3] * z[0] + route[:, 3:4] * z[1])
    if final_norm:
        out = _rms(out, g_ref[...])
    o_ref[...] = out


def _combine(h1, route, z, g_final, final_norm, tg=512):
    T = h1.shape[0]
    return pl.pallas_call(
        functools.partial(_combine_kernel, final_norm=final_norm),
        grid=(T // tg,),
        in_specs=[
            pl.BlockSpec((tg, D_MODEL), lambda i: (i, 0)),
            pl.BlockSpec((tg, LANES), lambda i: (i, 0)),
            pl.BlockSpec((2, z.shape[1], tg, z.shape[3]), lambda i: (0, 0, i, 0)),
            _const_spec((1, D_MODEL)),
        ],
        out_specs=pl.BlockSpec((tg, D_MODEL), lambda i: (i, 0)),
        out_shape=jax.ShapeDtypeStruct((T, D_MODEL), F32),
        compiler_params=_params("parallel"),
        name="moe_combine",
    )(h1, route, z, g_final)


def _moe_kernel(blk_ref, exp_ref, lo_ref, hi_ref, first_ref, x_ref, wg_ref, wu_ref, wd_ref, y_ref, acc_sc, *, bm):
    v = pl.program_id(0)
    c = pl.program_id(1)

    @pl.when((first_ref[v] == 1) & (c == 0))
    def _():
        acc_sc[...] = jnp.zeros(acc_sc.shape, F32)

    lo = lo_ref[v]
    hi = hi_ref[v]

    @pl.when(hi > lo)
    def _():
        row = blk_ref[v] * bm + lax.broadcasted_iota(jnp.int32, (bm, 1), 0)
        x = jnp.concatenate([half for p in range(x_ref.shape[0]) for half in _unpack_bf16_pairs(x_ref[p])], axis=-1)
        x = jnp.where((row >= lo) & (row < hi), x, 0.0).astype(BF16)
        g = _dot(x, wg_ref[0].astype(BF16))
        u = _dot(x, wu_ref[0].astype(BF16))
        acc_sc[...] += _dot((jax.nn.silu(g) * u).astype(BF16), wd_ref[0].astype(BF16))

    @pl.when(c == pl.num_programs(1) - 1)
    def _():
        width = y_ref.shape[2]
        for p in range(y_ref.shape[0]):
            y_ref[p] = _pack_bf16_pairs(acc_sc[:, 2 * p * width:(2 * p + 1) * width],
                                        acc_sc[:, (2 * p + 1) * width:(2 * p + 2) * width])


def _grouped_ffn(x_sorted, visits, wg, wu, wd, layer, bm=1024, fc=512):
    n_planes, n_rows, width = x_sorted.shape
    n_visits = visits[0].shape[0]
    n_exp, _, d_ff = wg.shape[1:]
    first = layer * n_exp
    flat = lambda w: w.reshape((w.shape[0] * n_exp,) + w.shape[2:])
    rows_spec = pl.BlockSpec((n_planes, bm, width), lambda v, c, blk, ex, lo, hi, fi: (0, blk[v], 0))
    grid_spec = pltpu.PrefetchScalarGridSpec(
        num_scalar_prefetch=5,
        grid=(n_visits, d_ff // fc),
        in_specs=[
            rows_spec,
            pl.BlockSpec((1, D_MODEL, fc), lambda v, c, blk, ex, lo, hi, fi: (first + ex[v], 0, c)),
            pl.BlockSpec((1, D_MODEL, fc), lambda v, c, blk, ex, lo, hi, fi: (first + ex[v], 0, c)),
            pl.BlockSpec((1, fc, D_MODEL), lambda v, c, blk, ex, lo, hi, fi: (first + ex[v], c, 0)),
        ],
        out_specs=rows_spec,
        scratch_shapes=[pltpu.VMEM((bm, D_MODEL), F32)],
    )
    return pl.pallas_call(
        functools.partial(_moe_kernel, bm=bm),
        grid_spec=grid_spec,
        out_shape=jax.ShapeDtypeStruct((n_planes, n_rows, width), jnp.uint32),
        compiler_params=_params("arbitrary", "arbitrary"),
        name="grouped_ffn",
    )(*visits, x_sorted, flat(wg), flat(wu), flat(wd))


def _routing_tables(route, counts, n_rows, bm):
    counts = counts[0, :N_EXPERTS].astype(jnp.int32)
    ends = jnp.cumsum(counts)
    starts = ends - counts
    e = route[:, 0:2].astype(jnp.int32)
    seg_start = jnp.sum(jnp.where(e[..., None] == jnp.arange(N_EXPERTS), starts, 0), axis=-1)
    dest = seg_start + route[:, 4:6].astype(jnp.int32)
    n_blocks = n_rows // bm
    cuts = jnp.sort(jnp.concatenate([jnp.arange(n_blocks, dtype=jnp.int32) * bm, starts[1:]]))
    lo = cuts
    hi = jnp.concatenate([cuts[1:], jnp.array([n_rows], jnp.int32)])
    blk = jnp.minimum(lo // bm, n_blocks - 1)
    ex = jnp.minimum(jnp.sum(ends[None, :] <= lo[:, None], axis=1), N_EXPERTS - 1).astype(jnp.int32)
    first = jnp.concatenate([jnp.ones((1,), jnp.int32), (blk[1:] != blk[:-1]).astype(jnp.int32)])
    return dest, (blk, ex, lo, hi, first)


def _layer_weights(w_in, cmp_pos, cmp_w1, cmp_w2):
    w_main = jnp.concatenate([w_in[:, _Q0:_GL0], w_in[:, _U0:_MG0]], axis=1).astype(BF16)
    w_mg = w_in[:, _MG0:].astype(BF16)
    gl = w_in[:, _GL0:_U0].reshape(D_MODEL, KV_GROUPS, _GATES_PER_GROUP)
    w_gl = jnp.pad(gl, ((0, 0), (0, 0), (0, LANES - _GATES_PER_GROUP))).reshape(D_MODEL, KV_GROUPS * LANES)
    half = CMP_STRIDE * HEAD_DIM
    pos = cmp_pos.reshape(2, 2, half)
    w1 = cmp_w1.reshape(2, 2, half, CMP_HIDDEN).astype(BF16)
    return w_main, w_mg, w_gl.astype(BF16), pos, w1, cmp_w2.astype(BF16)


def _cmp_to_sel_map(S):
    n_blk = S // SEL_BLOCK
    n_cmp = (S - CMP_LEN) // CMP_STRIDE + 1
    nck = S // CMP_STRIDE
    cs = np.arange(nck) * CMP_STRIDE
    bs = np.arange(n_blk) * SEL_BLOCK
    overlap = (cs[None, :] <= bs[:, None] + SEL_BLOCK - 1) & (cs[None, :] + CMP_LEN - 1 >= bs[:, None])
    overlap &= (np.arange(nck) < n_cmp)[None, :]
    return jnp.asarray(overlap.astype(np.float32))


def kernel(x, norm_mix_g, w_in, cmp_pos, cmp_w1, cmp_w2, w_nsa_up, pool_w, pool_scale, w_pool_up, w_out,
           norm_ffn_g, ffn_w_gate, ffn_w_up, ffn_w_down, moe_w_router, moe_b_router, moe_w_gate, moe_w_up,
           moe_w_down, final_norm_g):
    B, S, D = x.shape
    T = B * S
    depth = w_in.shape[0]
    cmp_map = _cmp_to_sel_map(S)
    attn_tq, attn_tk = 256, 512
    attn_tables = _attention_tables(S, attn_tq, attn_tk)
    h = x.reshape(T, D)
    bm = 1024
    for i in range(depth):
        w_main, w_mg, w_gl, pos, w1, w2 = _layer_weights(w_in[i], cmp_pos[i], cmp_w1[i], cmp_w2[i])
        q8, xc, kwin, ksx, vx, u, gls = _in_proj(h, norm_mix_g[i][None], w_main, w_gl, S)
        kc, vc = _compress(xc, pos, w1, w2, B, S)
        ocmp, unsel = _cmp_select(q8, kc, vc, cmp_map, B, S)
        onsa = _sparse_attention(q8, unsel, ksx, kwin, vx, ocmp, gls, attn_tables, B, S, attn_tq, attn_tk)
        mix_w = (norm_mix_g[i][None], w_mg, pool_w[i].astype(BF16), pool_scale[i][None], w_nsa_up[i].astype(BF16),
                 w_pool_up[i].astype(BF16), w_out[i].astype(BF16), norm_ffn_g[i][None])
        j = i // 2
        if i % 2 == 0:
            ffn = (ffn_w_gate[j].astype(BF16), ffn_w_up[j].astype(BF16), ffn_w_down[j].astype(BF16))
            h = _post_mixer(onsa, u, h, mix_w, S, ffn=ffn)
        else:
            w_router = jnp.pad(moe_w_router[j], ((0, 0), (0, LANES - N_EXPERTS)))
            w_router_hi = w_router.astype(BF16)
            w_router_lo = (w_router - w_router_hi.astype(F32)).astype(BF16)
            w_router = jnp.concatenate([w_router_hi, w_router_lo], axis=1)
            b_router = jnp.pad(moe_b_router[j], (0, LANES - N_EXPERTS))[None]
            h1, f, route, counts = _post_mixer(onsa, u, h, mix_w, S, router=(w_router, b_router))
            dest, visits = _routing_tables(route, counts, 2 * T, bm)
            x_sorted = _dispatch(f, dest)
            y_sorted = _grouped_ffn(x_sorted, visits, moe_w_gate, moe_w_up, moe_w_down, j, bm=bm)
            z = _undispatch(y_sorted, dest)
            h = _combine(h1, route, z, final_norm_g[None], final_norm=(i == depth - 1))
    return h.reshape(B, S, D)
```

```python
import functools

import numpy as np
import jax
import jax.numpy as jnp
from jax import lax
from jax.experimental import pallas as pl
from jax.experimental.pallas import tpu as pltpu
from jax.experimental.pallas import tpu_sc as plsc

D_MODEL = 1024
NSA_HEADS = 8
HEAD_DIM = 64
KV_GROUPS = 2
HEADS_PER_GROUP = NSA_HEADS // KV_GROUPS
NSA_WIDTH = NSA_HEADS * HEAD_DIM
KV_WIDTH = KV_GROUPS * HEAD_DIM
CMP_LEN = 32
CMP_STRIDE = 16
CMP_HIDDEN = 256
SEL_BLOCK = 64
SEL_TOPN = 8
WINDOW = 512
ATTN_SCALE = HEAD_DIM ** -0.5
POOL_WINDOWS = (2, 4, 8, 16)
POOL_GROUP_WIDTH = 128
POOL_WIDTH = 512
N_EXPERTS = 8
D_FF_EXPERT = 3584
RMS_EPS = 1e-6
NEG_INF = -1e30
FORCE = 1e9

LOG2E = 1.4426950408889634
MASK_BIG = 2.0 ** 100

LANES = 128
VMEM_LIMIT = 56 * 1024 * 1024

BF16 = jnp.bfloat16
F32 = jnp.float32

_Q0 = 0
_KV0 = NSA_WIDTH
_GL0 = _KV0 + 6 * KV_WIDTH
_U0 = _GL0 + 3 * NSA_HEADS
_MG0 = _U0 + POOL_WIDTH
_MAIN_COLS = NSA_WIDTH + 6 * KV_WIDTH + POOL_WIDTH
_GATES_PER_GROUP = 3 * HEADS_PER_GROUP


def _dot(a, b, **kw):
    return jnp.dot(a, b, preferred_element_type=F32, **kw)


def _dot_tn(a, b):
    return lax.dot_general(a, b, (((0,), (0,)), ((), ())), preferred_element_type=F32)


def _dot_nt(a, b):
    return lax.dot_general(a, b, (((1,), (1,)), ((), ())), preferred_element_type=F32)


def _rms(x, g):
    return x * lax.rsqrt(jnp.mean(x * x, axis=-1, keepdims=True) + RMS_EPS) * g


def _params(*sem):
    return pltpu.CompilerParams(dimension_semantics=sem, vmem_limit_bytes=VMEM_LIMIT)


def _const_spec(shape):
    nd = len(shape)
    return pl.BlockSpec(shape, lambda *_: (0,) * nd, pipeline_mode=pl.Buffered(1))


def _in_kernel(h_ref, g_ref, w_ref, wgl_ref, chunk_ref, q_ref, xc_ref, kw_ref, ksx_ref, vx_ref, u_ref, gl_ref, *, S):
    a = _rms(h_ref[...], g_ref[...]).astype(BF16)
    tm = a.shape[0]
    zeros = jnp.zeros((tm, HEAD_DIM), F32)
    ones = jnp.ones((tm, HEAD_DIM), F32)
    pos = (pl.program_id(0) * tm + lax.broadcasted_iota(jnp.int32, (tm, HEAD_DIM), 0)) % S
    lane = lax.broadcasted_iota(jnp.int32, (tm, HEAD_DIM), 1)
    block_code = jnp.where(pos // SEL_BLOCK == lane, -MASK_BIG, 0.0)

    def wide(lo, hi):
        return jnp.concatenate([lo, hi], axis=-1).astype(BF16)

    q = _dot(a, w_ref[:, 0:NSA_WIDTH]) * (ATTN_SCALE * LOG2E)
    for h in range(NSA_HEADS):
        q_ref[h] = wide(q[:, h * HEAD_DIM:(h + 1) * HEAD_DIM], zeros)
    for c in range(3):
        c0 = NSA_WIDTH + c * 256
        kv = _dot(a, w_ref[:, c0:c0 + 256])
        pieces = [kv[:, i * HEAD_DIM:(i + 1) * HEAD_DIM] for i in range(4)]
        if c == 0:
            n_chunks = tm // CMP_STRIDE
            for i in range(4):
                by_offset = _dot(chunk_ref[...], pieces[i].astype(BF16))
                for l in range(CMP_STRIDE):
                    xc_ref[i, :, l * HEAD_DIM:(l + 1) * HEAD_DIM] = (
                        by_offset[l * n_chunks:(l + 1) * n_chunks].astype(BF16))
            continue
        for g in range(KV_GROUPS):
            if c == 1:
                ksx_ref[g] = wide(pieces[g], block_code)
            else:
                kw_ref[g] = pieces[g].astype(BF16)
            vx_ref[2 * (c - 1) + g] = wide(pieces[2 + g], ones)
    u0 = NSA_WIDTH + 6 * KV_WIDTH
    u_ref[...] = _dot(a, w_ref[:, u0:u0 + POOL_WIDTH])
    gl = jax.nn.sigmoid(_dot(a, wgl_ref[...]))
    for g in range(KV_GROUPS):
        gl_ref[g] = gl[:, g * LANES:(g + 1) * LANES]


def _in_proj(h, g, w_main, w_gl, S, tm=512):
    T = h.shape[0]
    return pl.pallas_call(
        functools.partial(_in_kernel, S=S),
        grid=(T // tm,),
        in_specs=[
            pl.BlockSpec((tm, D_MODEL), lambda i: (i, 0)),
            _const_spec((1, D_MODEL)),
            _const_spec((D_MODEL, _MAIN_COLS)),
            _const_spec((D_MODEL, KV_GROUPS * LANES)),
            _const_spec((tm, tm)),
        ],
        out_specs=[
            pl.BlockSpec((NSA_HEADS, tm, LANES), lambda i: (0, i, 0)),
            pl.BlockSpec((4, tm // CMP_STRIDE, CMP_STRIDE * HEAD_DIM), lambda i: (0, i, 0)),
            pl.BlockSpec((KV_GROUPS, tm, HEAD_DIM), lambda i: (0, i, 0)),
            pl.BlockSpec((KV_GROUPS, tm, LANES), lambda i: (0, i, 0)),
            pl.BlockSpec((4, tm, LANES), lambda i: (0, i, 0)),
            pl.BlockSpec((tm, POOL_WIDTH), lambda i: (i, 0)),
            pl.BlockSpec((KV_GROUPS, tm, LANES), lambda i: (0, i, 0)),
        ],
        out_shape=[
            jax.ShapeDtypeStruct((NSA_HEADS, T, LANES), BF16),
            jax.ShapeDtypeStruct((4, T // CMP_STRIDE, CMP_STRIDE * HEAD_DIM), BF16),
            jax.ShapeDtypeStruct((KV_GROUPS, T, HEAD_DIM), BF16),
            jax.ShapeDtypeStruct((KV_GROUPS, T, LANES), BF16),
            jax.ShapeDtypeStruct((4, T, LANES), BF16),
            jax.ShapeDtypeStruct((T, POOL_WIDTH), F32),
            jax.ShapeDtypeStruct((KV_GROUPS, T, LANES), F32),
        ],
        compiler_params=_params("parallel"),
        name="in_proj",
    )(h, g, w_main, w_gl, _chunk_permutation(tm))


def _chunk_permutation(tm):
    n_chunks = tm // CMP_STRIDE
    perm = np.zeros((tm, tm), np.float32)
    l, n = np.meshgrid(np.arange(CMP_STRIDE), np.arange(n_chunks), indexing="ij")
    perm[(l * n_chunks + n).ravel(), (CMP_STRIDE * n + l).ravel()] = 1.0
    return jnp.asarray(perm, dtype=BF16)


def _cmp_kernel(x_ref, pos_ref, w1_ref, w2_ref, kc_ref, vc_ref):
    n_chunks = x_ref.shape[1]
    for j in range(2):
        out_ref = kc_ref if j == 0 else vc_ref
        for g in range(KV_GROUPS):
            x = x_ref[j * KV_GROUPS + g].astype(F32)
            lo = (x + pos_ref[j, 0:1, :]).astype(BF16)
            hi = (x + pos_ref[j, 1:2, :]).astype(BF16)
            hid = _dot(lo, w1_ref[j, 0]) + pltpu.roll(_dot(hi, w1_ref[j, 1]), n_chunks - 1, axis=0)
            act = jax.nn.gelu(hid).astype(BF16)
            out_ref[0, g] = _dot(act, w2_ref[j]).astype(BF16)


def _compress(x, pos, w1, w2, B, S):
    nck = S // CMP_STRIDE
    half = CMP_STRIDE * HEAD_DIM
    out = jax.ShapeDtypeStruct((B, KV_GROUPS, nck, HEAD_DIM), BF16)
    return pl.pallas_call(
        _cmp_kernel,
        grid=(B,),
        in_specs=[
            pl.BlockSpec((4, nck, half), lambda b: (0, b, 0)),
            _const_spec((2, 2, half)),
            _const_spec((2, 2, half, CMP_HIDDEN)),
            _const_spec((2, CMP_HIDDEN, HEAD_DIM)),
        ],
        out_specs=[pl.BlockSpec((1, KV_GROUPS, nck, HEAD_DIM), lambda b: (b, 0, 0, 0))] * 2,
        out_shape=[out, out],
        compiler_params=_params("parallel"),
        name="compress",
    )(x, pos, w1, w2)


def _sel_kernel(q_ref, kc_ref, vc_ref, map_ref, ocmp_ref, unsel_ref, *, tq, n_blk):
    i = pl.program_id(2)
    kc = kc_ref[0, 0]
    vc = vc_ref[0, 0]
    nck = kc.shape[0]
    n_idx = lax.broadcasted_iota(jnp.int32, (nck, tq), 0)
    t_idx = i * tq + lax.broadcasted_iota(jnp.int32, (nck, tq), 1)
    valid = n_idx * CMP_STRIDE + (CMP_LEN - 1) <= t_idx
    sees_any = t_idx[0:1, :] >= CMP_LEN - 1
    psum = jnp.zeros((nck, tq), F32)
    for r in range(HEADS_PER_GROUP):
        s = jnp.where(valid, _dot_nt(kc, q_ref[r][:, :HEAD_DIM]), NEG_INF)
        e = jnp.exp2(s - jnp.max(s, axis=0, keepdims=True))
        p = e * jnp.where(sees_any, 1.0 / jnp.sum(e, axis=0, keepdims=True), 0.0)
        ocmp_ref[r] = _dot_tn(p.astype(BF16), vc)
        psum = psum + p
    imp = _dot(map_ref[...], psum, precision=lax.Precision.HIGHEST)
    blk = lax.broadcasted_iota(jnp.int32, (n_blk, tq), 0)
    t_blk = i * tq + lax.broadcasted_iota(jnp.int32, (n_blk, tq), 1)
    cur = lax.shift_right_logical(t_blk, SEL_BLOCK.bit_length() - 1)
    imp = jnp.where(blk > cur, -FORCE, imp)
    imp = jnp.where((blk == 0) | (blk == cur), FORCE, imp)
    rank = jnp.zeros((n_blk, tq), F32)
    for j in range(n_blk):
        row = imp[j:j + 1, :]
        ahead = (row > imp) | ((row == imp) & (blk > j))
        rank = rank + jnp.where(ahead, 1.0, 0.0)
    unsel = jnp.where(rank < SEL_TOPN, 0.0, 1.0)
    table = jnp.concatenate([jnp.zeros((HEAD_DIM, tq), F32), unsel,
                             jnp.zeros((LANES - HEAD_DIM - n_blk, tq), F32)], axis=0).astype(BF16)
    eye = jnp.where(lax.broadcasted_iota(jnp.int32, (LANES, LANES), 0)
                    == lax.broadcasted_iota(jnp.int32, (LANES, LANES), 1), 1.0, 0.0).astype(BF16)
    unsel_ref[0] = _dot_tn(table, eye).astype(BF16)


def _cmp_select(q8, kc, vc, cmp_map, B, S, tq=512):
    T = B * S
    n_blk = S // SEL_BLOCK
    nck = S // CMP_STRIDE
    nq = S // tq
    hp = HEADS_PER_GROUP
    return pl.pallas_call(
        functools.partial(_sel_kernel, tq=tq, n_blk=n_blk),
        grid=(B, KV_GROUPS, nq),
        in_specs=[
            pl.BlockSpec((hp, tq, LANES), lambda b, g, i: (g, b * nq + i, 0)),
            pl.BlockSpec((1, 1, nck, HEAD_DIM), lambda b, g, i: (b, g, 0, 0)),
            pl.BlockSpec((1, 1, nck, HEAD_DIM), lambda b, g, i: (b, g, 0, 0)),
            _const_spec((n_blk, nck)),
        ],
        out_specs=[
            pl.BlockSpec((hp, tq, HEAD_DIM), lambda b, g, i: (g, b * nq + i, 0)),
            pl.BlockSpec((1, tq, LANES), lambda b, g, i: (g, b * nq + i, 0)),
        ],
        out_shape=[
            jax.ShapeDtypeStruct((NSA_HEADS, T, HEAD_DIM), F32),
            jax.ShapeDtypeStruct((KV_GROUPS, T, LANES), BF16),
        ],
        compiler_params=_params("parallel", "parallel", "parallel"),
        name="cmp_select",
    )(q8, kc, vc, cmp_map)


def _attn_kernel(q_ref, unsel_ref, ks_ref, vs_ref, kw_ref, vw_ref, ocmp_ref, gl_ref, diag_ref, wint_ref,
                 o_ref, *, tq, tk, S):
    i = pl.program_id(2)
    hp = HEADS_PER_GROUP
    rows = hp * tq
    span = WINDOW + tq

    def sel_tile(q4, k0, width, causal=None, state=None):
        s = _dot_nt(q4, ks_ref[0, k0:k0 + width, :])
        if causal is not None:
            s = (s.reshape(hp, tq, width) + causal[None]).reshape(rows, width)
        m_new = jnp.broadcast_to(jnp.max(s, axis=-1, keepdims=True), (rows, LANES))
        if state is not None:
            m_new = jnp.maximum(state[0], m_new)
        p = jnp.exp2(s - jnp.concatenate([m_new] * (width // LANES), axis=1))
        pv = _dot(p.astype(BF16), vs_ref[0, k0:k0 + width, :])
        return m_new, pv if state is None else jnp.exp2(state[0] - m_new) * state[1] + pv

    def step(n_full):
        q4 = (q_ref[...] + unsel_ref[...]).reshape(rows, LANES)

        w0 = pl.multiple_of(jnp.clip(i * tq - WINDOW, 0, S - span), LANES)
        kw = kw_ref[0, pl.ds(w0, span), :]
        vw = vw_ref[0, pl.ds(w0, span), :]
        s = (_dot_nt(q4[:, :HEAD_DIM], kw).reshape(hp, tq, span) + wint_ref[0][None]).reshape(rows, span)
        p = jnp.exp2(s - jnp.max(s, axis=-1, keepdims=True))
        accw = _dot(p.astype(BF16), vw)
        o_win = accw[:, :HEAD_DIM] / accw[:, HEAD_DIM:HEAD_DIM + 1]
        gl = gl_ref[0]
        gated = [gl[:, 3 * r:3 * r + 1] * ocmp_ref[r] + gl[:, 3 * r + 2:3 * r + 3] * o_win[r * tq:(r + 1) * tq]
                 for r in range(hp)]

        state = sel_tile(q4, n_full * tk, tk, causal=diag_ref[0])
        for j in range(n_full // 2):
            state = sel_tile(q4, j * 2 * tk, 2 * tk, state=state)
        if n_full % 2:
            state = sel_tile(q4, (n_full - 1) * tk, tk, state=state)
        acc = state[1]
        o_sel = acc[:, :HEAD_DIM] / acc[:, HEAD_DIM:HEAD_DIM + 1]

        outs = [gated[r] + gl[:, 3 * r + 1:3 * r + 2] * o_sel[r * tq:(r + 1) * tq] for r in range(hp)]
        o_ref[...] = jnp.concatenate(outs, axis=-1).astype(BF16)

    for n_full in range(((S - tq) // tk) + 1):
        pl.when((i * tq) // tk == n_full)(functools.partial(step, n_full))


def _attention_tables(S, tq, tk):
    a = np.arange(tq)[:, None]
    c = np.arange(tk)[None, :]
    diag = np.stack([np.where(c <= p * tq + a, 0.0, -MASK_BIG) for p in range(tk // tq)])
    span = WINDOW + tq
    win = []
    for i in range(WINDOW // tq + 1):
        t = i * tq + a
        kp = min(max(i * tq - WINDOW, 0), S - span) + np.arange(span)[None, :]
        win.append(np.where((kp <= t) & (kp > t - WINDOW), 0.0, -MASK_BIG))
    return jnp.asarray(diag.astype(np.float32)), jnp.asarray(np.stack(win).astype(np.float32))


def _sparse_attention(q8, unsel, ksx, kwin, vx, ocmp, gls, tables, B, S, tq, tk):
    diag, win = tables
    T = B * S
    nq = S // tq
    hp = HEADS_PER_GROUP
    span = WINDOW + tq
    n_win = WINDOW // tq
    q_tile = lambda width: pl.BlockSpec((hp, tq, width), lambda b, g, i: (g, b * nq + i, 0))
    group_tile = pl.BlockSpec((1, tq, LANES), lambda b, g, i: (g, b * nq + i, 0))
    seq = lambda first, width: pl.BlockSpec((1, S, width), lambda b, g, i: (first + g, b, 0))
    return pl.pallas_call(
        functools.partial(_attn_kernel, tq=tq, tk=tk, S=S),
        grid=(B, KV_GROUPS, nq),
        in_specs=[
            q_tile(LANES), group_tile,
            seq(0, LANES), seq(0, LANES), seq(0, HEAD_DIM), seq(2, LANES),
            q_tile(HEAD_DIM), group_tile,
            pl.BlockSpec((1, tq, tk), lambda b, g, i: (i % (tk // tq), 0, 0)),
            pl.BlockSpec((1, tq, span), lambda b, g, i: (jnp.minimum(i, n_win), 0, 0)),
        ],
        out_specs=pl.BlockSpec((tq, hp * HEAD_DIM), lambda b, g, i: (b * nq + i, g)),
        out_shape=jax.ShapeDtypeStruct((T, NSA_WIDTH), BF16),
        compiler_params=_params("parallel", "parallel", "parallel"),
        name="sparse_attention",
    )(q8, unsel, ksx, vx, kwin, vx, ocmp, gls, diag, win)


_POOL_HALO = 16


def _pool_mixer(u_ref, halo_ref, w_ref, scale_ref, tm, S):
    pos0 = (pl.program_id(0) * tm) % S
    halo = jnp.where(pos0 == 0, 0.0, halo_ref[...])
    x = jnp.concatenate([halo, u_ref[...]], axis=0)
    pos = pos0 + lax.broadcasted_iota(jnp.int32, (tm, 1), 0)
    out = []
    for gi, w in enumerate(POOL_WINDOWS):
        c0 = gi * POOL_GROUP_WIDTH
        xg = x[:, c0:c0 + POOL_GROUP_WIDTH]
        tot = xg[_POOL_HALO:_POOL_HALO + tm]
        for d in range(1, w):
            tot = tot + xg[_POOL_HALO - d:_POOL_HALO - d + tm]
        cnt = jnp.minimum(pos + 1, w).astype(F32)
        pooled = tot / cnt - xg[_POOL_HALO:_POOL_HALO + tm]
        mixed = _dot(pooled.astype(BF16), w_ref[gi])
        out.append((mixed * scale_ref[:, c0:c0 + POOL_GROUP_WIDTH]).astype(BF16))
    return jnp.concatenate(out, axis=-1)


_N_MERGE_REFS = 11


def _merge_residual(onsa_ref, u_ref, halo_ref, h_ref, gmix_ref, wmg_ref, poolw_ref, pscale_ref, wn_ref, wp_ref,
                    wo_ref, *, tm, S):
    h = h_ref[...]
    a = _rms(h, gmix_ref[...]).astype(BF16)
    nsa_up = _dot(onsa_ref[...], wn_ref[...])
    pool_up = _dot(_pool_mixer(u_ref, halo_ref, poolw_ref, pscale_ref, tm, S), wp_ref[...])
    merged = (jax.nn.sigmoid(_dot(a, wmg_ref[:, :D_MODEL])) * nsa_up
              + jax.nn.sigmoid(_dot(a, wmg_ref[:, D_MODEL:])) * pool_up)
    return h + _dot(merged.astype(BF16), wo_ref[...])


def _post_dense_kernel(*refs, tm, S, fc):
    mix_refs = refs[:_N_MERGE_REFS]
    gffn_ref, wg_ref, wu_ref, wd_ref, o_ref = refs[_N_MERGE_REFS:]
    h1 = _merge_residual(*mix_refs, tm=tm, S=S)
    f = _rms(h1, gffn_ref[...]).astype(BF16)
    o_ref[...] = h1
    for c in range(wg_ref.shape[1] // fc):
        g = _dot(f, wg_ref[:, c * fc:(c + 1) * fc])
        u = _dot(f, wu_ref[:, c * fc:(c + 1) * fc])
        o_ref[...] += _dot((jax.nn.silu(g) * u).astype(BF16), wd_ref[c * fc:(c + 1) * fc, :])


def _post_moe_kernel(*refs, tm, S):
    mix_refs = refs[:_N_MERGE_REFS]
    g_ref, wr_ref, br_ref, h1_ref, f_ref, route_ref, cnt_ref, base_sc = refs[_N_MERGE_REFS:]

    @pl.when(pl.program_id(0) == 0)
    def _():
        base_sc[...] = jnp.zeros(base_sc.shape, F32)

    h1 = _merge_residual(*mix_refs, tm=tm, S=S)
    h1_ref[...] = h1
    f = _rms(h1, g_ref[...])
    width = f_ref.shape[2]
    for p in range(f_ref.shape[0]):
        f_ref[p] = _pack_bf16_pairs(f[:, 2 * p * width:(2 * p + 1) * width],
                                    f[:, (2 * p + 1) * width:(2 * p + 2) * width])

    lane = lax.broadcasted_iota(jnp.int32, (tm, LANES), 1)
    f_hi = f.astype(BF16)
    f_lo = (f - f_hi.astype(F32)).astype(BF16)
    hi_terms = _dot(f_hi, wr_ref[...])
    logits = hi_terms[:, :LANES] + hi_terms[:, LANES:] + _dot(f_lo, wr_ref[:, :LANES]) + br_ref[...]
    lowest = jnp.finfo(F32).min
    lane_f = lane.astype(F32)
    lg = jnp.where(lane < N_EXPERTS, logits, lowest)
    m1 = jnp.max(lg, axis=-1, keepdims=True)
    i1 = jnp.min(jnp.where(lg == m1, lane_f, float(LANES)), axis=-1, keepdims=True)
    lg2 = jnp.where(lane_f == i1, lowest, lg)
    m2 = jnp.max(lg2, axis=-1, keepdims=True)
    i2 = jnp.min(jnp.where(lg2 == m2, lane_f, float(LANES)), axis=-1, keepdims=True)
    e2 = jnp.exp(m2 - m1)
    gate1 = 1.0 / (1.0 + e2)
    gate2 = e2 / (1.0 + e2)

    oh1 = lane_f == i1
    oh2 = lane_f == i2
    both = jnp.where(oh1 | oh2, 1.0, 0.0)
    r_i = lax.broadcasted_iota(jnp.int32, (tm, tm), 0)
    c_i = lax.broadcasted_iota(jnp.int32, (tm, tm), 1)
    before = jnp.where(r_i > c_i, 1.0, 0.0).astype(BF16)
    seen = _dot(before, both.astype(BF16)) + base_sc[...]
    rank1 = jnp.sum(jnp.where(oh1, seen, 0.0), axis=-1, keepdims=True)
    rank2 = jnp.sum(jnp.where(oh2, seen, 0.0), axis=-1, keepdims=True)
    base_sc[...] = base_sc[...] + jnp.sum(both, axis=0, keepdims=True)
    cnt_ref[...] = base_sc[...]

    route = jnp.zeros((tm, LANES), F32)
    for col, val in enumerate((i1, i2, gate1, gate2, rank1, rank2)):
        route = jnp.where(lane == col, val, route)
    route_ref[...] = route


def _post_mixer(onsa, u, h, mix_w, S, ffn=None, router=None, tm=512, fc=256):
    T = h.shape[0]
    g_mix, w_mg, pool_w, pool_scale, wn, wp, wo, g_ffn = mix_w
    tile = lambda w: pl.BlockSpec((tm, w), lambda i: (i, 0))
    halo_blocks = tm // _POOL_HALO
    in_specs = [
        tile(NSA_WIDTH), tile(POOL_WIDTH),
        pl.BlockSpec((_POOL_HALO, POOL_WIDTH), lambda i: (jnp.maximum(i * halo_blocks - 1, 0), 0)),
        tile(D_MODEL), _const_spec((1, D_MODEL)), _const_spec((D_MODEL, 2 * D_MODEL)),
        _const_spec((len(POOL_WINDOWS), POOL_GROUP_WIDTH, POOL_GROUP_WIDTH)), _const_spec((1, POOL_WIDTH)),
        _const_spec((NSA_WIDTH, D_MODEL)), _const_spec((POOL_WIDTH, D_MODEL)),
        _const_spec((D_MODEL, D_MODEL)), _const_spec((1, D_MODEL)),
    ]
    args = [onsa, u, u, h, g_mix, w_mg, pool_w, pool_scale, wn, wp, wo, g_ffn]
    if router is None:
        wg, wu, wd = ffn
        d_ff = wg.shape[1]
        return pl.pallas_call(
            functools.partial(_post_dense_kernel, tm=tm, S=S, fc=fc),
            grid=(T // tm,),
            in_specs=in_specs + [_const_spec((D_MODEL, d_ff)), _const_spec((D_MODEL, d_ff)),
                                 _const_spec((d_ff, D_MODEL))],
            out_specs=tile(D_MODEL),
            out_shape=jax.ShapeDtypeStruct((T, D_MODEL), F32),
            compiler_params=_params("parallel"),
            name="post_mixer_dense",
        )(*args, wg, wu, wd)
    w_router, b_router = router
    return pl.pallas_call(
        functools.partial(_post_moe_kernel, tm=tm, S=S),
        grid=(T // tm,),
        in_specs=in_specs + [_const_spec((D_MODEL, 2 * LANES)), _const_spec((1, LANES))],
        out_specs=[tile(D_MODEL), pl.BlockSpec((N_PLANES, tm, PLANE_WIDTH), lambda i: (0, i, 0)), tile(LANES),
                   pl.BlockSpec((1, LANES), lambda i: (0, 0))],
        out_shape=[
            jax.ShapeDtypeStruct((T, D_MODEL), F32),
            jax.ShapeDtypeStruct((N_PLANES, T, PLANE_WIDTH), jnp.uint32),
            jax.ShapeDtypeStruct((T, LANES), F32),
            jax.ShapeDtypeStruct((1, LANES), F32),
        ],
        scratch_shapes=[pltpu.VMEM((1, LANES), F32)],
        compiler_params=_params("arbitrary"),
        name="post_mixer_moe",
    )(*args, w_router, b_router)


PLANE_WIDTH = 256
N_PLANES = D_MODEL // (2 * PLANE_WIDTH)
SC_WINDOW = 128


def _sc_mesh():
    return plsc.VectorSubcoreMesh(core_axis_name="core", subcore_axis_name="subcore")


def _pack_bf16_pairs(lo, hi):
    lo_bits = lax.bitcast_convert_type(lo.astype(BF16).astype(F32), jnp.uint32)
    hi_bits = lax.bitcast_convert_type(hi.astype(BF16).astype(F32), jnp.uint32)
    return (lo_bits >> 16) | (hi_bits & jnp.uint32(0xFFFF0000))


def _unpack_bf16_pairs(words):
    lo = lax.bitcast_convert_type(words << 16, F32)
    hi = lax.bitcast_convert_type(words & jnp.uint32(0xFFFF0000), F32)
    return lo, hi


def _sc_scatter_pair(rows, idx_a, idx_b, n_out):
    n_rows, width = rows.shape

    @pl.kernel(out_type=jax.ShapeDtypeStruct((n_out, width), rows.dtype), mesh=_sc_mesh(), scratch_types=[])
    def scatter(rows_hbm, a_hbm, b_hbm, out_hbm):
        def body(rows_vmem, a_vmem, b_vmem):
            pltpu.sync_copy(rows_vmem, out_hbm.at[a_vmem.at[0]])
            pltpu.sync_copy(rows_vmem, out_hbm.at[b_vmem.at[0]])

        idx_spec = pl.BlockSpec((1, SC_WINDOW), lambda i: (0, i))
        pltpu.emit_pipeline(
            body,
            grid=(n_rows // SC_WINDOW,),
            in_specs=[pl.BlockSpec((SC_WINDOW, width), lambda i: (i, 0)), idx_spec, idx_spec],
            out_specs=[],
            core_axis_name=("core", "subcore"),
            dimension_semantics=(pltpu.PARALLEL,),
        )(rows_hbm, a_hbm, b_hbm)

    return scatter(rows, idx_a, idx_b)


def _sc_gather(rows, idx):
    n_out = idx.shape[1]
    width = rows.shape[1]

    @pl.kernel(out_type=jax.ShapeDtypeStruct((n_out, width), rows.dtype), mesh=_sc_mesh(), scratch_types=[])
    def gather(rows_hbm, idx_hbm, out_hbm):
        def body(idx_vmem, out_vmem):
            pltpu.sync_copy(rows_hbm.at[idx_vmem.at[0]], out_vmem)

        pltpu.emit_pipeline(
            body,
            grid=(n_out // SC_WINDOW,),
            in_specs=[pl.BlockSpec((1, SC_WINDOW), lambda i: (0, i))],
            out_specs=[pl.BlockSpec((SC_WINDOW, width), lambda i: (i, 0))],
            core_axis_name=("core", "subcore"),
            dimension_semantics=(pltpu.PARALLEL,),
        )(idx_hbm, out_hbm)

    return gather(rows, idx)


def _dispatch(f_planes, dest):
    n_planes, T, width = f_planes.shape
    n_rows = 2 * T
    plane_base = (jnp.arange(n_planes, dtype=jnp.int32) * n_rows)[:, None]
    idx = [(plane_base + dest[:, k][None, :]).reshape(1, n_planes * T) for k in range(2)]
    out = _sc_scatter_pair(f_planes.reshape(n_planes * T, width), idx[0], idx[1], n_planes * n_rows)
    return out.reshape(n_planes, n_rows, width)


def _undispatch(y_planes, dest):
    n_planes, n_rows, width = y_planes.shape
    T = dest.shape[0]
    plane_base = (jnp.arange(n_planes, dtype=jnp.int32) * n_rows)[None, :, None]
    idx = (plane_base + dest.T[:, None, :]).reshape(1, 2 * n_planes * T)
    out = _sc_gather(y_planes.reshape(n_planes * n_rows, width), idx)
    return out.reshape(2, n_planes, T, width)


def _combine_kernel(h_ref, route_ref, z_ref, g_ref, o_ref, *, final_norm):
    route = route_ref[...]
    z = [jnp.concatenate([half for p in range(z_ref.shape[1]) for half in _unpack_bf16_pairs(z_ref[k, p])], axis=-1)
         for k in range(2)]
    out = h_ref[...] + (route[:, 2:3] * z[0] + route[:, 3:4] * z[1])
    if final_norm:
        out = _rms(out, g_ref[...])
    o_ref[...] = out


def _combine(h1, route, z, g_final, final_norm, tg=512):
    T = h1.shape[0]
    return pl.pallas_call(
        functools.partial(_combine_kernel, final_norm=final_norm),
        grid=(T // tg,),
        in_specs=[
            pl.BlockSpec((tg, D_MODEL), lambda i: (i, 0)),
            pl.BlockSpec((tg, LANES), lambda i: (i, 0)),
            pl.BlockSpec((2, z.shape[1], tg, z.shape[3]), lambda i: (0, 0, i, 0)),
            _const_spec((1, D_MODEL)),
        ],
        out_specs=pl.BlockSpec((tg, D_MODEL), lambda i: (i, 0)),
        out_shape=jax.ShapeDtypeStruct((T, D_MODEL), F32),
        compiler_params=_params("parallel"),
        name="moe_combine",
    )(h1, route, z, g_final)


def _moe_kernel(blk_ref, exp_ref, lo_ref, hi_ref, first_ref, x_ref, wg_ref, wu_ref, wd_ref, y_ref, acc_sc, *, bm):
    v = pl.program_id(0)
    c = pl.program_id(1)

    @pl.when((first_ref[v] == 1) & (c == 0))
    def _():
        acc_sc[...] = jnp.zeros(acc_sc.shape, F32)

    lo = lo_ref[v]
    hi = hi_ref[v]

    @pl.when(hi > lo)
    def _():
        row = blk_ref[v] * bm + lax.broadcasted_iota(jnp.int32, (bm, 1), 0)
        x = jnp.concatenate([half for p in range(x_ref.shape[0]) for half in _unpack_bf16_pairs(x_ref[p])], axis=-1)
        x = jnp.where((row >= lo) & (row < hi), x, 0.0).astype(BF16)
        g = _dot(x, wg_ref[0].astype(BF16))
        u = _dot(x, wu_ref[0].astype(BF16))
        acc_sc[...] += _dot((jax.nn.silu(g) * u).astype(BF16), wd_ref[0].astype(BF16))

    @pl.when(c == pl.num_programs(1) - 1)
    def _():
        width = y_ref.shape[2]
        for p in range(y_ref.shape[0]):
            y_ref[p] = _pack_bf16_pairs(acc_sc[:, 2 * p * width:(2 * p + 1) * width],
                                        acc_sc[:, (2 * p + 1) * width:(2 * p + 2) * width])


def _grouped_ffn(x_sorted, visits, wg, wu, wd, layer, bm=1024, fc=512):
    n_planes, n_rows, width = x_sorted.shape
    n_visits = visits[0].shape[0]
    n_exp, _, d_ff = wg.shape[1:]
    first = layer * n_exp
    flat = lambda w: w.reshape((w.shape[0] * n_exp,) + w.shape[2:])
    rows_spec = pl.BlockSpec((n_planes, bm, width), lambda v, c, blk, ex, lo, hi, fi: (0, blk[v], 0))
    grid_spec = pltpu.PrefetchScalarGridSpec(
        num_scalar_prefetch=5,
        grid=(n_visits, d_ff // fc),
        in_specs=[
            rows_spec,
            pl.BlockSpec((1, D_MODEL, fc), lambda v, c, blk, ex, lo, hi, fi: (first + ex[v], 0, c)),
            pl.BlockSpec((1, D_MODEL, fc), lambda v, c, blk, ex, lo, hi, fi: (first + ex[v], 0, c)),
            pl.BlockSpec((1, fc, D_MODEL), lambda v, c, blk, ex, lo, hi, fi: (first + ex[v], c, 0)),
        ],
        out_specs=rows_spec,
        scratch_shapes=[pltpu.VMEM((bm, D_MODEL), F32)],
    )
    return pl.pallas_call(
        functools.partial(_moe_kernel, bm=bm),
        grid_spec=grid_spec,
        out_shape=jax.ShapeDtypeStruct((n_planes, n_rows, width), jnp.uint32),
        compiler_params=_params("arbitrary", "arbitrary"),
        name="grouped_ffn",
    )(*visits, x_sorted, flat(wg), flat(wu), flat(wd))


def _routing_tables(route, counts, n_rows, bm):
    counts = counts[0, :N_EXPERTS].astype(jnp.int32)
    ends = jnp.cumsum(counts)
    starts = ends - counts
    e = route[:, 0:2].astype(jnp.int32)
    seg_start = jnp.sum(jnp.where(e[..., None] == jnp.arange(N_EXPERTS), starts, 0), axis=-1)
    dest = seg_start + route[:, 4:6].astype(jnp.int32)
    n_blocks = n_rows // bm
    cuts = jnp.sort(jnp.concatenate([jnp.arange(n_blocks, dtype=jnp.int32) * bm, starts[1:]]))
    lo = cuts
    hi = jnp.concatenate([cuts[1:], jnp.array([n_rows], jnp.int32)])
    blk = jnp.minimum(lo // bm, n_blocks - 1)
    ex = jnp.minimum(jnp.sum(ends[None, :] <= lo[:, None], axis=1), N_EXPERTS - 1).astype(jnp.int32)
    first = jnp.concatenate([jnp.ones((1,), jnp.int32), (blk[1:] != blk[:-1]).astype(jnp.int32)])
    return dest, (blk, ex, lo, hi, first)


def _layer_weights(w_in, cmp_pos, cmp_w1, cmp_w2):
    w_main = jnp.concatenate([w_in[:, _Q0:_GL0], w_in[:, _U0:_MG0]], axis=1).astype(BF16)
    w_mg = w_in[:, _MG0:].astype(BF16)
    gl = w_in[:, _GL0:_U0].reshape(D_MODEL, KV_GROUPS, _GATES_PER_GROUP)
    w_gl = jnp.pad(gl, ((0, 0), (0, 0), (0, LANES - _GATES_PER_GROUP))).reshape(D_MODEL, KV_GROUPS * LANES)
    half = CMP_STRIDE * HEAD_DIM
    pos = cmp_pos.reshape(2, 2, half)
    w1 = cmp_w1.reshape(2, 2, half, CMP_HIDDEN).astype(BF16)
    return w_main, w_mg, w_gl.astype(BF16), pos, w1, cmp_w2.astype(BF16)


def _cmp_to_sel_map(S):
    n_blk = S // SEL_BLOCK
    n_cmp = (S - CMP_LEN) // CMP_STRIDE + 1
    nck = S // CMP_STRIDE
    cs = np.arange(nck) * CMP_STRIDE
    bs = np.arange(n_blk) * SEL_BLOCK
    overlap = (cs[None, :] <= bs[:, None] + SEL_BLOCK - 1) & (cs[None, :] + CMP_LEN - 1 >= bs[:, None])
    overlap &= (np.arange(nck) < n_cmp)[None, :]
    return jnp.asarray(overlap.astype(np.float32))


def kernel(x, norm_mix_g, w_in, cmp_pos, cmp_w1, cmp_w2, w_nsa_up, pool_w, pool_scale, w_pool_up, w_out,
           norm_ffn_g, ffn_w_gate, ffn_w_up, ffn_w_down, moe_w_router, moe_b_router, moe_w_gate, moe_w_up,
           moe_w_down, final_norm_g):
    B, S, D = x.shape
    T = B * S
    depth = w_in.shape[0]
    cmp_map = _cmp_to_sel_map(S)
    attn_tq, attn_tk = 256, 512
    attn_tables = _attention_tables(S, attn_tq, attn_tk)
    h = x.reshape(T, D)
    bm = 1024
    for i in range(depth):
        w_main, w_mg, w_gl, pos, w1, w2 = _layer_weights(w_in[i], cmp_pos[i], cmp_w1[i], cmp_w2[i])
        q8, xc, kwin, ksx, vx, u, gls = _in_proj(h, norm_mix_g[i][None], w_main, w_gl, S)
        kc, vc = _compress(xc, pos, w1, w2, B, S)
        ocmp, unsel = _cmp_select(q8, kc, vc, cmp_map, B, S)
        onsa = _sparse_attention(q8, unsel, ksx, kwin, vx, ocmp, gls, attn_tables, B, S, attn_tq, attn_tk)
        mix_w = (norm_mix_g[i][None], w_mg, pool_w[i].astype(BF16), pool_scale[i][None], w_nsa_up[i].astype(BF16),
                 w_pool_up[i].astype(BF16), w_out[i].astype(BF16), norm_ffn_g[i][None])
        j = i // 2
        if i % 2 == 0:
            ffn = (ffn_w_gate[j].astype(BF16), ffn_w_up[j].astype(BF16), ffn_w_down[j].astype(BF16))
            h = _post_mixer(onsa, u, h, mix_w, S, ffn=ffn)
        else:
            w_router = jnp.pad(moe_w_router[j], ((0, 0), (0, LANES - N_EXPERTS)))
            w_router_hi = w_router.astype(BF16)
            w_router_lo = (w_router - w_router_hi.astype(F32)).astype(BF16)
            w_router = jnp.concatenate([w_router_hi, w_router_lo], axis=1)
            b_router = jnp.pad(moe_b_router[j], (0, LANES - N_EXPERTS))[None]
            h1, f, route, counts = _post_mixer(onsa, u, h, mix_w, S, router=(w_router, b_router))
            dest, visits = _routing_tables(route, counts, 2 * T, bm)
            x_sorted = _dispatch(f, dest)
            y_sorted = _grouped_ffn(x_sorted, visits, moe_w_gate, moe_w_up, moe_w_down, j, bm=bm)
            z = _undispatch(y_sorted, dest)
            h = _combine(h1, route, z, final_norm_g[None], final_norm=(i == depth - 1))
    return h.reshape(B, S, D)
```

```python
import functools

import numpy as np
import jax
import jax.numpy as jnp
from jax import lax
from jax.experimental import pallas as pl
from jax.experimental.pallas import tpu as pltpu
from jax.experimental.pallas import tpu_sc as plsc

D_MODEL = 1024
NSA_HEADS = 8
HEAD_DIM = 64
KV_GROUPS = 2
HEADS_PER_GROUP = NSA_HEADS // KV_GROUPS
NSA_WIDTH = NSA_HEADS * HEAD_DIM
KV_WIDTH = KV_GROUPS * HEAD_DIM
CMP_LEN = 32
CMP_STRIDE = 16
CMP_HIDDEN = 256
SEL_BLOCK = 64
SEL_TOPN = 8
WINDOW = 512
ATTN_SCALE = HEAD_DIM ** -0.5
POOL_WINDOWS = (2, 4, 8, 16)
POOL_GROUP_WIDTH = 128
POOL_WIDTH = 512
N_EXPERTS = 8
D_FF_EXPERT = 3584
RMS_EPS = 1e-6
NEG_INF = -1e30
FORCE = 1e9

LOG2E = 1.4426950408889634
MASK_BIG = 2.0 ** 100

LANES = 128
VMEM_LIMIT = 56 * 1024 * 1024

BF16 = jnp.bfloat16
F32 = jnp.float32

_Q0 = 0
_KV0 = NSA_WIDTH
_GL0 = _KV0 + 6 * KV_WIDTH
_U0 = _GL0 + 3 * NSA_HEADS
_MG0 = _U0 + POOL_WIDTH
_MAIN_COLS = NSA_WIDTH + 6 * KV_WIDTH + POOL_WIDTH
_GATES_PER_GROUP = 3 * HEADS_PER_GROUP


def _dot(a, b, **kw):
    return jnp.dot(a, b, preferred_element_type=F32, **kw)


def _dot_tn(a, b):
    return lax.dot_general(a, b, (((0,), (0,)), ((), ())), preferred_element_type=F32)


def _dot_nt(a, b):
    return lax.dot_general(a, b, (((1,), (1,)), ((), ())), preferred_element_type=F32)


def _rms(x, g):
    return x * lax.rsqrt(jnp.mean(x * x, axis=-1, keepdims=True) + RMS_EPS) * g


def _params(*sem):
    return pltpu.CompilerParams(dimension_semantics=sem, vmem_limit_bytes=VMEM_LIMIT)


def _const_spec(shape):
    nd = len(shape)
    return pl.BlockSpec(shape, lambda *_: (0,) * nd, pipeline_mode=pl.Buffered(1))


def _in_kernel(h_ref, g_ref, w_ref, wgl_ref, chunk_ref, q_ref, xc_ref, kw_ref, ksx_ref, vx_ref, u_ref, gl_ref, *, S):
    a = _rms(h_ref[...], g_ref[...]).astype(BF16)
    tm = a.shape[0]
    zeros = jnp.zeros((tm, HEAD_DIM), F32)
    ones = jnp.ones((tm, HEAD_DIM), F32)
    pos = (pl.program_id(0) * tm + lax.broadcasted_iota(jnp.int32, (tm, HEAD_DIM), 0)) % S
    lane = lax.broadcasted_iota(jnp.int32, (tm, HEAD_DIM), 1)
    block_code = jnp.where(pos // SEL_BLOCK == lane, -MASK_BIG, 0.0)

    def wide(lo, hi):
        return jnp.concatenate([lo, hi], axis=-1).astype(BF16)

    q = _dot(a, w_ref[:, 0:NSA_WIDTH]) * (ATTN_SCALE * LOG2E)
    for h in range(NSA_HEADS):
        q_ref[h] = wide(q[:, h * HEAD_DIM:(h + 1) * HEAD_DIM], zeros)
    for c in range(3):
        c0 = NSA_WIDTH + c * 256
        kv = _dot(a, w_ref[:, c0:c0 + 256])
        pieces = [kv[:, i * HEAD_DIM:(i + 1) * HEAD_DIM] for i in range(4)]
        if c == 0:
            n_chunks = tm // CMP_STRIDE
            for i in range(4):
                by_offset = _dot(chunk_ref[...], pieces[i].astype(BF16))
                for l in range(CMP_STRIDE):
                    xc_ref[i, :, l * HEAD_DIM:(l + 1) * HEAD_DIM] = (
                        by_offset[l * n_chunks:(l + 1) * n_chunks].astype(BF16))
            continue
        for g in range(KV_GROUPS):
            if c == 1:
                ksx_ref[g] = wide(pieces[g], block_code)
            else:
                kw_ref[g] = pieces[g].astype(BF16)
            vx_ref[2 * (c - 1) + g] = wide(pieces[2 + g], ones)
    u0 = NSA_WIDTH + 6 * KV_WIDTH
    u_ref[...] = _dot(a, w_ref[:, u0:u0 + POOL_WIDTH])
    gl = jax.nn.sigmoid(_dot(a, wgl_ref[...]))
    for g in range(KV_GROUPS):
        gl_ref[g] = gl[:, g * LANES:(g + 1) * LANES]


def _in_proj(h, g, w_main, w_gl, S, tm=512):
    T = h.shape[0]
    return pl.pallas_call(
        functools.partial(_in_kernel, S=S),
        grid=(T // tm,),
        in_specs=[
            pl.BlockSpec((tm, D_MODEL), lambda i: (i, 0)),
            _const_spec((1, D_MODEL)),
            _const_spec((D_MODEL, _MAIN_COLS)),
            _const_spec((D_MODEL, KV_GROUPS * LANES)),
            _const_spec((tm, tm)),
        ],
        out_specs=[
            pl.BlockSpec((NSA_HEADS, tm, LANES), lambda i: (0, i, 0)),
            pl.BlockSpec((4, tm // CMP_STRIDE, CMP_STRIDE * HEAD_DIM), lambda i: (0, i, 0)),
            pl.BlockSpec((KV_GROUPS, tm, HEAD_DIM), lambda i: (0, i, 0)),
            pl.BlockSpec((KV_GROUPS, tm, LANES), lambda i: (0, i, 0)),
            pl.BlockSpec((4, tm, LANES), lambda i: (0, i, 0)),
            pl.BlockSpec((tm, POOL_WIDTH), lambda i: (i, 0)),
            pl.BlockSpec((KV_GROUPS, tm, LANES), lambda i: (0, i, 0)),
        ],
        out_shape=[
            jax.ShapeDtypeStruct((NSA_HEADS, T, LANES), BF16),
            jax.ShapeDtypeStruct((4, T // CMP_STRIDE, CMP_STRIDE * HEAD_DIM), BF16),
            jax.ShapeDtypeStruct((KV_GROUPS, T, HEAD_DIM), BF16),
            jax.ShapeDtypeStruct((KV_GROUPS, T, LANES), BF16),
            jax.ShapeDtypeStruct((4, T, LANES), BF16),
            jax.ShapeDtypeStruct((T, POOL_WIDTH), F32),
            jax.ShapeDtypeStruct((KV_GROUPS, T, LANES), F32),
        ],
        compiler_params=_params("parallel"),
        name="in_proj",
    )(h, g, w_main, w_gl, _chunk_permutation(tm))


def _chunk_permutation(tm):
    n_chunks = tm // CMP_STRIDE
    perm = np.zeros((tm, tm), np.float32)
    l, n = np.meshgrid(np.arange(CMP_STRIDE), np.arange(n_chunks), indexing="ij")
    perm[(l * n_chunks + n).ravel(), (CMP_STRIDE * n + l).ravel()] = 1.0
    return jnp.asarray(perm, dtype=BF16)


def _cmp_kernel(x_ref, pos_ref, w1_ref, w2_ref, kc_ref, vc_ref):
    n_chunks = x_ref.shape[1]
    for j in range(2):
        out_ref = kc_ref if j == 0 else vc_ref
        for g in range(KV_GROUPS):
            x = x_ref[j * KV_GROUPS + g].astype(F32)
            lo = (x + pos_ref[j, 0:1, :]).astype(BF16)
            hi = (x + pos_ref[j, 1:2, :]).astype(BF16)
            hid = _dot(lo, w1_ref[j, 0]) + pltpu.roll(_dot(hi, w1_ref[j, 1]), n_chunks - 1, axis=0)
            act = jax.nn.gelu(hid).astype(BF16)
            out_ref[0, g] = _dot(act, w2_ref[j]).astype(BF16)


def _compress(x, pos, w1, w2, B, S):
    nck = S // CMP_STRIDE
    half = CMP_STRIDE * HEAD_DIM
    out = jax.ShapeDtypeStruct((B, KV_GROUPS, nck, HEAD_DIM), BF16)
    return pl.pallas_call(
        _cmp_kernel,
        grid=(B,),
        in_specs=[
            pl.BlockSpec((4, nck, half), lambda b: (0, b, 0)),
            _const_spec((2, 2, half)),
            _const_spec((2, 2, half, CMP_HIDDEN)),
            _const_spec((2, CMP_HIDDEN, HEAD_DIM)),
        ],
        out_specs=[pl.BlockSpec((1, KV_GROUPS, nck, HEAD_DIM), lambda b: (b, 0, 0, 0))] * 2,
        out_shape=[out, out],
        compiler_params=_params("parallel"),
        name="compress",
    )(x, pos, w1, w2)


def _sel_kernel(q_ref, kc_ref, vc_ref, map_ref, ocmp_ref, unsel_ref, *, tq, n_blk):
    i = pl.program_id(2)
    kc = kc_ref[0, 0]
    vc = vc_ref[0, 0]
    nck = kc.shape[0]
    n_idx = lax.broadcasted_iota(jnp.int32, (nck, tq), 0)
    t_idx = i * tq + lax.broadcasted_iota(jnp.int32, (nck, tq), 1)
    valid = n_idx * CMP_STRIDE + (CMP_LEN - 1) <= t_idx
    sees_any = t_idx[0:1, :] >= CMP_LEN - 1
    psum = jnp.zeros((nck, tq), F32)
    for r in range(HEADS_PER_GROUP):
        s = jnp.where(valid, _dot_nt(kc, q_ref[r][:, :HEAD_DIM]), NEG_INF)
        e = jnp.exp2(s - jnp.max(s, axis=0, keepdims=True))
        p = e * jnp.where(sees_any, 1.0 / jnp.sum(e, axis=0, keepdims=True), 0.0)
        ocmp_ref[r] = _dot_tn(p.astype(BF16), vc)
        psum = psum + p
    imp = _dot(map_ref[...], psum, precision=lax.Precision.HIGHEST)
    blk = lax.broadcasted_iota(jnp.int32, (n_blk, tq), 0)
    t_blk = i * tq + lax.broadcasted_iota(jnp.int32, (n_blk, tq), 1)
    cur = lax.shift_right_logical(t_blk, SEL_BLOCK.bit_length() - 1)
    imp = jnp.where(blk > cur, -FORCE, imp)
    imp = jnp.where((blk == 0) | (blk == cur), FORCE, imp)
    rank = jnp.zeros((n_blk, tq), F32)
    for j in range(n_blk):
        row = imp[j:j + 1, :]
        ahead = (row > imp) | ((row == imp) & (blk > j))
        rank = rank + jnp.where(ahead, 1.0, 0.0)
    unsel = jnp.where(rank < SEL_TOPN, 0.0, 1.0)
    table = jnp.concatenate([jnp.zeros((HEAD_DIM, tq), F32), unsel,
                             jnp.zeros((LANES - HEAD_DIM - n_blk, tq), F32)], axis=0).astype(BF16)
    eye = jnp.where(lax.broadcasted_iota(jnp.int32, (LANES, LANES), 0)
                    == lax.broadcasted_iota(jnp.int32, (LANES, LANES), 1), 1.0, 0.0).astype(BF16)
    unsel_ref[0] = _dot_tn(table, eye).astype(BF16)


def _cmp_select(q8, kc, vc, cmp_map, B, S, tq=512):
    T = B * S
    n_blk = S // SEL_BLOCK
    nck = S // CMP_STRIDE
    nq = S // tq
    hp = HEADS_PER_GROUP
    return pl.pallas_call(
        functools.partial(_sel_kernel, tq=tq, n_blk=n_blk),
        grid=(B, KV_GROUPS, nq),
        in_specs=[
            pl.BlockSpec((hp, tq, LANES), lambda b, g, i: (g, b * nq + i, 0)),
            pl.BlockSpec((1, 1, nck, HEAD_DIM), lambda b, g, i: (b, g, 0, 0)),
            pl.BlockSpec((1, 1, nck, HEAD_DIM), lambda b, g, i: (b, g, 0, 0)),
            _const_spec((n_blk, nck)),
        ],
        out_specs=[
            pl.BlockSpec((hp, tq, HEAD_DIM), lambda b, g, i: (g, b * nq + i, 0)),
            pl.BlockSpec((1, tq, LANES), lambda b, g, i: (g, b * nq + i, 0)),
        ],
        out_shape=[
            jax.ShapeDtypeStruct((NSA_HEADS, T, HEAD_DIM), F32),
            jax.ShapeDtypeStruct((KV_GROUPS, T, LANES), BF16),
        ],
        compiler_params=_params("parallel", "parallel", "parallel"),
        name="cmp_select",
    )(q8, kc, vc, cmp_map)


def _attn_kernel(q_ref, unsel_ref, ks_ref, vs_ref, kw_ref, vw_ref, ocmp_ref, gl_ref, diag_ref, wint_ref,
                 o_ref, *, tq, tk, S):
    i = pl.program_id(2)
    hp = HEADS_PER_GROUP
    rows = hp * tq
    span = WINDOW + tq

    def sel_tile(q4, k0, width, causal=None, state=None):
        s = _dot_nt(q4, ks_ref[0, k0:k0 + width, :])
        if causal is not None:
            s = (s.reshape(hp, tq, width) + causal[None]).reshape(rows, width)
        m_new = jnp.broadcast_to(jnp.max(s, axis=-1, keepdims=True), (rows, LANES))
        if state is not None:
            m_new = jnp.maximum(state[0], m_new)
        p = jnp.exp2(s - jnp.concatenate([m_new] * (width // LANES), axis=1))
        pv = _dot(p.astype(BF16), vs_ref[0, k0:k0 + width, :])
        return m_new, pv if state is None else jnp.exp2(state[0] - m_new) * state[1] + pv

    def step(tile):
        n_full = (tile * tq) // tk
        diag_width = (tile % (tk // tq) + 1) * tq
        q4 = (q_ref[...] + unsel_ref[...]).reshape(rows, LANES)

        w0 = pl.multiple_of(jnp.clip(i * tq - WINDOW, 0, S - span), LANES)
        kw = kw_ref[0, pl.ds(w0, span), :]
        vw = vw_ref[0, pl.ds(w0, span), :]
        s = (_dot_nt(q4[:, :HEAD_DIM], kw).reshape(hp, tq, span) + wint_ref[0][None]).reshape(rows, span)
        p = jnp.exp2(s - jnp.max(s, axis=-1, keepdims=True))
        accw = _dot(p.astype(BF16), vw)
        o_win = accw[:, :HEAD_DIM] / accw[:, HEAD_DIM:HEAD_DIM + 1]
        gl = gl_ref[0]
        gated = [gl[:, 3 * r:3 * r + 1] * ocmp_ref[r] + gl[:, 3 * r + 2:3 * r + 3] * o_win[r * tq:(r + 1) * tq]
                 for r in range(hp)]

        state = sel_tile(q4, n_full * tk, diag_width, causal=diag_ref[0][:, :diag_width])
        for j in range(n_full // 2):
            state = sel_tile(q4, j * 2 * tk, 2 * tk, state=state)
        if n_full % 2:
            state = sel_tile(q4, (n_full - 1) * tk, tk, state=state)
        acc = state[1]
        o_sel = acc[:, :HEAD_DIM] / acc[:, HEAD_DIM:HEAD_DIM + 1]

        outs = [gated[r] + gl[:, 3 * r + 1:3 * r + 2] * o_sel[r * tq:(r + 1) * tq] for r in range(hp)]
        o_ref[...] = jnp.concatenate(outs, axis=-1).astype(BF16)

    for tile in range(S // tq):
        pl.when(i == tile)(functools.partial(step, tile))


def _attention_tables(S, tq, tk):
    a = np.arange(tq)[:, None]
    c = np.arange(tk)[None, :]
    diag = np.stack([np.where(c <= p * tq + a, 0.0, -MASK_BIG) for p in range(tk // tq)])
    span = WINDOW + tq
    win = []
    for i in range(WINDOW // tq + 1):
        t = i * tq + a
        kp = min(max(i * tq - WINDOW, 0), S - span) + np.arange(span)[None, :]
        win.append(np.where((kp <= t) & (kp > t - WINDOW), 0.0, -MASK_BIG))
    return jnp.asarray(diag.astype(np.float32)), jnp.asarray(np.stack(win).astype(np.float32))


def _sparse_attention(q8, unsel, ksx, kwin, vx, ocmp, gls, tables, B, S, tq, tk):
    diag, win = tables
    T = B * S
    nq = S // tq
    hp = HEADS_PER_GROUP
    span = WINDOW + tq
    n_win = WINDOW // tq
    q_tile = lambda width: pl.BlockSpec((hp, tq, width), lambda b, g, i: (g, b * nq + i, 0))
    group_tile = pl.BlockSpec((1, tq, LANES), lambda b, g, i: (g, b * nq + i, 0))
    seq = lambda first, width: pl.BlockSpec((1, S, width), lambda b, g, i: (first + g, b, 0))
    return pl.pallas_call(
        functools.partial(_attn_kernel, tq=tq, tk=tk, S=S),
        grid=(B, KV_GROUPS, nq),
        in_specs=[
            q_tile(LANES), group_tile,
            seq(0, LANES), seq(0, LANES), seq(0, HEAD_DIM), seq(2, LANES),
            q_tile(HEAD_DIM), group_tile,
            pl.BlockSpec((1, tq, tk), lambda b, g, i: (i % (tk // tq), 0, 0)),
            pl.BlockSpec((1, tq, span), lambda b, g, i: (jnp.minimum(i, n_win), 0, 0)),
        ],
        out_specs=pl.BlockSpec((tq, hp * HEAD_DIM), lambda b, g, i: (b * nq + i, g)),
        out_shape=jax.ShapeDtypeStruct((T, NSA_WIDTH), BF16),
        compiler_params=_params("parallel", "parallel", "parallel"),
        name="sparse_attention",
    )(q8, unsel, ksx, vx, kwin, vx, ocmp, gls, diag, win)


_POOL_HALO = 16


def _pool_mixer(u_ref, halo_ref, w_ref, scale_ref, tm, S):
    pos0 = (pl.program_id(0) * tm) % S
    halo = jnp.where(pos0 == 0, 0.0, halo_ref[...])
    x = jnp.concatenate([halo, u_ref[...]], axis=0)
    pos = pos0 + lax.broadcasted_iota(jnp.int32, (tm, 1), 0)
    out = []
    for gi, w in enumerate(POOL_WINDOWS):
        c0 = gi * POOL_GROUP_WIDTH
        xg = x[:, c0:c0 + POOL_GROUP_WIDTH]
        tot = xg[_POOL_HALO:_POOL_HALO + tm]
        for d in range(1, w):
            tot = tot + xg[_POOL_HALO - d:_POOL_HALO - d + tm]
        cnt = jnp.minimum(pos + 1, w).astype(F32)
        pooled = tot / cnt - xg[_POOL_HALO:_POOL_HALO + tm]
        mixed = _dot(pooled.astype(BF16), w_ref[gi])
        out.append((mixed * scale_ref[:, c0:c0 + POOL_GROUP_WIDTH]).astype(BF16))
    return jnp.concatenate(out, axis=-1)


_N_MERGE_REFS = 11


def _merge_residual(onsa_ref, u_ref, halo_ref, h_ref, gmix_ref, wmg_ref, poolw_ref, pscale_ref, wn_ref, wp_ref,
                    wo_ref, *, tm, S):
    h = h_ref[...]
    a = _rms(h, gmix_ref[...]).astype(BF16)
    nsa_up = _dot(onsa_ref[...], wn_ref[...])
    pool_up = _dot(_pool_mixer(u_ref, halo_ref, poolw_ref, pscale_ref, tm, S), wp_ref[...])
    merged = (jax.nn.sigmoid(_dot(a, wmg_ref[:, :D_MODEL])) * nsa_up
              + jax.nn.sigmoid(_dot(a, wmg_ref[:, D_MODEL:])) * pool_up)
    return h + _dot(merged.astype(BF16), wo_ref[...])


def _post_dense_kernel(*refs, tm, S, fc):
    mix_refs = refs[:_N_MERGE_REFS]
    gffn_ref, wg_ref, wu_ref, wd_ref, o_ref = refs[_N_MERGE_REFS:]
    h1 = _merge_residual(*mix_refs, tm=tm, S=S)
    f = _rms(h1, gffn_ref[...]).astype(BF16)
    o_ref[...] = h1
    for c in range(wg_ref.shape[1] // fc):
        g = _dot(f, wg_ref[:, c * fc:(c + 1) * fc])
        u = _dot(f, wu_ref[:, c * fc:(c + 1) * fc])
        o_ref[...] += _dot((jax.nn.silu(g) * u).astype(BF16), wd_ref[c * fc:(c + 1) * fc, :])


def _post_moe_kernel(*refs, tm, S):
    mix_refs = refs[:_N_MERGE_REFS]
    g_ref, wr_ref, br_ref, h1_ref, f_ref, route_ref, cnt_ref, base_sc = refs[_N_MERGE_REFS:]

    @pl.when(pl.program_id(0) == 0)
    def _():
        base_sc[...] = jnp.zeros(base_sc.shape, F32)

    h1 = _merge_residual(*mix_refs, tm=tm, S=S)
    h1_ref[...] = h1
    f = _rms(h1, g_ref[...])
    width = f_ref.shape[2]
    for p in range(f_ref.shape[0]):
        f_ref[p] = _pack_bf16_pairs(f[:, 2 * p * width:(2 * p + 1) * width],
                                    f[:, (2 * p + 1) * width:(2 * p + 2) * width])

    lane = lax.broadcasted_iota(jnp.int32, (tm, LANES), 1)
    f_hi = f.astype(BF16)
    f_lo = (f - f_hi.astype(F32)).astype(BF16)
    hi_terms = _dot(f_hi, wr_ref[...])
    logits = hi_terms[:, :LANES] + hi_terms[:, LANES:] + _dot(f_lo, wr_ref[:, :LANES]) + br_ref[...]
    lowest = jnp.finfo(F32).min
    lane_f = lane.astype(F32)
    lg = jnp.where(lane < N_EXPERTS, logits, lowest)
    m1 = jnp.max(lg, axis=-1, keepdims=True)
    i1 = jnp.min(jnp.where(lg == m1, lane_f, float(LANES)), axis=-1, keepdims=True)
    lg2 = jnp.where(lane_f == i1, lowest, lg)
    m2 = jnp.max(lg2, axis=-1, keepdims=True)
    i2 = jnp.min(jnp.where(lg2 == m2, lane_f, float(LANES)), axis=-1, keepdims=True)
    e2 = jnp.exp(m2 - m1)
    gate1 = 1.0 / (1.0 + e2)
    gate2 = e2 / (1.0 + e2)

    oh1 = lane_f == i1
    oh2 = lane_f == i2
    both = jnp.where(oh1 | oh2, 1.0, 0.0)
    r_i = lax.broadcasted_iota(jnp.int32, (tm, tm), 0)
    c_i = lax.broadcasted_iota(jnp.int32, (tm, tm), 1)
    before = jnp.where(r_i > c_i, 1.0, 0.0).astype(BF16)
    seen = _dot(before, both.astype(BF16)) + base_sc[...]
    rank1 = jnp.sum(jnp.where(oh1, seen, 0.0), axis=-1, keepdims=True)
    rank2 = jnp.sum(jnp.where(oh2, seen, 0.0), axis=-1, keepdims=True)
    base_sc[...] = base_sc[...] + jnp.sum(both, axis=0, keepdims=True)
    cnt_ref[...] = base_sc[...]

    route = jnp.zeros((tm, LANES), F32)
    for col, val in enumerate((i1, i2, gate1, gate2, rank1, rank2)):
        route = jnp.where(lane == col, val, route)
    route_ref[...] = route


def _post_mixer(onsa, u, h, mix_w, S, ffn=None, router=None, tm=512, fc=256):
    T = h.shape[0]
    g_mix, w_mg, pool_w, pool_scale, wn, wp, wo, g_ffn = mix_w
    tile = lambda w: pl.BlockSpec((tm, w), lambda i: (i, 0))
    halo_blocks = tm // _POOL_HALO
    in_specs = [
        tile(NSA_WIDTH), tile(POOL_WIDTH),
        pl.BlockSpec((_POOL_HALO, POOL_WIDTH), lambda i: (jnp.maximum(i * halo_blocks - 1, 0), 0)),
        tile(D_MODEL), _const_spec((1, D_MODEL)), _const_spec((D_MODEL, 2 * D_MODEL)),
        _const_spec((len(POOL_WINDOWS), POOL_GROUP_WIDTH, POOL_GROUP_WIDTH)), _const_spec((1, POOL_WIDTH)),
        _const_spec((NSA_WIDTH, D_MODEL)), _const_spec((POOL_WIDTH, D_MODEL)),
        _const_spec((D_MODEL, D_MODEL)), _const_spec((1, D_MODEL)),
    ]
    args = [onsa, u, u, h, g_mix, w_mg, pool_w, pool_scale, wn, wp, wo, g_ffn]
    if router is None:
        wg, wu, wd = ffn
        d_ff = wg.shape[1]
        return pl.pallas_call(
            functools.partial(_post_dense_kernel, tm=tm, S=S, fc=fc),
            grid=(T // tm,),
            in_specs=in_specs + [_const_spec((D_MODEL, d_ff)), _const_spec((D_MODEL, d_ff)),
                                 _const_spec((d_ff, D_MODEL))],
            out_specs=tile(D_MODEL),
            out_shape=jax.ShapeDtypeStruct((T, D_MODEL), F32),
            compiler_params=_params("parallel"),
            name="post_mixer_dense",
        )(*args, wg, wu, wd)
    w_router, b_router = router
    return pl.pallas_call(
        functools.partial(_post_moe_kernel, tm=tm, S=S),
        grid=(T // tm,),
        in_specs=in_specs + [_const_spec((D_MODEL, 2 * LANES)), _const_spec((1, LANES))],
        out_specs=[tile(D_MODEL), pl.BlockSpec((N_PLANES, tm, PLANE_WIDTH), lambda i: (0, i, 0)), tile(LANES),
                   pl.BlockSpec((1, LANES), lambda i: (0, 0))],
        out_shape=[
            jax.ShapeDtypeStruct((T, D_MODEL), F32),
            jax.ShapeDtypeStruct((N_PLANES, T, PLANE_WIDTH), jnp.uint32),
            jax.ShapeDtypeStruct((T, LANES), F32),
            jax.ShapeDtypeStruct((1, LANES), F32),
        ],
        scratch_shapes=[pltpu.VMEM((1, LANES), F32)],
        compiler_params=_params("arbitrary"),
        name="post_mixer_moe",
    )(*args, w_router, b_router)


PLANE_WIDTH = 256
N_PLANES = D_MODEL // (2 * PLANE_WIDTH)
SC_WINDOW = 128


def _sc_mesh():
    return plsc.VectorSubcoreMesh(core_axis_name="core", subcore_axis_name="subcore")


def _pack_bf16_pairs(lo, hi):
    lo_bits = lax.bitcast_convert_type(lo.astype(BF16).astype(F32), jnp.uint32)
    hi_bits = lax.bitcast_convert_type(hi.astype(BF16).astype(F32), jnp.uint32)
    return (lo_bits >> 16) | (hi_bits & jnp.uint32(0xFFFF0000))


def _unpack_bf16_pairs(words):
    lo = lax.bitcast_convert_type(words << 16, F32)
    hi = lax.bitcast_convert_type(words & jnp.uint32(0xFFFF0000), F32)
    return lo, hi


def _sc_scatter_pair(rows, idx_a, idx_b, n_out):
    n_rows, width = rows.shape

    @pl.kernel(out_type=jax.ShapeDtypeStruct((n_out, width), rows.dtype), mesh=_sc_mesh(), scratch_types=[])
    def scatter(rows_hbm, a_hbm, b_hbm, out_hbm):
        def body(rows_vmem, a_vmem, b_vmem):
            pltpu.sync_copy(rows_vmem, out_hbm.at[a_vmem.at[0]])
            pltpu.sync_copy(rows_vmem, out_hbm.at[b_vmem.at[0]])

        idx_spec = pl.BlockSpec((1, SC_WINDOW), lambda i: (0, i))
        pltpu.emit_pipeline(
            body,
            grid=(n_rows // SC_WINDOW,),
            in_specs=[pl.BlockSpec((SC_WINDOW, width), lambda i: (i, 0)), idx_spec, idx_spec],
            out_specs=[],
            core_axis_name=("core", "subcore"),
            dimension_semantics=(pltpu.PARALLEL,),
        )(rows_hbm, a_hbm, b_hbm)

    return scatter(rows, idx_a, idx_b)


def _sc_gather(rows, idx):
    n_out = idx.shape[1]
    width = rows.shape[1]

    @pl.kernel(out_type=jax.ShapeDtypeStruct((n_out, width), rows.dtype), mesh=_sc_mesh(), scratch_types=[])
    def gather(rows_hbm, idx_hbm, out_hbm):
        def body(idx_vmem, out_vmem):
            pltpu.sync_copy(rows_hbm.at[idx_vmem.at[0]], out_vmem)

        pltpu.emit_pipeline(
            body,
            grid=(n_out // SC_WINDOW,),
            in_specs=[pl.BlockSpec((1, SC_WINDOW), lambda i: (0, i))],
            out_specs=[pl.BlockSpec((SC_WINDOW, width), lambda i: (i, 0))],
            core_axis_name=("core", "subcore"),
            dimension_semantics=(pltpu.PARALLEL,),
        )(idx_hbm, out_hbm)

    return gather(rows, idx)


def _dispatch(f_planes, dest):
    n_planes, T, width = f_planes.shape
    n_rows = 2 * T
    plane_base = (jnp.arange(n_planes, dtype=jnp.int32) * n_rows)[:, None]
    idx = [(plane_base + dest[:, k][None, :]).reshape(1, n_planes * T) for k in range(2)]
    out = _sc_scatter_pair(f_planes.reshape(n_planes * T, width), idx[0], idx[1], n_planes * n_rows)
    return out.reshape(n_planes, n_rows, width)


def _undispatch(y_planes, dest):
    n_planes, n_rows, width = y_planes.shape
    T = dest.shape[0]
    plane_base = (jnp.arange(n_planes, dtype=jnp.int32) * n_rows)[None, :, None]
    idx = (plane_base + dest.T[:, None, :]).reshape(1, 2 * n_planes * T)
    out = _sc_gather(y_planes.reshape(n_planes * n_rows, width), idx)
    return out.reshape(2, n_planes, T, width)


def _combine_kernel(h_ref, route_ref, z_ref, g_ref, o_ref, *, final_norm):
    route = route_ref[...]
    z = [jnp.concatenate([half for p in range(z_ref.shape[1]) for half in _unpack_bf16_pairs(z_ref[k, p])], axis=-1)
         for k in range(2)]
    out = h_ref[...] + (route[:, 2:3] * z[0] + route[:, 3:4] * z[1])
    if final_norm:
        out = _rms(out, g_ref[...])
    o_ref[...] = out


def _combine(h1, route, z, g_final, final_norm, tg=512):
    T = h1.shape[0]
    return pl.pallas_call(
        functools.partial(_combine_kernel, final_norm=final_norm),
        grid=(T // tg,),
        in_specs=[
            pl.BlockSpec((tg, D_MODEL), lambda i: (i, 0)),
            pl.BlockSpec((tg, LANES), lambda i: (i, 0)),
            pl.BlockSpec((2, z.shape[1], tg, z.shape[3]), lambda i: (0, 0, i, 0)),
            _const_spec((1, D_MODEL)),
        ],
        out_specs=pl.BlockSpec((tg, D_MODEL), lambda i: (i, 0)),
        out_shape=jax.ShapeDtypeStruct((T, D_MODEL), F32),
        compiler_params=_params("parallel"),
        name="moe_combine",
    )(h1, route, z, g_final)


def _moe_kernel(blk_ref, exp_ref, lo_ref, hi_ref, first_ref, x_ref, wg_ref, wu_ref, wd_ref, y_ref, acc_sc, *, bm):
    v = pl.program_id(0)
    c = pl.program_id(1)

    @pl.when((first_ref[v] == 1) & (c == 0))
    def _():
        acc_sc[...] = jnp.zeros(acc_sc.shape, F32)

    lo = lo_ref[v]
    hi = hi_ref[v]

    @pl.when(hi > lo)
    def _():
        row = blk_ref[v] * bm + lax.broadcasted_iota(jnp.int32, (bm, 1), 0)
        x = jnp.concatenate([half for p in range(x_ref.shape[0]) for half in _unpack_bf16_pairs(x_ref[p])], axis=-1)
        x = jnp.where((row >= lo) & (row < hi), x, 0.0).astype(BF16)
        g = _dot(x, wg_ref[0].astype(BF16))
        u = _dot(x, wu_ref[0].astype(BF16))
        acc_sc[...] += _dot((jax.nn.silu(g) * u).astype(BF16), wd_ref[0].astype(BF16))

    @pl.when(c == pl.num_programs(1) - 1)
    def _():
        width = y_ref.shape[2]
        for p in range(y_ref.shape[0]):
            y_ref[p] = _pack_bf16_pairs(acc_sc[:, 2 * p * width:(2 * p + 1) * width],
                                        acc_sc[:, (2 * p + 1) * width:(2 * p + 2) * width])


def _grouped_ffn(x_sorted, visits, wg, wu, wd, layer, bm=1024, fc=512):
    n_planes, n_rows, width = x_sorted.shape
    n_visits = visits[0].shape[0]
    n_exp, _, d_ff = wg.shape[1:]
    first = layer * n_exp
    flat = lambda w: w.reshape((w.shape[0] * n_exp,) + w.shape[2:])
    rows_spec = pl.BlockSpec((n_planes, bm, width), lambda v, c, blk, ex, lo, hi, fi: (0, blk[v], 0))
    grid_spec = pltpu.PrefetchScalarGridSpec(
        num_scalar_prefetch=5,
        grid=(n_visits, d_ff // fc),
        in_specs=[
            rows_spec,
            pl.BlockSpec((1, D_MODEL, fc), lambda v, c, blk, ex, lo, hi, fi: (first + ex[v], 0, c)),
            pl.BlockSpec((1, D_MODEL, fc), lambda v, c, blk, ex, lo, hi, fi: (first + ex[v], 0, c)),
            pl.BlockSpec((1, fc, D_MODEL), lambda v, c, blk, ex, lo, hi, fi: (first + ex[v], c, 0)),
        ],
        out_specs=rows_spec,
        scratch_shapes=[pltpu.VMEM((bm, D_MODEL), F32)],
    )
    return pl.pallas_call(
        functools.partial(_moe_kernel, bm=bm),
        grid_spec=grid_spec,
        out_shape=jax.ShapeDtypeStruct((n_planes, n_rows, width), jnp.uint32),
        compiler_params=_params("arbitrary", "arbitrary"),
        name="grouped_ffn",
    )(*visits, x_sorted, flat(wg), flat(wu), flat(wd))


def _routing_tables(route, counts, n_rows, bm):
    counts = counts[0, :N_EXPERTS].astype(jnp.int32)
    ends = jnp.cumsum(counts)
    starts = ends - counts
    e = route[:, 0:2].astype(jnp.int32)
    seg_start = jnp.sum(jnp.where(e[..., None] == jnp.arange(N_EXPERTS), starts, 0), axis=-1)
    dest = seg_start + route[:, 4:6].astype(jnp.int32)
    n_blocks = n_rows // bm
    cuts = jnp.sort(jnp.concatenate([jnp.arange(n_blocks, dtype=jnp.int32) * bm, starts[1:]]))
    lo = cuts
    hi = jnp.concatenate([cuts[1:], jnp.array([n_rows], jnp.int32)])
    blk = jnp.minimum(lo // bm, n_blocks - 1)
    ex = jnp.minimum(jnp.sum(ends[None, :] <= lo[:, None], axis=1), N_EXPERTS - 1).astype(jnp.int32)
    first = jnp.concatenate([jnp.ones((1,), jnp.int32), (blk[1:] != blk[:-1]).astype(jnp.int32)])
    return dest, (blk, ex, lo, hi, first)


def _layer_weights(w_in, cmp_pos, cmp_w1, cmp_w2):
    w_main = jnp.concatenate([w_in[:, _Q0:_GL0], w_in[:, _U0:_MG0]], axis=1).astype(BF16)
    w_mg = w_in[:, _MG0:].astype(BF16)
    gl = w_in[:, _GL0:_U0].reshape(D_MODEL, KV_GROUPS, _GATES_PER_GROUP)
    w_gl = jnp.pad(gl, ((0, 0), (0, 0), (0, LANES - _GATES_PER_GROUP))).reshape(D_MODEL, KV_GROUPS * LANES)
    half = CMP_STRIDE * HEAD_DIM
    pos = cmp_pos.reshape(2, 2, half)
    w1 = cmp_w1.reshape(2, 2, half, CMP_HIDDEN).astype(BF16)
    return w_main, w_mg, w_gl.astype(BF16), pos, w1, cmp_w2.astype(BF16)


def _cmp_to_sel_map(S):
    n_blk = S // SEL_BLOCK
    n_cmp = (S - CMP_LEN) // CMP_STRIDE + 1
    nck = S // CMP_STRIDE
    cs = np.arange(nck) * CMP_STRIDE
    bs = np.arange(n_blk) * SEL_BLOCK
    overlap = (cs[None, :] <= bs[:, None] + SEL_BLOCK - 1) & (cs[None, :] + CMP_LEN - 1 >= bs[:, None])
    overlap &= (np.arange(nck) < n_cmp)[None, :]
    return jnp.asarray(overlap.astype(np.float32))


def kernel(x, norm_mix_g, w_in, cmp_pos, cmp_w1, cmp_w2, w_nsa_up, pool_w, pool_scale, w_pool_up, w_out,
           norm_ffn_g, ffn_w_gate, ffn_w_up, ffn_w_down, moe_w_router, moe_b_router, moe_w_gate, moe_w_up,
           moe_w_down, final_norm_g):
    B, S, D = x.shape
    T = B * S
    depth = w_in.shape[0]
    cmp_map = _cmp_to_sel_map(S)
    attn_tq, attn_tk = 256, 512
    attn_tables = _attention_tables(S, attn_tq, attn_tk)
    h = x.reshape(T, D)
    bm = 1024
    for i in range(depth):
        w_main, w_mg, w_gl, pos, w1, w2 = _layer_weights(w_in[i], cmp_pos[i], cmp_w1[i], cmp_w2[i])
        q8, xc, kwin, ksx, vx, u, gls = _in_proj(h, norm_mix_g[i][None], w_main, w_gl, S)
        kc, vc = _compress(xc, pos, w1, w2, B, S)
        ocmp, unsel = _cmp_select(q8, kc, vc, cmp_map, B, S)
        onsa = _sparse_attention(q8, unsel, ksx, kwin, vx, ocmp, gls, attn_tables, B, S, attn_tq, attn_tk)
        mix_w = (norm_mix_g[i][None], w_mg, pool_w[i].astype(BF16), pool_scale[i][None], w_nsa_up[i].astype(BF16),
                 w_pool_up[i].astype(BF16), w_out[i].astype(BF16), norm_ffn_g[i][None])
        j = i // 2
        if i % 2 == 0:
            ffn = (ffn_w_gate[j].astype(BF16), ffn_w_up[j].astype(BF16), ffn_w_down[j].astype(BF16))
            h = _post_mixer(onsa, u, h, mix_w, S, ffn=ffn)
        else:
            w_router = jnp.pad(moe_w_router[j], ((0, 0), (0, LANES - N_EXPERTS)))
            w_router_hi = w_router.astype(BF16)
            w_router_lo = (w_router - w_router_hi.astype(F32)).astype(BF16)
            w_router = jnp.concatenate([w_router_hi, w_router_lo], axis=1)
            b_router = jnp.pad(moe_b_router[j], (0, LANES - N_EXPERTS))[None]
            h1, f, route, counts = _post_mixer(onsa, u, h, mix_w, S, router=(w_router, b_router))
            dest, visits = _routing_tables(route, counts, 2 * T, bm)
            x_sorted = _dispatch(f, dest)
            y_sorted = _grouped_ffn(x_sorted, visits, moe_w_gate, moe_w_up, moe_w_down, j, bm=bm)
            z = _undispatch(y_sorted, dest)
            h = _combine(h1, route, z, final_norm_g[None], final_norm=(i == depth - 1))
    return h.reshape(B, S, D)
```
